```python
import math
import jax
import jax.numpy as jnp
from jax import lax
import numpy as np

D_MODEL = 1024
BATCH = 4
SEQ = 8192
DEPTH = 2

CTX_LEN = 256
GRID_W = 64
HEAD_DIM = 64
N_MIXERS = 4
GROUP_WIDTH = D_MODEL // N_MIXERS
GROUP_HEADS = GROUP_WIDTH // HEAD_DIM
Q_BLOCK = 128
MLA_NOPE = HEAD_DIM
MLA_ROPE = HEAD_DIM // 2
MLA_V = HEAD_DIM
MLA_KV_RANK = D_MODEL // 8
SWA_KV_HEADS = GROUP_HEADS // 2
SWA_WINDOW = 128
SWA_BLOCK = 128
NA_KH = 8
NA_KW = 16
NA_QB_W = 16
NA_KB_W = NA_QB_W + NA_KW
RET_CHUNK = 128
PEER_HEADS = 8
PEER_NKEYS = 128
PEER_EXPERTS = PEER_NKEYS * PEER_NKEYS
PEER_TOPK = 16
PEER_DKEY = 256
PEER_TOKEN_BLOCK = 128
ADA_CHUNKS = 6
ROPE_BASE = 10000.0
EPS = 1e-6
NEG_INF = -1e30

IN_SPLITS = (
    GROUP_HEADS * (MLA_NOPE + MLA_ROPE),
    MLA_KV_RANK,
    MLA_ROPE,
    GROUP_HEADS * HEAD_DIM,
    SWA_KV_HEADS * HEAD_DIM,
    SWA_KV_HEADS * HEAD_DIM,
    GROUP_WIDTH,
    GROUP_WIDTH,
    GROUP_WIDTH,
    GROUP_WIDTH,
    GROUP_WIDTH,
    GROUP_WIDTH,
    GROUP_WIDTH,
    GROUP_WIDTH,
)
IN_WIDTH = sum(IN_SPLITS)

kernel_name = "hybrid_parallel_heads_peer_flow_block"


def rmsnorm(x, g):
    xf = x.astype(jnp.float32)
    y = xf * lax.rsqrt(jnp.mean(xf * xf, axis=-1, keepdims=True) + EPS)
    return (y * g.astype(jnp.float32)).astype(x.dtype)


def modulate(h, shift, scale):
    return h * (1.0 + scale[:, None, :]) + shift[:, None, :]


def split_heads(t, n):
    b, s, _ = t.shape
    return t.reshape(b, s, n, -1).transpose(0, 2, 1, 3)


def merge_heads(t):
    b, n, s, d = t.shape
    return t.transpose(0, 2, 1, 3).reshape(b, s, n * d)


def softmax_f32(logits):
    return jax.nn.softmax(logits.astype(jnp.float32), axis=-1)


def axial_rope_tables(n_tok, rot_dim):
    quarter = rot_dim // 4
    inv = ROPE_BASE ** (-jnp.arange(quarter, dtype=jnp.float32) / quarter)
    t = jnp.arange(n_tok, dtype=jnp.int32)
    pos = jnp.stack([t // GRID_W, t % GRID_W], axis=-1).astype(jnp.float32)
    ang = pos[:, :, None] * inv
    return jnp.cos(ang), jnp.sin(ang)


def apply_axial_rope(x, cos, sin):
    shp = x.shape
    quarter = shp[-1] // 4
    xf = x.astype(jnp.float32).reshape(shp[:-1] + (2, 2, quarter))
    x1, x2 = xf[..., 0, :], xf[..., 1, :]
    out = jnp.stack([x1 * cos - x2 * sin, x1 * sin + x2 * cos], axis=-2)
    return out.reshape(shp).astype(x.dtype)


def dense_ctx_attention(q, k, v, scale, sink=None):
    s = jnp.einsum("bhqd,bhkd->bhqk", q, k).astype(jnp.float32) * scale
    if sink is not None:
        sink_col = jnp.broadcast_to(sink.astype(jnp.float32)[None, :, None, None], s.shape[:-1] + (1,))
        p = softmax_f32(jnp.concatenate([s, sink_col], axis=-1))[..., :-1]
    else:
        p = softmax_f32(s)
    return jnp.einsum("bhqk,bhkd->bhqd", p.astype(v.dtype), v)


def mla_kv(ckv, kv_norm_g, w_ukv):
    kv = split_heads(rmsnorm(ckv, kv_norm_g) @ w_ukv, GROUP_HEADS)
    return kv[..., :MLA_NOPE], kv[..., MLA_NOPE:]


def mla_mixer(q_l, ckv_l, kr_l, q_c, ckv_c, kr_c, kv_norm_g, w_ukv, cos, sin, with_ctx):
    h = GROUP_HEADS
    scale = (MLA_NOPE + MLA_ROPE) ** -0.5
    kn_l, v_l = mla_kv(ckv_l, kv_norm_g, w_ukv)
    kn_c, v_c = mla_kv(ckv_c, kv_norm_g, w_ukv)
    kr_lr = apply_axial_rope(kr_l, cos, sin)
    q = split_heads(q_l, h)
    qn, qr = q[..., :MLA_NOPE], apply_axial_rope(q[..., MLA_NOPE:], cos, sin)
    kn = jnp.concatenate([kn_c, kn_l], axis=2)
    kr = jnp.concatenate([kr_c, kr_lr], axis=1)
    v = jnp.concatenate([v_c, v_l], axis=2)
    b, _, n, _ = qn.shape
    nb = n // Q_BLOCK

    def block(qs):
        qn_b, qr_b = qs
        s = (jnp.einsum("bhqd,bhkd->bhqk", qn_b, kn)
             + jnp.einsum("bhqd,bkd->bhqk", qr_b, kr)).astype(jnp.float32) * scale
        p = softmax_f32(s)
        return jnp.einsum("bhqk,bhkd->bhqd", p.astype(v.dtype), v)

    to_blocks = lambda t: t.reshape(b, h, nb, Q_BLOCK, t.shape[-1]).transpose(2, 0, 1, 3, 4)
    o = lax.map(block, (to_blocks(qn), to_blocks(qr)))
    y_l = merge_heads(o.transpose(1, 2, 0, 3, 4).reshape(b, h, n, MLA_V))
    y_c = None
    if with_ctx:
        qc = split_heads(q_c, h)
        kc = jnp.concatenate([kn_c, jnp.broadcast_to(kr_c[:, None], kn_c.shape[:3] + (MLA_ROPE,))], axis=-1)
        y_c = merge_heads(dense_ctx_attention(qc, kc, v_c, scale))
    return y_l, y_c


def swa_mixer(q_l, k_l, v_l, q_c, k_c, v_c, sink, cos, sin, with_ctx):
    h, g = GROUP_HEADS, SWA_KV_HEADS
    r = h // g
    d = HEAD_DIM
    scale = d ** -0.5
    q = apply_axial_rope(split_heads(q_l, h), cos, sin)
    k = apply_axial_rope(split_heads(k_l, g), cos, sin)
    v = split_heads(v_l, g)
    kc, vc = split_heads(k_c, g), split_heads(v_c, g)
    b, _, n, _ = q.shape
    nb = n // SWA_BLOCK
    qb = q.reshape(b, g, r, nb, SWA_BLOCK, d)

    def band(t):
        tb = t.reshape(b, g, nb, SWA_BLOCK, d)
        tp = jnp.pad(tb, ((0, 0), (0, 0), (1, 1), (0, 0), (0, 0)))
        return jnp.concatenate([tp[:, :, :-2], tp[:, :, 1:-1], tp[:, :, 2:]], axis=3)

    kband, vband = band(k), band(v)
    nk = 3 * SWA_BLOCK
    rel = np.arange(nk)[None, :] - SWA_BLOCK - np.arange(SWA_BLOCK)[:, None]
    kpos = (np.arange(nb)[:, None] - 1) * SWA_BLOCK + np.arange(nk)[None, :]
    mask = (np.abs(rel) <= SWA_WINDOW)[None] & ((kpos >= 0) & (kpos < n))[:, None, :]
    s_band = jnp.einsum("bgrnqd,bgnkd->bgrnqk", qb, kband).astype(jnp.float32) * scale
    s_band = jnp.where(mask, s_band, NEG_INF)
    s_ctx = jnp.einsum("bgrnqd,bgkd->bgrnqk", qb, kc).astype(jnp.float32) * scale
    sink_col = jnp.broadcast_to(sink.astype(jnp.float32).reshape(1, g, r, 1, 1, 1), s_ctx.shape[:-1] + (1,))
    p = softmax_f32(jnp.concatenate([s_band, s_ctx, sink_col], axis=-1))
    n_ctx = kc.shape[2]
    o = (jnp.einsum("bgrnqk,bgnkd->bgrnqd", p[..., :nk].astype(v.dtype), vband)
         + jnp.einsum("bgrnqk,bgkd->bgrnqd", p[..., nk:nk + n_ctx].astype(v.dtype), vc))
    y_l = merge_heads(o.reshape(b, h, n, d))
    y_c = None
    if with_ctx:
        y_c = merge_heads(dense_ctx_attention(split_heads(q_c, h), jnp.repeat(kc, r, axis=1),
                                              jnp.repeat(vc, r, axis=1), scale, sink))
    return y_l, y_c


def na_mixer(q_l, k_l, v_l, q_c, k_c, v_c, rpb, with_ctx):
    h, d = GROUP_HEADS, HEAD_DIM
    scale = d ** -0.5
    q, k, v = split_heads(q_l, h), split_heads(k_l, h), split_heads(v_l, h)
    kc, vc = split_heads(k_c, h), split_heads(v_c, h)
    b, _, n, _ = q.shape
    rows = n // GRID_W
    kh = min(NA_KH, rows)
    nj = GRID_W // NA_QB_W
    qcol = np.arange(GRID_W).reshape(nj, NA_QB_W)
    cs = np.clip(qcol - NA_KW // 2, 0, GRID_W - NA_KW)
    cb = np.minimum(cs[:, 0], GRID_W - NA_KB_W)
    kcol = cb[:, None] + np.arange(NA_KB_W)[None, :]
    col_mask = (kcol[:, None, :] >= cs[:, :, None]) & (kcol[:, None, :] < cs[:, :, None] + NA_KW)
    dc_idx = np.clip(kcol[:, None, :] - qcol[:, :, None] + NA_KW - 1, 0, 2 * NA_KW - 2).astype(np.int32)
    r_idx = np.arange(rows)
    rs = np.clip(r_idx - kh // 2, 0, rows - kh)
    dr_idx = (rs[:, None] + np.arange(kh)[None, :] - r_idx[:, None] + NA_KH - 1).astype(np.int32)
    kg = k.reshape(b, h, rows, GRID_W, d)
    vg = v.reshape(b, h, rows, GRID_W, d)
    qg = q.reshape(b, h, rows, GRID_W, d).transpose(2, 0, 1, 3, 4)
    rpb_f = rpb.astype(jnp.float32)
    nk = kh * NA_KB_W

    def row_block(inp):
        q_row, r0, dri = inp
        k_win = lax.dynamic_slice_in_dim(kg, r0, kh, axis=2)[:, :, :, kcol]
        v_win = lax.dynamic_slice_in_dim(vg, r0, kh, axis=2)[:, :, :, kcol]
        qb = q_row.reshape(b, h, nj, NA_QB_W, d)
        s = jnp.einsum("bhjqd,bhajkd->bhjqak", qb, k_win).astype(jnp.float32) * scale
        bias = rpb_f[:, dri][:, :, dc_idx].transpose(0, 2, 3, 1, 4)
        s = jnp.where(col_mask[:, :, None, :], s + bias, NEG_INF).reshape(b, h, nj, NA_QB_W, nk)
        s_ctx = jnp.einsum("bhjqd,bhkd->bhjqk", qb, kc).astype(jnp.float32) * scale
        p = softmax_f32(jnp.concatenate([s, s_ctx], axis=-1))
        p_win = p[..., :nk].reshape(b, h, nj, NA_QB_W, kh, NA_KB_W).astype(v.dtype)
        o = (jnp.einsum("bhjqak,bhajkd->bhjqd", p_win, v_win)
             + jnp.einsum("bhjqk,bhkd->bhjqd", p[..., nk:].astype(v.dtype), vc))
        return o.reshape(b, h, GRID_W, d)

    o = lax.map(row_block, (qg, jnp.asarray(rs, jnp.int32), jnp.asarray(dr_idx)))
    y_l = merge_heads(o.transpose(1, 2, 0, 3, 4).reshape(b, h, n, d))
    y_c = None
    if with_ctx:
        y_c = merge_heads(dense_ctx_attention(split_heads(q_c, h), kc, vc, scale))
    return y_l, y_c


def retention_scan(q, k, v, log_gamma, state0, emit):
    b, h, t, _ = q.shape
    c = RET_CHUNK
    nc = t // c
    idx = jnp.arange(c, dtype=jnp.float32)
    diff = idx[:, None] - idx[None, :]
    lg = log_gamma[:, None, None]
    decay_in = jnp.where(diff >= 0, jnp.exp(lg * jnp.maximum(diff, 0.0)), 0.0)
    q_dec = jnp.exp(log_gamma[:, None] * (idx + 1.0))[..., None]
    k_dec = jnp.exp(log_gamma[:, None] * (c - 1.0 - idx))[..., None]
    c_dec = jnp.exp(log_gamma * c)[:, None, None]
    chunks = lambda a: a.reshape(b, h, nc, c, a.shape[-1]).transpose(2, 0, 1, 3, 4)

    def step(s, inp):
        qb, kb, vb = inp
        s_new = s * c_dec + jnp.einsum("bhjd,bhje->bhde", kb * k_dec, vb)
        if emit:
            inner = jnp.einsum("bhid,bhjd->bhij", qb, kb) * decay_in
            o = jnp.einsum("bhij,bhjd->bhid", inner, vb) + jnp.einsum("bhid,bhde->bhie", qb * q_dec, s)
            return s_new, o
        return s_new, None

    s_fin, o = lax.scan(step, state0, (chunks(q), chunks(k), chunks(v)))
    if emit:
        o = o.transpose(1, 2, 0, 3, 4).reshape(b, h, t, -1)
    return o, s_fin


def head_groupnorm(o, g):
    mu = jnp.mean(o, axis=-1, keepdims=True)
    var = jnp.mean(jnp.square(o - mu), axis=-1, keepdims=True)
    return merge_heads((o - mu) * lax.rsqrt(var + EPS)) * g.astype(jnp.float32)


def retention_gate(o_f, o_b, g_f, g_b, gn_f, gn_b):
    y = (jax.nn.silu(g_f.astype(jnp.float32)) * head_groupnorm(o_f, gn_f)
         + jax.nn.silu(g_b.astype(jnp.float32)) * head_groupnorm(o_b, gn_b))
    return y.astype(g_f.dtype)


def retention_mixer(q_l, k_l, v_l, gf_l, gb_l, q_c, k_c, v_c, gf_c, gb_c,
                    decay_f, decay_b, gn_f, gn_b, with_ctx):
    prep = lambda t: split_heads(t, GROUP_HEADS).astype(jnp.float32)
    kscale = HEAD_DIM ** -0.5
    ql, kl, vl = prep(q_l), prep(k_l) * kscale, prep(v_l)
    qc, kc, vc = prep(q_c), prep(k_c) * kscale, prep(v_c)
    lg_f = jax.nn.log_sigmoid(decay_f.astype(jnp.float32))
    lg_b = jax.nn.log_sigmoid(decay_b.astype(jnp.float32))
    zero = jnp.zeros(kl.shape[:2] + (HEAD_DIM, HEAD_DIM), jnp.float32)
    flip = lambda t: t[:, :, ::-1]
    oc_f, s_f = retention_scan(qc, kc, vc, lg_f, zero, with_ctx)
    oc_b, s_b = retention_scan(flip(qc), flip(kc), flip(vc), lg_b, zero, with_ctx)
    ol_f, _ = retention_scan(ql, kl, vl, lg_f, s_f, True)
    ol_b, _ = retention_scan(flip(ql), flip(kl), flip(vl), lg_b, s_b, True)
    y_l = retention_gate(ol_f, flip(ol_b), gf_l, gb_l, gn_f, gn_b)
    y_c = None
    if with_ctx:
        y_c = retention_gate(oc_f, flip(oc_b), gf_c, gb_c, gn_f, gn_b)
    return y_l, y_c


def peer_ffn(hid, wq, sub_k1, sub_k2, expert_u, expert_v):
    b, t, d = hid.shape
    flat = hid.reshape(b * t, d)
    blk = math.gcd(b * t, PEER_TOKEN_BLOCK)
    half = PEER_DKEY // 2

    def block(xb):
        tb = xb.shape[0]
        q = (xb @ wq).reshape(tb, PEER_HEADS, 2, half).astype(jnp.float32)
        s1 = jnp.einsum("thd,nd->thn", q[:, :, 0], sub_k1.astype(jnp.float32))
        s2 = jnp.einsum("thd,nd->thn", q[:, :, 1], sub_k2.astype(jnp.float32))
        v1, i1 = lax.top_k(s1, PEER_TOPK)
        v2, i2 = lax.top_k(s2, PEER_TOPK)
        cand_s = (v1[..., :, None] + v2[..., None, :]).reshape(tb, PEER_HEADS, PEER_TOPK * PEER_TOPK)
        cand_i = (i1[..., :, None] * PEER_NKEYS + i2[..., None, :]).reshape(tb, PEER_HEADS, PEER_TOPK * PEER_TOPK)
        top_s, pos = lax.top_k(cand_s, PEER_TOPK)
        eid = jnp.take_along_axis(cand_i, pos, axis=-1).reshape(tb, PEER_HEADS * PEER_TOPK)
        gate = jax.nn.softmax(top_s, axis=-1).reshape(tb, PEER_HEADS * PEER_TOPK)
        a = jnp.einsum("td,tkd->tk", xb, expert_u[eid]).astype(jnp.float32)
        w = (jax.nn.gelu(a, approximate=False) * gate).astype(xb.dtype)
        return jnp.einsum("tk,tkd->td", w, expert_v[eid])

    out = lax.map(block, flat.reshape(-1, blk, d))
    return out.reshape(b, t, d)


def setup_inputs(seed: int = 0) -> dict:
    key = jax.random.key(seed)
    ks = jax.random.split(key, 26)
    f32 = jnp.float32
    L, D, H = DEPTH, D_MODEL, GROUP_HEADS
    nrm = lambda k, shape, s: jax.random.normal(k, shape, f32) * s
    gain = lambda k, shape: 1.0 + 0.02 * jax.random.normal(k, shape, f32)
    decay0 = jnp.asarray(np.log(2.0 ** (5 + np.arange(H)) - 1.0), f32)
    return {
        "x": nrm(ks[0], (BATCH, SEQ, D), 1.0),
        "c": nrm(ks[1], (BATCH, D), 1.0),
        "ctx": nrm(ks[2], (BATCH, CTX_LEN, D), 1.0),
        "c_ctx": nrm(ks[3], (D,), 1.0),
        "ada_w": nrm(ks[4], (L, D, ADA_CHUNKS * D), 0.5 * D ** -0.5),
        "ada_b": nrm(ks[5], (L, ADA_CHUNKS * D), 0.01),
        "norm_mix_g": gain(ks[6], (L, D)),
        "w_in": nrm(ks[7], (L, D, IN_WIDTH), D ** -0.5),
        "mla_kv_norm_g": gain(ks[8], (L, MLA_KV_RANK)),
        "mla_w_ukv": nrm(ks[9], (L, MLA_KV_RANK, H * (MLA_NOPE + MLA_V)), MLA_KV_RANK ** -0.5),
        "swa_sink": nrm(ks[10], (L, H), 0.5),
        "na_rpb": nrm(ks[11], (L, H, 2 * NA_KH - 1, 2 * NA_KW - 1), 0.1),
        "ret_decay_f": decay0 + nrm(ks[12], (L, H), 0.1),
        "ret_decay_b": decay0 + nrm(ks[13], (L, H), 0.1),
        "ret_gn_f": gain(ks[14], (L, GROUP_WIDTH)),
        "ret_gn_b": gain(ks[15], (L, GROUP_WIDTH)),
        "mix_beta": gain(ks[16], (L, D)),
        "w_out": nrm(ks[17], (L, D, D), D ** -0.5),
        "norm_ffn_g": gain(ks[18], (L, D)),
        "peer_wq": nrm(ks[19], (L, D, PEER_HEADS * PEER_DKEY), D ** -0.5),
        "peer_k1": nrm(ks[20], (L, PEER_NKEYS, PEER_DKEY // 2), (PEER_DKEY // 2) ** -0.5),
        "peer_k2": nrm(ks[21], (L, PEER_NKEYS, PEER_DKEY // 2), (PEER_DKEY // 2) ** -0.5),
        "peer_u": nrm(ks[22], (L, PEER_EXPERTS, D), D ** -0.5),
        "peer_v": nrm(ks[23], (L, PEER_EXPERTS, D), 0.5),
        "final_norm_g": gain(ks[24], (D,)),
    }


def reference(x, c, ctx, c_ctx, ada_w, ada_b, norm_mix_g, w_in, mla_kv_norm_g, mla_w_ukv, swa_sink, na_rpb,
              ret_decay_f, ret_decay_b, ret_gn_f, ret_gn_b, mix_beta, w_out, norm_ffn_g, peer_wq, peer_k1,
              peer_k2, peer_u, peer_v, final_norm_g):
    n = x.shape[1]
    cos_r, sin_r = axial_rope_tables(n, MLA_ROPE)
    cos_h, sin_h = axial_rope_tables(n, HEAD_DIM)
    cuts = np.cumsum(IN_SPLITS)[:-1].tolist()
    xc = ctx
    for layer in range(DEPTH):
        with_ctx = layer < DEPTH - 1
        mod_l = jnp.split(jax.nn.silu(c) @ ada_w[layer] + ada_b[layer], ADA_CHUNKS, axis=-1)
        mod_c = jnp.split(jax.nn.silu(c_ctx)[None, :] @ ada_w[layer] + ada_b[layer], ADA_CHUNKS, axis=-1)
        p_l = jnp.split(modulate(rmsnorm(x, norm_mix_g[layer]), mod_l[0], mod_l[1]) @ w_in[layer], cuts, axis=-1)
        p_c = jnp.split(modulate(rmsnorm(xc, norm_mix_g[layer]), mod_c[0], mod_c[1]) @ w_in[layer], cuts, axis=-1)
        ya_l, ya_c = mla_mixer(*p_l[0:3], *p_c[0:3], mla_kv_norm_g[layer], mla_w_ukv[layer], cos_r, sin_r, with_ctx)
        yb_l, yb_c = swa_mixer(*p_l[3:6], *p_c[3:6], swa_sink[layer], cos_h, sin_h, with_ctx)
        yc_l, yc_c = na_mixer(*p_l[6:9], *p_c[6:9], na_rpb[layer], with_ctx)
        yd_l, yd_c = retention_mixer(*p_l[9:14], *p_c[9:14], ret_decay_f[layer], ret_decay_b[layer],
                                     ret_gn_f[layer], ret_gn_b[layer], with_ctx)
        y_l = jnp.concatenate([ya_l, yb_l, yc_l, yd_l], axis=-1) * mix_beta[layer]
        x = x + mod_l[2][:, None, :] * (y_l @ w_out[layer])
        h2 = modulate(rmsnorm(x, norm_ffn_g[layer]), mod_l[3], mod_l[4])
        x = x + mod_l[5][:, None, :] * peer_ffn(h2, peer_wq[layer], peer_k1[layer], peer_k2[layer],
                                                 peer_u[layer], peer_v[layer])
        if with_ctx:
            y_c = jnp.concatenate([ya_c, yb_c, yc_c, yd_c], axis=-1) * mix_beta[layer]
            xc = xc + mod_c[2][:, None, :] * (y_c @ w_out[layer])
            h2c = modulate(rmsnorm(xc, norm_ffn_g[layer]), mod_c[3], mod_c[4])
            xc = xc + mod_c[5][:, None, :] * peer_ffn(h2c, peer_wq[layer], peer_k1[layer], peer_k2[layer],
                                                       peer_u[layer], peer_v[layer])
    return rmsnorm(x, final_norm_g)
```

```python
import functools

import numpy as np
import jax
import jax.numpy as jnp
from jax import lax
from jax.experimental import pallas as pl
from jax.experimental.pallas import tpu as pltpu

F32 = jnp.float32
BF16 = jnp.bfloat16

EPS = 1e-6
NEG_INF = -1e30
ROPE_BASE = 10000.0
GRID_W = 64
HEAD_DIM = 64
LANES = 128
V7X_VMEM_BYTES = 64 * 1024 * 1024
MIB = 1024 * 1024

MLA_ROPE = 32
MLA_QK = 96
MLA_RANK = 128
SWA_WINDOW = 128
NA_KH, NA_KW = 8, 16
NA_ROWS = 4
RET_CHUNK = 128
PEER_HEADS = 8
PEER_NKEYS = 128
PEER_TOPK = 16
PEER_TOKENS = 256
PEER_ROWS = 8

_MQ, _CKV, _KR, _SQ, _SK, _SV, _NQ, _NK, _NV, _RQ, _RK, _RV, _GF, _GB, _IN_COLS = (
    0, 512, 640, 768, 1024, 1152, 1280, 1536, 1792, 2048, 2304, 2560, 2816, 3072, 3328)


def _cparams(semantics, vmem_bytes):
    limit = int(min(vmem_bytes, V7X_VMEM_BYTES * 7 // 8))
    return pltpu.CompilerParams(dimension_semantics=semantics, vmem_limit_bytes=limit)


def _dot(a, b):
    return jnp.dot(a, b, preferred_element_type=F32)


def _dot_nt(a, b):
    return lax.dot_general(a, b, (((1,), (1,)), ((), ())), preferred_element_type=F32)


def _dot_tn(a, b):
    return lax.dot_general(a, b, (((0,), (0,)), ((), ())), preferred_element_type=F32)


def _low_half(shape):
    return lax.broadcasted_iota(jnp.int32, shape, len(shape) - 1) < HEAD_DIM


def _half_masks():
    lo = _low_half((1, LANES))
    return (jnp.where(lo, 1.0, 0.0).astype(BF16), jnp.where(lo, 0.0, 1.0).astype(BF16))


def _norm_mod(x, g, shift, scale):
    ms = jnp.mean(x * x, axis=-1, keepdims=True)
    return (x * lax.rsqrt(ms + EPS) * g) * (1.0 + scale) + shift


def _rmsnorm(x, g):
    ms = jnp.mean(x * x, axis=-1, keepdims=True)
    return x * lax.rsqrt(ms + EPS) * g


def _ada_kernel(c_ref, w_ref, b_ref, o_ref):
    c = c_ref[...]
    o_ref[0] = _dot(c * jax.nn.sigmoid(c), w_ref[0]) + b_ref[0]


def _ada_modulation(cc, ada_w, ada_b):
    depth, d, width = ada_w.shape
    rows = cc.shape[0]
    tn = 1024
    return pl.pallas_call(
        _ada_kernel,
        out_shape=jax.ShapeDtypeStruct((depth, rows, width), F32),
        grid=(depth, width // tn),
        in_specs=[pl.BlockSpec((rows, d), lambda l, j: (0, 0)),
                  pl.BlockSpec((1, d, tn), lambda l, j: (l, 0, j)),
                  pl.BlockSpec((1, 1, tn), lambda l, j: (l, 0, j))],
        out_specs=pl.BlockSpec((1, rows, tn), lambda l, j: (l, 0, j)),
        compiler_params=_cparams(("arbitrary", "arbitrary"), 2 * d * tn * 4 + 8 * MIB),
    )(cc, ada_w, ada_b.reshape(depth, 1, width))


def _rope(a, cos, sin, half):
    lane = lax.broadcasted_iota(jnp.int32, a.shape, 1)
    first = (lane % (2 * half)) < half
    rot = jnp.where(first, -pltpu.roll(a, LANES - half, 1), pltpu.roll(a, half, 1))
    return a * cos + rot * sin


def _inproj_kernel(x_ref, sh_ref, sc_ref, g_ref, w_ref, kvg_ref, wkn_ref, e_ref, wv_ref,
                   c32_ref, s32_ref, c64_ref, s64_ref,
                   mq_ref, mk_ref, mv_ref, sq_ref, sk_ref, sv_ref, nq_ref, nk_ref, nv_ref,
                   rq_ref, rk_ref, rv_ref, gf_ref, gb_ref):
    h = _norm_mod(x_ref[0], g_ref[...], sh_ref[0], sc_ref[0]).astype(BF16)

    def seg(lo, width):
        return _dot(h, w_ref[:, lo:lo + width])

    c32, s32, c64, s64 = c32_ref[...], s32_ref[...], c64_ref[...], s64_ref[...]
    mla_scale = MLA_QK ** -0.5
    head_scale = HEAD_DIM ** -0.5
    for hd in range(4):
        a = seg(_MQ + hd * LANES, LANES)
        mq_ref[0, :, hd * LANES:(hd + 1) * LANES] = (_rope(a, c32, s32, MLA_ROPE // 4) * mla_scale).astype(BF16)
    kvn = _rmsnorm(seg(_CKV, MLA_RANK), kvg_ref[...]).astype(BF16)
    kr = _rope(seg(_KR, LANES), c32, s32, MLA_ROPE // 4).astype(BF16)
    mk_ref[0] = (_dot(kvn, wkn_ref[...]) + _dot(kr, e_ref[...])).astype(BF16)
    mv_ref[0] = _dot(kvn, wv_ref[...]).astype(BF16)
    for grp in range(2):
        a = seg(_SQ + grp * LANES, LANES)
        sq_ref[0, :, grp * LANES:(grp + 1) * LANES] = (_rope(a, c64, s64, HEAD_DIM // 4) * head_scale).astype(BF16)
    sk_ref[0] = _rope(seg(_SK, LANES), c64, s64, HEAD_DIM // 4).astype(BF16)
    sv_ref[0] = seg(_SV, LANES).astype(BF16)
    nq_ref[0] = (seg(_NQ, 256) * head_scale).astype(BF16)
    nk_ref[0] = seg(_NK, 256).astype(BF16)
    nv_ref[0] = seg(_NV, 256).astype(BF16)
    rq_ref[0] = seg(_RQ, 256).astype(BF16)
    rk_ref[0] = (seg(_RK, 256) * head_scale).astype(BF16)
    rv_ref[0] = seg(_RV, 256).astype(BF16)
    gf_ref[0] = seg(_GF, 256).astype(BF16)
    gb_ref[0] = seg(_GB, 256).astype(BF16)


_INPROJ_WIDTHS = (512, 512, 256, 256, 128, 128, 256, 256, 256, 256, 256, 256, 256, 256)


def _in_projection(x, shift, scale, g, w, kvg, wkn, emat, wv, tabs):
    b, t, d = x.shape
    tm = min(512, t)
    const = lambda shape: pl.BlockSpec(shape, lambda bi, i: (0,) * len(shape))
    tab = pl.BlockSpec((tm, LANES), lambda bi, i: (i, 0))
    vec = pl.BlockSpec((1, 1, d), lambda bi, i: (bi, 0, 0))
    est = 2 * (tm * d * 4 + d * _IN_COLS * 2 + tm * sum(_INPROJ_WIDTHS) * 2 + 4 * tm * LANES * 4) + 12 * MIB
    return pl.pallas_call(
        _inproj_kernel,
        out_shape=[jax.ShapeDtypeStruct((b, t, wd), BF16) for wd in _INPROJ_WIDTHS],
        grid=(b, t // tm),
        in_specs=[pl.BlockSpec((1, tm, d), lambda bi, i: (bi, i, 0)), vec, vec, const((1, d)),
                  const((d, _IN_COLS)), const((1, MLA_RANK)), const((MLA_RANK, 512)), const((LANES, 512)),
                  const((MLA_RANK, 256)), tab, tab, tab, tab],
        out_specs=[pl.BlockSpec((1, tm, wd), lambda bi, i: (bi, i, 0)) for wd in _INPROJ_WIDTHS],
        compiler_params=_cparams(("arbitrary", "arbitrary"), est),
    )(x, shift, scale, g, w, kvg, wkn, emat, wv, *tabs)


def _attend_pair(qs, pieces, sinks):
    outs = []
    for s in (0, 1):
        scores = []
        for ks, _, biases in pieces:
            sc = _dot_nt(qs[s], ks[s])
            if biases[s] is not None:
                sc = sc + biases[s]
            scores.append(sc)
        m = functools.reduce(jnp.maximum, [jnp.max(sc, axis=-1, keepdims=True) for sc in scores])
        if sinks is not None:
            m = jnp.maximum(m, sinks[s])
        den = jnp.exp(sinks[s] - m) if sinks is not None else 0.0
        o = 0.0
        for sc, (_, v, _) in zip(scores, pieces):
            p = jnp.exp(sc - m)
            den = den + jnp.sum(p, axis=-1, keepdims=True)
            o = o + _dot(p.astype(BF16), v)
        outs.append(o / den)
    return jnp.where(_low_half(outs[0].shape), outs[0], outs[1])


def _mla_kernel(q_ref, kc_ref, vc_ref, kl_ref, vl_ref, o_ref, m_ref, l_ref, acc_ref):
    j = pl.program_id(2)

    @pl.when(j == 0)
    def _():
        m_ref[...] = jnp.full(m_ref.shape, NEG_INF, F32)
        l_ref[...] = jnp.zeros(l_ref.shape, F32)
        acc_ref[...] = jnp.zeros(acc_ref.shape, F32)

    def update(k, v):
        lo = _low_half((1, LANES))
        for pair in range(2):
            alphas, pvs = [], []
            for s in (0, 1):
                hd = 2 * pair + s
                q = q_ref[0, :, hd * LANES:(hd + 1) * LANES]
                sc = _dot_nt(q, k[:, hd * LANES:(hd + 1) * LANES])
                m_prev = m_ref[hd]
                m_new = jnp.maximum(m_prev, jnp.max(sc, axis=-1, keepdims=True))
                alpha = jnp.exp(m_prev - m_new)
                p = jnp.exp(sc - m_new)
                l_ref[hd] = alpha * l_ref[hd] + jnp.sum(p, axis=-1, keepdims=True)
                m_ref[hd] = m_new
                alphas.append(alpha)
                pvs.append(_dot(p.astype(BF16), v[:, pair * LANES:(pair + 1) * LANES]))
            acc_ref[pair] = (acc_ref[pair] * jnp.where(lo, alphas[0], alphas[1])
                             + jnp.where(lo, pvs[0], pvs[1]))

    @pl.when(j == 0)
    def _():
        update(kc_ref[0], vc_ref[0])

    @pl.when(j > 0)
    def _():
        update(kl_ref[0], vl_ref[0])

    @pl.when(j == pl.num_programs(2) - 1)
    def _():
        lo = _low_half((1, LANES))
        for pair in range(2):
            inv = jnp.where(lo, 1.0 / l_ref[2 * pair], 1.0 / l_ref[2 * pair + 1])
            o_ref[0, :, pair * LANES:(pair + 1) * LANES] = (acc_ref[pair] * inv).astype(BF16)


def _mla_attention(q, kc, vc, kl, vl):
    b, n, _ = q.shape
    c = kc.shape[1]
    tq = min(512, n)
    tk = min(512, n)
    est = 2 * (tq * 512 * 2 + c * 768 * 2 + tk * 768 * 2 + tq * 256 * 2) + 6 * tq * LANES * 4 + 10 * tq * tk * 4 + 8 * MIB
    return pl.pallas_call(
        _mla_kernel,
        out_shape=jax.ShapeDtypeStruct((b, n, 256), BF16),
        grid=(b, n // tq, 1 + n // tk),
        in_specs=[pl.BlockSpec((1, tq, 512), lambda bi, i, j: (bi, i, 0)),
                  pl.BlockSpec((1, c, 512), lambda bi, i, j: (bi, 0, 0)),
                  pl.BlockSpec((1, c, 256), lambda bi, i, j: (bi, 0, 0)),
                  pl.BlockSpec((1, tk, 512), lambda bi, i, j: (bi, jnp.maximum(j - 1, 0), 0)),
                  pl.BlockSpec((1, tk, 256), lambda bi, i, j: (bi, jnp.maximum(j - 1, 0), 0))],
        out_specs=pl.BlockSpec((1, tq, 256), lambda bi, i, j: (bi, i, 0)),
        scratch_shapes=[pltpu.VMEM((4, tq, 1), F32), pltpu.VMEM((4, tq, 1), F32), pltpu.VMEM((2, tq, LANES), F32)],
        compiler_params=_cparams(("arbitrary", "arbitrary", "arbitrary"), est),
    )(q, kc, vc, kl, vl)


def _swa_kernel(q_ref, kp_ref, kcur_ref, kn_ref, vp_ref, vcur_ref, vn_ref, kctx_ref, vctx_ref, sink_ref, o_ref):
    i = pl.program_id(1)
    last = pl.num_programs(1) - 1
    tq = q_ref.shape[1]

    def band_bias(width, offset, edge_penalty):
        r = lax.broadcasted_iota(jnp.int32, (tq, width), 0)
        cidx = lax.broadcasted_iota(jnp.int32, (tq, width), 1)
        rel = cidx + offset - r
        inside = jnp.where(rel >= -SWA_WINDOW, jnp.where(rel <= SWA_WINDOW, 1, 0), 0)
        return jnp.where(inside == 1, edge_penalty, NEG_INF)

    halo = kp_ref.shape[1]
    b_prev = band_bias(halo, -halo, jnp.where(i > 0, 0.0, NEG_INF))
    b_cur = band_bias(tq, 0, 0.0)
    b_next = band_bias(halo, tq, jnp.where(i < last, 0.0, NEG_INF))
    m0, m1 = _half_masks()
    kctx, vctx = kctx_ref[0], vctx_ref[0]
    pieces = [((kp_ref[0],) * 2, vp_ref[0], (b_prev,) * 2),
              ((kcur_ref[0],) * 2, vcur_ref[0], (b_cur,) * 2),
              ((kn_ref[0],) * 2, vn_ref[0], (b_next,) * 2),
              ((kctx,) * 2, vctx, (None, None))]
    for grp in range(2):
        qg = q_ref[0, :, grp * LANES:(grp + 1) * LANES]
        sinks = [sink_ref[2 * grp + s:2 * grp + s + 1, 0:1] for s in (0, 1)]
        o_ref[0, :, grp * LANES:(grp + 1) * LANES] = _attend_pair((qg * m0, qg * m1), pieces, sinks).astype(BF16)


def _swa_attention(q, k, v, kc, vc, sink_rows):
    b, n, _ = q.shape
    c = kc.shape[1]
    tq = 256
    halo = SWA_WINDOW
    r = tq // halo
    nh = n // halo
    cur = lambda bi, i: (bi, i, 0)
    prev = lambda bi, i: (bi, jnp.maximum(i * r - 1, 0), 0)
    nxt = lambda bi, i: (bi, jnp.minimum(i * r + r, nh - 1), 0)
    ctx = lambda bi, i: (bi, 0, 0)
    est = 16 * tq * (tq + 2 * halo + c) * 4 + 8 * MIB
    return pl.pallas_call(
        _swa_kernel,
        out_shape=jax.ShapeDtypeStruct((b, n, 256), BF16),
        grid=(b, n // tq),
        in_specs=[pl.BlockSpec((1, tq, 256), cur),
                  pl.BlockSpec((1, halo, LANES), prev), pl.BlockSpec((1, tq, LANES), cur), pl.BlockSpec((1, halo, LANES), nxt),
                  pl.BlockSpec((1, halo, LANES), prev), pl.BlockSpec((1, tq, LANES), cur), pl.BlockSpec((1, halo, LANES), nxt),
                  pl.BlockSpec((1, c, LANES), ctx), pl.BlockSpec((1, c, LANES), ctx),
                  pl.BlockSpec((4, LANES), lambda bi, i: (0, 0))],
        out_specs=pl.BlockSpec((1, tq, 256), cur),
        compiler_params=_cparams(("arbitrary", "arbitrary"), est),
    )(q, k, k, k, v, v, v, kc, vc, sink_rows)


def _na_kernel(q_ref, kp_ref, kcur_ref, kn_ref, vp_ref, vcur_ref, vn_ref, kctx_ref, vctx_ref, bias_ref, o_ref):
    tq = q_ref.shape[1]
    m0, m1 = _half_masks()
    for pair in range(2):
        sl = slice(pair * LANES, (pair + 1) * LANES)
        qp = q_ref[0, :, sl]
        pieces = []
        for idx, (kr, vr) in enumerate(((kp_ref, vp_ref), (kcur_ref, vcur_ref), (kn_ref, vn_ref))):
            biases = tuple(bias_ref[0, 2 * pair + s, :, idx * tq:(idx + 1) * tq] for s in (0, 1))
            pieces.append(((kr[0, :, sl],) * 2, vr[0, :, sl], biases))
        pieces.append(((kctx_ref[0, :, sl],) * 2, vctx_ref[0, :, sl], (None, None)))
        o_ref[0, :, sl] = _attend_pair((qp * m0, qp * m1), pieces, None).astype(BF16)


def _na_bias_tables(rpb, rows):
    nb = rows // NA_ROWS
    a = np.arange(NA_ROWS)[:, None, None, None]
    cq = np.arange(GRID_W)[None, :, None, None]
    kr = np.arange(3 * NA_ROWS)[None, None, :, None]
    ck = np.arange(GRID_W)[None, None, None, :]
    idxs, valids = [], []
    for j in (0, 1, nb - 1):
        r = NA_ROWS * j + a
        rs = np.clip(r - NA_KH // 2, 0, rows - NA_KH)
        rk = NA_ROWS * (j - 1) + kr
        cs = np.clip(cq - NA_KW // 2, 0, GRID_W - NA_KW)
        valid = (rk >= rs) & (rk < rs + NA_KH) & (ck >= cs) & (ck < cs + NA_KW)
        idx = np.clip(rk - r + NA_KH - 1, 0, 2 * NA_KH - 2) * (2 * NA_KW - 1) + np.clip(ck - cq + NA_KW - 1, 0, 2 * NA_KW - 2)
        shape = (NA_ROWS * GRID_W, 3 * NA_ROWS * GRID_W)
        idxs.append(np.broadcast_to(idx, valid.shape).reshape(shape))
        valids.append(valid.reshape(shape))
    idx = jnp.asarray(np.stack(idxs), jnp.int32)
    valid = jnp.asarray(np.stack(valids))
    flat = rpb.astype(F32).reshape(rpb.shape[0], -1)
    return jnp.where(valid[:, None], jnp.transpose(flat[:, idx], (1, 0, 2, 3)), NEG_INF)


def _na_attention(q, k, v, kc, vc, bias):
    b, n, _ = q.shape
    c = kc.shape[1]
    tq = NA_ROWS * GRID_W
    nb = n // tq
    cur = lambda bi, i: (bi, i, 0)
    prev = lambda bi, i: (bi, jnp.maximum(i - 1, 0), 0)
    nxt = lambda bi, i: (bi, jnp.minimum(i + 1, nb - 1), 0)
    ctx = lambda bi, i: (bi, 0, 0)
    variant = lambda bi, i: (jnp.where(i == 0, 0, jnp.where(i == nb - 1, 2, 1)), 0, 0, 0)
    blk = pl.BlockSpec((1, tq, 256), cur)
    est = 2 * 4 * tq * 3 * tq * 4 + 16 * tq * (3 * tq + c) * 4 + 8 * MIB
    return pl.pallas_call(
        _na_kernel,
        out_shape=jax.ShapeDtypeStruct((b, n, 256), BF16),
        grid=(b, nb),
        in_specs=[blk, pl.BlockSpec((1, tq, 256), prev), blk, pl.BlockSpec((1, tq, 256), nxt),
                  pl.BlockSpec((1, tq, 256), prev), blk, pl.BlockSpec((1, tq, 256), nxt),
                  pl.BlockSpec((1, c, 256), ctx), pl.BlockSpec((1, c, 256), ctx),
                  pl.BlockSpec((1, 4, tq, 3 * tq), variant)],
        out_specs=blk,
        compiler_params=_cparams(("arbitrary", "arbitrary"), est),
    )(q, k, k, k, v, v, v, kc, vc, bias)


def _ctx_attn_kernel(mq_ref, mk_ref, mv_ref, sq_ref, sk_ref, sv_ref, sink_ref, nq_ref, nk_ref, nv_ref,
                     ya_ref, yb_ref, yc_ref):
    m0, m1 = _half_masks()
    none2 = (None, None)
    for pair in range(2):
        sl = slice(pair * LANES, (pair + 1) * LANES)
        h0 = slice(2 * pair * LANES, (2 * pair + 1) * LANES)
        h1 = slice((2 * pair + 1) * LANES, (2 * pair + 2) * LANES)
        ya_ref[0, :, sl] = _attend_pair((mq_ref[0, :, h0], mq_ref[0, :, h1]),
                                        [((mk_ref[0, :, h0], mk_ref[0, :, h1]), mv_ref[0, :, sl], none2)],
                                        None).astype(BF16)
        qg = sq_ref[0, :, sl]
        sinks = [sink_ref[2 * pair + s:2 * pair + s + 1, 0:1] for s in (0, 1)]
        yb_ref[0, :, sl] = _attend_pair((qg * m0, qg * m1), [((sk_ref[0],) * 2, sv_ref[0], none2)],
                                        sinks).astype(BF16)
        qn = nq_ref[0, :, sl]
        yc_ref[0, :, sl] = _attend_pair((qn * m0, qn * m1), [((nk_ref[0, :, sl],) * 2, nv_ref[0, :, sl], none2)],
                                        None).astype(BF16)


def _ctx_attention(mq, mk, mv, sq, sk, sv, sink_rows, nq, nk, nv):
    b, c, _ = mq.shape
    spec = lambda wd: pl.BlockSpec((1, c, wd), lambda bi: (bi, 0, 0))
    return pl.pallas_call(
        _ctx_attn_kernel,
        out_shape=[jax.ShapeDtypeStruct((b, c, 256), BF16)] * 3,
        grid=(b,),
        in_specs=[spec(512), spec(512), spec(256), spec(256), spec(LANES), spec(LANES),
                  pl.BlockSpec((4, LANES), lambda bi: (0, 0)), spec(256), spec(256), spec(256)],
        out_specs=[spec(256)] * 3,
        compiler_params=_cparams(("arbitrary",), 24 * MIB),
    )(mq, mk, mv, sq, sk, sv, sink_rows, nq, nk, nv)


def _ret_chunk(q, k, v, g, dec_lane, gn_lane, s_scr, reverse):
    c = RET_CHUNK
    lo = _low_half((1, LANES))
    lg = -(jnp.maximum(-dec_lane, 0.0) + jnp.log1p(jnp.exp(-jnp.abs(dec_lane))))
    ti = lax.broadcasted_iota(jnp.int32, (c, LANES), 0).astype(F32)
    if reverse:
        qpow, kpow = c - ti, ti
    else:
        qpow, kpow = ti + 1.0, c - 1.0 - ti
    qdec = jnp.exp(lg * qpow)
    kdec = jnp.exp(lg * kpow)
    cdec = jnp.exp(lg * float(c))
    ii = lax.broadcasted_iota(jnp.int32, (c, c), 0)
    jj = lax.broadcasted_iota(jnp.int32, (c, c), 1)
    dist = (jj - ii) if reverse else (ii - jj)
    distf = jnp.maximum(dist, 0).astype(F32)
    m0, m1 = _half_masks()
    outs = []
    for s, msk in ((0, m0), (1, m1)):
        lg_s = lg[:, s * HEAD_DIM:s * HEAD_DIM + 1]
        decay = jnp.where(dist >= 0, jnp.exp(lg_s * distf), 0.0)
        inner = _dot_nt(q * msk, k) * decay
        outs.append(_dot(inner.astype(BF16), v))
    state = s_scr[...]
    o = jnp.where(lo, outs[0], outs[1]) + _dot((q.astype(F32) * qdec).astype(BF16), state.astype(BF16))
    kd = (k.astype(F32) * kdec).astype(BF16)
    same_head = (ii < HEAD_DIM) == (jj < HEAD_DIM)
    s_scr[...] = state * cdec + jnp.where(same_head, _dot_tn(kd, v), 0.0)

    def head_mean(x):
        s_lo = jnp.sum(jnp.where(lo, x, 0.0), axis=-1, keepdims=True)
        s_hi = jnp.sum(jnp.where(lo, 0.0, x), axis=-1, keepdims=True)
        return jnp.where(lo, s_lo, s_hi) * (1.0 / HEAD_DIM)

    dev = o - head_mean(o)
    normed = dev * lax.rsqrt(head_mean(dev * dev) + EPS)
    gf = g.astype(F32)
    return (gf * jax.nn.sigmoid(gf)) * (normed * gn_lane)


def _ret_kernel(decf_ref, decb_ref, gnf_ref, gnb_ref, s0f_ref, s0b_ref,
                qf_ref, kf_ref, vf_ref, gf_ref, qb_ref, kb_ref, vb_ref, gb_ref,
                yf_ref, yb_ref, sf_ref, sb_ref, sf_scr, sb_scr):
    t = pl.program_id(2)

    @pl.when(t == 0)
    def _():
        sf_scr[...] = s0f_ref[0, 0]
        sb_scr[...] = s0b_ref[0, 0]

    yf_ref[0] = _ret_chunk(qf_ref[0], kf_ref[0], vf_ref[0], gf_ref[0], decf_ref[0], gnf_ref[...], sf_scr,
                           False).astype(BF16)
    yb_ref[0] = _ret_chunk(qb_ref[0], kb_ref[0], vb_ref[0], gb_ref[0], decb_ref[0], gnb_ref[...], sb_scr,
                           True).astype(BF16)

    @pl.when(t == pl.num_programs(2) - 1)
    def _():
        sf_ref[0, 0] = sf_scr[...]
        sb_ref[0, 0] = sb_scr[...]


def _retention(q, k, v, gf, gb, dec_f, dec_b, gn_f, gn_b, s0f, s0b):
    b, t, _ = q.shape
    c = RET_CHUNK
    nt = t // c
    fwd = pl.BlockSpec((1, c, LANES), lambda bi, p, i: (bi, i, p))
    bwd = pl.BlockSpec((1, c, LANES), lambda bi, p, i: (bi, nt - 1 - i, p))
    dec = pl.BlockSpec((1, 1, LANES), lambda bi, p, i: (p, 0, 0))
    gn = pl.BlockSpec((1, LANES), lambda bi, p, i: (0, p))
    st = pl.BlockSpec((1, 1, LANES, LANES), lambda bi, p, i: (bi, p, 0, 0))
    return pl.pallas_call(
        _ret_kernel,
        out_shape=[jax.ShapeDtypeStruct((b, t, 256), BF16)] * 2 + [jax.ShapeDtypeStruct((b, 2, LANES, LANES), F32)] * 2,
        grid=(b, 2, nt),
        in_specs=[dec, dec, gn, gn, st, st, fwd, fwd, fwd, fwd, bwd, bwd, bwd, bwd],
        out_specs=[fwd, bwd, st, st],
        scratch_shapes=[pltpu.VMEM((LANES, LANES), F32)] * 2,
        compiler_params=_cparams(("arbitrary", "arbitrary", "arbitrary"), 16 * MIB),
    )(dec_f, dec_b, gn_f, gn_b, s0f, s0b, q, k, v, gf, q, k, v, gb)


def _outproj_kernel(x_ref, ya_ref, yb_ref, yc_ref, ydf_ref, ydb_ref, beta_ref, w_ref, gate_ref, o_ref):
    ys = (ya_ref[0].astype(F32), yb_ref[0].astype(F32), yc_ref[0].astype(F32),
          ydf_ref[0].astype(F32) + ydb_ref[0].astype(F32))
    acc = 0.0
    for s, y in enumerate(ys):
        sl = slice(s * 256, (s + 1) * 256)
        acc = acc + _dot((y * beta_ref[:, sl]).astype(BF16), w_ref[sl, :])
    o_ref[0] = x_ref[0] + gate_ref[0] * acc


def _out_projection(x, ya, yb, yc, ydf, ydb, beta, w, gate):
    b, t, d = x.shape
    tm = min(512, t)
    row = lambda wd: pl.BlockSpec((1, tm, wd), lambda bi, i: (bi, i, 0))
    est = 2 * (2 * tm * d * 4 + 5 * tm * 256 * 2 + d * d * 2) + 8 * MIB
    return pl.pallas_call(
        _outproj_kernel,
        out_shape=jax.ShapeDtypeStruct((b, t, d), F32),
        grid=(b, t // tm),
        in_specs=[row(d), row(256), row(256), row(256), row(256), row(256),
                  pl.BlockSpec((1, d), lambda bi, i: (0, 0)), pl.BlockSpec((d, d), lambda bi, i: (0, 0)),
                  pl.BlockSpec((1, 1, d), lambda bi, i: (bi, 0, 0))],
        out_specs=row(d),
        compiler_params=_cparams(("arbitrary", "arbitrary"), est),
    )(x, ya, yb, yc, ydf, ydb, beta, w, gate)


def _top_values(work, out_ref, count):
    for r in range(count):
        m = jnp.max(work, axis=0, keepdims=True)
        out_ref[r:r + 1, :] = m
        work = jnp.where(work == m, -jnp.inf, work)


def _peer_route(hd, q_scr, k1_ref, k2_ref, s1_scr, s2_scr, e1_scr, e2_scr, thr_scr, v1_scr, v2_scr, top_scr):
    half = PEER_NKEYS
    base = pl.multiple_of(hd * 2 * half, 2 * half)
    s1 = _dot(k1_ref[...], q_scr[pl.ds(base, half), :].astype(BF16))
    s2 = _dot(k2_ref[...], q_scr[pl.ds(base + half, half), :].astype(BF16))
    s1_scr[hd] = s1
    s2_scr[hd] = s2
    _top_values(s1, v1_scr, PEER_TOPK)
    _top_values(s2, v2_scr, PEER_TOPK)
    v1, v2 = v1_scr[...], v2_scr[...]
    row8 = lax.broadcasted_iota(jnp.int32, (8, v1.shape[1]), 0)
    cands = [v1[0:1] + v2]
    for a in range(1, 8):
        cands.append(jnp.where(row8 < PEER_TOPK // (a + 1), v1[a:a + 1] + v2[0:8], -jnp.inf))
    cands.append(v1[8:16] + v2[0:1])
    _top_values(jnp.concatenate(cands, axis=0), top_scr, PEER_TOPK)
    top = top_scr[...]
    z = jnp.sum(jnp.exp(top - top[0:1]), axis=0, keepdims=True)
    thr_scr[pl.ds(hd, 1), :] = top[PEER_TOPK - 1:PEER_TOPK]
    e1_scr[hd] = jnp.exp(s1 - v1[0:1]) / z
    e2_scr[hd] = jnp.exp(s2 - v2[0:1])


def _peer_kernel(*refs, final_norm):
    if final_norm:
        (x_ref, sh_ref, sc_ref, gt_ref, g_ref, wqt_ref, k1_ref, k2_ref, u_ref, vt_ref, fg_ref, o_ref,
         h_scr, q_scr, s1_scr, s2_scr, e1_scr, e2_scr, thr_scr, v1_scr, v2_scr, top_scr, g_scr, acc_scr) = refs
    else:
        (x_ref, sh_ref, sc_ref, gt_ref, g_ref, wqt_ref, k1_ref, k2_ref, u_ref, vt_ref, o_ref,
         h_scr, q_scr, s1_scr, s2_scr, e1_scr, e2_scr, thr_scr, v1_scr, v2_scr, top_scr, g_scr, acc_scr) = refs
        fg_ref = None
    step = pl.program_id(2)

    @pl.when(step == 0)
    def _():
        h = _norm_mod(x_ref[0], g_ref[...], sh_ref[0], sc_ref[0]).astype(BF16)
        h_scr[...] = h
        q_scr[...] = _dot_nt(wqt_ref[...], h)
        acc_scr[...] = jnp.zeros(acc_scr.shape, F32)

        def route(hd, carry):
            _peer_route(hd, q_scr, k1_ref, k2_ref, s1_scr, s2_scr, e1_scr, e2_scr, thr_scr,
                        v1_scr, v2_scr, top_scr)
            return carry

        lax.fori_loop(0, PEER_HEADS, route, 0)

    for r in range(PEER_ROWS):
        i = step * PEER_ROWS + r
        gate = jnp.zeros((PEER_NKEYS, h_scr.shape[0]), F32)
        for hd in range(PEER_HEADS):
            total = s2_scr[hd] + s1_scr[hd, pl.ds(i, 1), :]
            val = e2_scr[hd] * e1_scr[hd, pl.ds(i, 1), :]
            gate = gate + jnp.where(total >= thr_scr[hd:hd + 1, :], val, 0.0)
        g_scr[r * PEER_NKEYS:(r + 1) * PEER_NKEYS, :] = gate
    a = _dot_nt(u_ref[...], h_scr[...])
    gelu = 0.5 * a * (1.0 + lax.erf(a * np.float32(np.sqrt(0.5))))
    acc_scr[...] += _dot(vt_ref[...], (gelu * g_scr[...]).astype(BF16))

    @pl.when(step == pl.num_programs(2) - 1)
    def _():
        y = x_ref[0] + gt_ref[0] * jnp.transpose(acc_scr[...])
        if final_norm:
            y = _rmsnorm(y, fg_ref[...])
        o_ref[0] = y


def _peer_ffn(x, shift, scale, gate, g, wqt, k1, k2, u, vt, final_g=None):
    b, t, d = x.shape
    tt = PEER_TOKENS
    ec = PEER_ROWS * PEER_NKEYS
    nq = wqt.shape[0]
    final_norm = final_g is not None
    tok = pl.BlockSpec((1, tt, d), lambda bi, i, s: (bi, i, 0))
    vec = pl.BlockSpec((1, 1, d), lambda bi, i, s: (bi, 0, 0))
    const = lambda shape: pl.BlockSpec(shape, lambda bi, i, s: (0,) * len(shape))
    in_specs = [tok, vec, vec, vec, const((1, d)), const((nq, d)), const(k1.shape), const(k2.shape),
                pl.BlockSpec((ec, d), lambda bi, i, s: (s, 0)), pl.BlockSpec((d, ec), lambda bi, i, s: (0, s))]
    args = [x, shift, scale, gate, g, wqt, k1, k2, u, vt]
    if final_norm:
        in_specs.append(const((1, d)))
        args.append(final_g)
    table = pltpu.VMEM((PEER_HEADS, PEER_NKEYS, tt), F32)
    scratch = [pltpu.VMEM((tt, d), BF16), pltpu.VMEM((nq, tt), F32), table, table, table, table,
               pltpu.VMEM((PEER_HEADS, tt), F32), pltpu.VMEM((PEER_TOPK, tt), F32), pltpu.VMEM((PEER_TOPK, tt), F32),
               pltpu.VMEM((PEER_TOPK, tt), F32), pltpu.VMEM((ec, tt), F32), pltpu.VMEM((d, tt), F32)]
    est = (4 * tt * d * 4 + 2 * nq * d * 2 + 4 * ec * d * 2 + tt * d * 2 + nq * tt * 4
           + 4 * PEER_HEADS * PEER_NKEYS * tt * 4 + ec * tt * 4 + d * tt * 4 + 6 * ec * tt * 4 + 8 * MIB)
    return pl.pallas_call(
        functools.partial(_peer_kernel, final_norm=final_norm),
        out_shape=jax.ShapeDtypeStruct((b, t, d), F32),
        grid=(b, t // tt, u.shape[0] // ec),
        in_specs=in_specs,
        out_specs=tok,
        scratch_shapes=scratch,
        compiler_params=_cparams(("arbitrary", "arbitrary", "arbitrary"), est),
    )(*args)


def _prep_in_weights(w_in):
    d = w_in.shape[0]
    zeros = lambda n: jnp.zeros((d, n), w_in.dtype)
    cols = []
    for hd in range(4):
        cols += [w_in[:, hd * MLA_QK:(hd + 1) * MLA_QK], zeros(LANES - MLA_QK)]
    cols.append(w_in[:, 384:512])
    cols += [zeros(HEAD_DIM), w_in[:, 512:544], zeros(LANES - HEAD_DIM - MLA_ROPE)]
    for hd in (0, 2, 1, 3):
        cols.append(w_in[:, 544 + hd * HEAD_DIM:544 + (hd + 1) * HEAD_DIM])
    cols.append(w_in[:, 800:])
    return jnp.concatenate(cols, axis=1).astype(BF16)


def _prep_kv_weights(w_ukv):
    zeros = jnp.zeros((MLA_RANK, HEAD_DIM), w_ukv.dtype)
    wkn = jnp.concatenate([blk for hd in range(4) for blk in (w_ukv[:, hd * LANES:hd * LANES + HEAD_DIM], zeros)], axis=1)
    wv = jnp.concatenate([w_ukv[:, hd * LANES + HEAD_DIM:(hd + 1) * LANES] for hd in range(4)], axis=1)
    sel = np.zeros((LANES, 4 * LANES), np.float32)
    for hd in range(4):
        for r in range(MLA_ROPE):
            sel[HEAD_DIM + r, hd * LANES + HEAD_DIM + r] = 1.0
    return wkn.astype(BF16), jnp.asarray(sel, BF16), wv.astype(BF16)


def _rope_lane_tables(n):
    def tables(rot_dim):
        quarter = rot_dim // 4
        inv = ROPE_BASE ** (-jnp.arange(quarter, dtype=F32) / quarter)
        t = jnp.arange(n, dtype=jnp.int32)
        pos = jnp.stack([t // GRID_W, t % GRID_W], axis=-1).astype(F32)
        ang = pos[:, :, None] * inv
        lanes = lambda tb: jnp.concatenate([tb[:, 0], tb[:, 0], tb[:, 1], tb[:, 1]], axis=-1)
        return lanes(jnp.cos(ang)), lanes(jnp.sin(ang))

    c32, s32 = tables(MLA_ROPE)
    c64, s64 = tables(HEAD_DIM)
    ones = lambda w: jnp.ones((n, w), F32)
    zeros = lambda w: jnp.zeros((n, w), F32)
    return (jnp.concatenate([ones(HEAD_DIM), c32, ones(LANES - HEAD_DIM - MLA_ROPE)], axis=-1),
            jnp.concatenate([zeros(HEAD_DIM), s32, zeros(LANES - HEAD_DIM - MLA_ROPE)], axis=-1),
            jnp.concatenate([c64, c64], axis=-1), jnp.concatenate([s64, s64], axis=-1))


_SWA_SLOT_HEADS = (0, 2, 1, 3)


def kernel(x, c, ctx, c_ctx, ada_w, ada_b, norm_mix_g, w_in, mla_kv_norm_g, mla_w_ukv, swa_sink, na_rpb, ret_decay_f, ret_decay_b, ret_gn_f, ret_gn_b, mix_beta, w_out, norm_ffn_g, peer_wq, peer_k1, peer_k2, peer_u, peer_v, final_norm_g):
    b, n, d = x.shape
    n_ctx = ctx.shape[1]
    depth = ada_w.shape[0]
    rows = n // GRID_W

    pad_rows = -(b + 1) % 8
    cc = jnp.concatenate([c, c_ctx[None, :], jnp.zeros((pad_rows, d), F32)], axis=0)
    mod = _ada_modulation(cc, ada_w, ada_b)

    lat_tabs = _rope_lane_tables(n)
    ctx_tabs = (jnp.ones((n_ctx, LANES), F32), jnp.zeros((n_ctx, LANES), F32)) * 2
    slot_heads = np.asarray(_SWA_SLOT_HEADS)
    out_perm = np.arange(d)
    out_perm[256:512] = 256 + (slot_heads[:, None] * HEAD_DIM + np.arange(HEAD_DIM)[None, :]).reshape(-1)
    zero_state = jnp.zeros((b, 2, LANES, LANES), F32)
    pair_lanes = lambda p: jnp.repeat(p.astype(F32), HEAD_DIM).reshape(2, 1, LANES)

    xc = ctx
    for layer in range(depth):
        with_ctx = layer < depth - 1
        last = layer == depth - 1
        chunk = lambda k, lo, hi: mod[layer, lo:hi, k * d:(k + 1) * d][:, None, :]
        mod_l = [chunk(k, 0, b) for k in range(6)]
        mod_c = [jnp.broadcast_to(chunk(k, b, b + 1), (b, 1, d)) for k in range(6)]

        w_cols = _prep_in_weights(w_in[layer])
        wkn, emat, wv = _prep_kv_weights(mla_w_ukv[layer])
        g_mix = norm_mix_g[layer][None, :]
        kvg = mla_kv_norm_g[layer][None, :]
        p_l = _in_projection(x, mod_l[0], mod_l[1], g_mix, w_cols, kvg, wkn, emat, wv, lat_tabs)
        p_c = _in_projection(xc, mod_c[0], mod_c[1], g_mix, w_cols, kvg, wkn, emat, wv, ctx_tabs)
        mq, mk, mv, sq, sk, sv, nq, nk, nv, rq, rk, rv, gf, gb = p_l
        cmq, cmk, cmv, csq, csk, csv, cnq, cnk, cnv, crq, crk, crv, cgf, cgb = p_c

        sink_rows = jnp.broadcast_to(swa_sink[layer].astype(F32)[slot_heads][:, None], (4, LANES))
        bias = _na_bias_tables(na_rpb[layer], rows)
        dec_f, dec_b = pair_lanes(ret_decay_f[layer]), pair_lanes(ret_decay_b[layer])
        gn_f, gn_b = ret_gn_f[layer][None, :], ret_gn_b[layer][None, :]

        ya = _mla_attention(mq, cmk, cmv, mk, mv)
        yb = _swa_attention(sq, sk, sv, csk, csv, sink_rows)
        yc = _na_attention(nq, nk, nv, cnk, cnv, bias)
        cyf, cyb, s_f, s_b = _retention(crq, crk, crv, cgf, cgb, dec_f, dec_b, gn_f, gn_b, zero_state, zero_state)
        ydf, ydb, _, _ = _retention(rq, rk, rv, gf, gb, dec_f, dec_b, gn_f, gn_b, s_f, s_b)

        beta = mix_beta[layer][out_perm][None, :]
        w_o = w_out[layer][out_perm, :].astype(BF16)
        x = _out_projection(x, ya, yb, yc, ydf, ydb, beta, w_o, mod_l[2])

        g_ffn = norm_ffn_g[layer][None, :]
        wqt = peer_wq[layer].T.astype(BF16)
        k1, k2 = peer_k1[layer].astype(BF16), peer_k2[layer].astype(BF16)
        u, vt = peer_u[layer].astype(BF16), peer_v[layer].T.astype(BF16)
        x = _peer_ffn(x, mod_l[3], mod_l[4], mod_l[5], g_ffn, wqt, k1, k2, u, vt,
                      final_norm_g[None, :] if last else None)
        if with_ctx:
            cya, cyb_, cyc = _ctx_attention(cmq, cmk, cmv, csq, csk, csv, sink_rows, cnq, cnk, cnv)
            xc = _out_projection(xc, cya, cyb_, cyc, cyf, cyb, beta, w_o, mod_c[2])
            xc = _peer_ffn(xc, mod_c[3], mod_c[4], mod_c[5], g_ffn, wqt, k1, k2, u, vt)
    return x
```

```python
import functools

import numpy as np
import jax
import jax.numpy as jnp
from jax import lax
from jax.experimental import pallas as pl
from jax.experimental.pallas import tpu as pltpu

F32 = jnp.float32
BF16 = jnp.bfloat16

EPS = 1e-6
NEG_INF = -1e30
ROPE_BASE = 10000.0
GRID_W = 64
HEAD_DIM = 64
LANES = 128
V7X_VMEM_BYTES = 64 * 1024 * 1024
MIB = 1024 * 1024

MLA_ROPE = 32
MLA_QK = 96
MLA_RANK = 128
SWA_WINDOW = 128
NA_KH, NA_KW = 8, 16
NA_ROWS = 4
RET_CHUNK = 128
PEER_HEADS = 8
PEER_NKEYS = 128
PEER_TOPK = 16
PEER_TOKENS = 256
PEER_ROWS = 8
PEER_SUB = 256

_MQ, _CKV, _KR, _SQ, _SK, _SV, _NQ, _NK, _NV, _RQ, _RK, _RV, _GF, _GB, _IN_COLS = (
    0, 512, 640, 768, 1024, 1152, 1280, 1536, 1792, 2048, 2304, 2560, 2816, 3072, 3328)


def _cparams(semantics, vmem_bytes):
    limit = int(min(vmem_bytes, V7X_VMEM_BYTES * 7 // 8))
    return pltpu.CompilerParams(dimension_semantics=semantics, vmem_limit_bytes=limit)


def _dot(a, b):
    return jnp.dot(a, b, preferred_element_type=F32)


def _dot_nt(a, b):
    return lax.dot_general(a, b, (((1,), (1,)), ((), ())), preferred_element_type=F32)


def _dot_tn(a, b):
    return lax.dot_general(a, b, (((0,), (0,)), ((), ())), preferred_element_type=F32)


def _low_half(shape):
    return lax.broadcasted_iota(jnp.int32, shape, len(shape) - 1) < HEAD_DIM


def _half_masks():
    lo = _low_half((1, LANES))
    return (jnp.where(lo, 1.0, 0.0).astype(BF16), jnp.where(lo, 0.0, 1.0).astype(BF16))


def _norm_mod(x, g, shift, scale):
    ms = jnp.mean(x * x, axis=-1, keepdims=True)
    return (x * lax.rsqrt(ms + EPS) * g) * (1.0 + scale) + shift


def _rmsnorm(x, g):
    ms = jnp.mean(x * x, axis=-1, keepdims=True)
    return x * lax.rsqrt(ms + EPS) * g


def _ada_kernel(c_ref, w_ref, b_ref, o_ref):
    c = c_ref[...]
    o_ref[0] = _dot(c * jax.nn.sigmoid(c), w_ref[0]) + b_ref[0]


def _ada_modulation(cc, ada_w, ada_b):
    depth, d, width = ada_w.shape
    rows = cc.shape[0]
    tn = 1024
    return pl.pallas_call(
        _ada_kernel,
        out_shape=jax.ShapeDtypeStruct((depth, rows, width), F32),
        grid=(depth, width // tn),
        in_specs=[pl.BlockSpec((rows, d), lambda l, j: (0, 0)),
                  pl.BlockSpec((1, d, tn), lambda l, j: (l, 0, j)),
                  pl.BlockSpec((1, 1, tn), lambda l, j: (l, 0, j))],
        out_specs=pl.BlockSpec((1, rows, tn), lambda l, j: (l, 0, j)),
        compiler_params=_cparams(("arbitrary", "arbitrary"), 2 * d * tn * 4 + 8 * MIB),
    )(cc, ada_w, ada_b.reshape(depth, 1, width))


def _rope(a, cos, sin, half):
    lane = lax.broadcasted_iota(jnp.int32, a.shape, 1)
    first = (lane % (2 * half)) < half
    rot = jnp.where(first, -pltpu.roll(a, LANES - half, 1), pltpu.roll(a, half, 1))
    return a * cos + rot * sin


def _inproj_kernel(x_ref, sh_ref, sc_ref, g_ref, w_ref, kvg_ref, wkn_ref, e_ref, wv_ref,
                   c32_ref, s32_ref, c64_ref, s64_ref,
                   mq_ref, mk_ref, mv_ref, sq_ref, sk_ref, sv_ref, nq_ref, nk_ref, nv_ref,
                   rq_ref, rk_ref, rv_ref, gf_ref, gb_ref):
    h = _norm_mod(x_ref[0], g_ref[...], sh_ref[0], sc_ref[0]).astype(BF16)

    def seg(lo, width):
        return _dot(h, w_ref[:, lo:lo + width])

    c32, s32, c64, s64 = c32_ref[...], s32_ref[...], c64_ref[...], s64_ref[...]
    mla_scale = MLA_QK ** -0.5
    head_scale = HEAD_DIM ** -0.5
    for hd in range(4):
        a = seg(_MQ + hd * LANES, LANES)
        mq_ref[0, :, hd * LANES:(hd + 1) * LANES] = (_rope(a, c32, s32, MLA_ROPE // 4) * mla_scale).astype(BF16)
    kvn = _rmsnorm(seg(_CKV, MLA_RANK), kvg_ref[...]).astype(BF16)
    kr = _rope(seg(_KR, LANES), c32, s32, MLA_ROPE // 4).astype(BF16)
    mk_ref[0] = (_dot(kvn, wkn_ref[...]) + _dot(kr, e_ref[...])).astype(BF16)
    mv_ref[0] = _dot(kvn, wv_ref[...]).astype(BF16)
    for grp in range(2):
        a = seg(_SQ + grp * LANES, LANES)
        sq_ref[0, :, grp * LANES:(grp + 1) * LANES] = (_rope(a, c64, s64, HEAD_DIM // 4) * head_scale).astype(BF16)
    sk_ref[0] = _rope(seg(_SK, LANES), c64, s64, HEAD_DIM // 4).astype(BF16)
    sv_ref[0] = seg(_SV, LANES).astype(BF16)
    nq_ref[0] = (seg(_NQ, 256) * head_scale).astype(BF16)
    nk_ref[0] = seg(_NK, 256).astype(BF16)
    nv_ref[0] = seg(_NV, 256).astype(BF16)
    rq_ref[0] = seg(_RQ, 256).astype(BF16)
    rk_ref[0] = (seg(_RK, 256) * head_scale).astype(BF16)
    rv_ref[0] = seg(_RV, 256).astype(BF16)
    gf_ref[0] = seg(_GF, 256).astype(BF16)
    gb_ref[0] = seg(_GB, 256).astype(BF16)


_INPROJ_WIDTHS = (512, 512, 256, 256, 128, 128, 256, 256, 256, 256, 256, 256, 256, 256)


def _in_projection(x, shift, scale, g, w, kvg, wkn, emat, wv, tabs):
    b, t, d = x.shape
    tm = min(512, t)
    const = lambda shape: pl.BlockSpec(shape, lambda bi, i: (0,) * len(shape))
    tab = pl.BlockSpec((tm, LANES), lambda bi, i: (i, 0))
    vec = pl.BlockSpec((1, 1, d), lambda bi, i: (bi, 0, 0))
    est = 2 * (tm * d * 4 + d * _IN_COLS * 2 + tm * sum(_INPROJ_WIDTHS) * 2 + 4 * tm * LANES * 4) + 12 * MIB
    return pl.pallas_call(
        _inproj_kernel,
        out_shape=[jax.ShapeDtypeStruct((b, t, wd), BF16) for wd in _INPROJ_WIDTHS],
        grid=(b, t // tm),
        in_specs=[pl.BlockSpec((1, tm, d), lambda bi, i: (bi, i, 0)), vec, vec, const((1, d)),
                  const((d, _IN_COLS)), const((1, MLA_RANK)), const((MLA_RANK, 512)), const((LANES, 512)),
                  const((MLA_RANK, 256)), tab, tab, tab, tab],
        out_specs=[pl.BlockSpec((1, tm, wd), lambda bi, i: (bi, i, 0)) for wd in _INPROJ_WIDTHS],
        compiler_params=_cparams(("arbitrary", "arbitrary"), est),
    )(x, shift, scale, g, w, kvg, wkn, emat, wv, *tabs)


def _attend_pair(qs, pieces, sinks):
    outs = []
    for s in (0, 1):
        scores = []
        for ks, _, biases in pieces:
            sc = _dot_nt(qs[s], ks[s])
            if biases[s] is not None:
                sc = sc + biases[s]
            scores.append(sc)
        m = functools.reduce(jnp.maximum, [jnp.max(sc, axis=-1, keepdims=True) for sc in scores])
        if sinks is not None:
            m = jnp.maximum(m, sinks[s])
        den = jnp.exp(sinks[s] - m) if sinks is not None else 0.0
        o = 0.0
        for sc, (_, v, _) in zip(scores, pieces):
            p = jnp.exp(sc - m)
            den = den + jnp.sum(p, axis=-1, keepdims=True)
            o = o + _dot(p.astype(BF16), v)
        outs.append(o / den)
    return jnp.where(_low_half(outs[0].shape), outs[0], outs[1])


def _mla_kernel(q_ref, kc_ref, vc_ref, kl_ref, vl_ref, o_ref, m_ref, l_ref, acc_ref):
    j = pl.program_id(2)

    @pl.when(j == 0)
    def _():
        m_ref[...] = jnp.full(m_ref.shape, NEG_INF, F32)
        l_ref[...] = jnp.zeros(l_ref.shape, F32)
        acc_ref[...] = jnp.zeros(acc_ref.shape, F32)

    def update(k, v):
        lo = _low_half((1, LANES))
        m0, m1 = _half_masks()
        reps = k.shape[0] // LANES
        for pair in range(2):
            vp = v[:, pair * LANES:(pair + 1) * LANES]
            vals = (vp * m0 + m1, vp * m1 + m0)
            alphas, pvs = [], []
            for s in (0, 1):
                hd = 2 * pair + s
                q = q_ref[0, :, hd * LANES:(hd + 1) * LANES]
                sc = _dot_nt(q, k[:, hd * LANES:(hd + 1) * LANES])
                m_prev = m_ref[hd]
                m_new = jnp.maximum(m_prev, jnp.max(sc, axis=-1, keepdims=True))
                m_ref[hd] = m_new
                alphas.append(jnp.exp(m_prev - m_new))
                p = jnp.exp(sc - jnp.tile(m_new, (1, reps)))
                pvs.append(_dot(p.astype(BF16), vals[s]))
            acc_ref[pair] = (acc_ref[pair] * jnp.where(lo, alphas[0], alphas[1])
                             + jnp.where(lo, pvs[0], pvs[1]))
            l_ref[pair] = (l_ref[pair] * jnp.where(lo, alphas[1], alphas[0])
                           + jnp.where(lo, pvs[1], pvs[0]))

    @pl.when(j == 0)
    def _():
        update(kc_ref[0], vc_ref[0])

    @pl.when(j > 0)
    def _():
        update(kl_ref[0], vl_ref[0])

    @pl.when(j == pl.num_programs(2) - 1)
    def _():
        for pair in range(2):
            inv = pltpu.roll(1.0 / l_ref[pair], HEAD_DIM, 1)
            o_ref[0, :, pair * LANES:(pair + 1) * LANES] = (acc_ref[pair] * inv).astype(BF16)


def _mla_attention(q, kc, vc, kl, vl):
    b, n, _ = q.shape
    c = kc.shape[1]
    tq = min(512, n)
    tk = min(512, n)
    est = 2 * (tq * 512 * 2 + c * 768 * 2 + tk * 768 * 2 + tq * 256 * 2) + 6 * tq * LANES * 4 + 10 * tq * tk * 4 + 8 * MIB
    return pl.pallas_call(
        _mla_kernel,
        out_shape=jax.ShapeDtypeStruct((b, n, 256), BF16),
        grid=(b, n // tq, 1 + n // tk),
        in_specs=[pl.BlockSpec((1, tq, 512), lambda bi, i, j: (bi, i, 0)),
                  pl.BlockSpec((1, c, 512), lambda bi, i, j: (bi, 0, 0)),
                  pl.BlockSpec((1, c, 256), lambda bi, i, j: (bi, 0, 0)),
                  pl.BlockSpec((1, tk, 512), lambda bi, i, j: (bi, jnp.maximum(j - 1, 0), 0)),
                  pl.BlockSpec((1, tk, 256), lambda bi, i, j: (bi, jnp.maximum(j - 1, 0), 0))],
        out_specs=pl.BlockSpec((1, tq, 256), lambda bi, i, j: (bi, i, 0)),
        scratch_shapes=[pltpu.VMEM((4, tq, LANES), F32), pltpu.VMEM((2, tq, LANES), F32), pltpu.VMEM((2, tq, LANES), F32)],
        compiler_params=_cparams(("arbitrary", "arbitrary", "arbitrary"), est),
    )(q, kc, vc, kl, vl)


def _swa_kernel(q_ref, kp_ref, kcur_ref, kn_ref, vp_ref, vcur_ref, vn_ref, kctx_ref, vctx_ref, sink_ref, o_ref):
    i = pl.program_id(1)
    last = pl.num_programs(1) - 1
    tq = q_ref.shape[1]

    def band_bias(width, offset, edge_penalty):
        r = lax.broadcasted_iota(jnp.int32, (tq, width), 0)
        cidx = lax.broadcasted_iota(jnp.int32, (tq, width), 1)
        rel = cidx + offset - r
        inside = jnp.where(rel >= -SWA_WINDOW, jnp.where(rel <= SWA_WINDOW, 1, 0), 0)
        return jnp.where(inside == 1, edge_penalty, NEG_INF)

    halo = kp_ref.shape[1]
    b_prev = band_bias(halo, -halo, jnp.where(i > 0, 0.0, NEG_INF))
    b_cur = band_bias(tq, 0, 0.0)
    b_next = band_bias(halo, tq, jnp.where(i < last, 0.0, NEG_INF))
    m0, m1 = _half_masks()
    kctx, vctx = kctx_ref[0], vctx_ref[0]
    pieces = [((kp_ref[0],) * 2, vp_ref[0], (b_prev,) * 2),
              ((kcur_ref[0],) * 2, vcur_ref[0], (b_cur,) * 2),
              ((kn_ref[0],) * 2, vn_ref[0], (b_next,) * 2),
              ((kctx,) * 2, vctx, (None, None))]
    for grp in range(2):
        qg = q_ref[0, :, grp * LANES:(grp + 1) * LANES]
        sinks = [sink_ref[2 * grp + s:2 * grp + s + 1, 0:1] for s in (0, 1)]
        o_ref[0, :, grp * LANES:(grp + 1) * LANES] = _attend_pair((qg * m0, qg * m1), pieces, sinks).astype(BF16)


def _swa_attention(q, k, v, kc, vc, sink_rows):
    b, n, _ = q.shape
    c = kc.shape[1]
    tq = 256
    halo = SWA_WINDOW
    r = tq // halo
    nh = n // halo
    cur = lambda bi, i: (bi, i, 0)
    prev = lambda bi, i: (bi, jnp.maximum(i * r - 1, 0), 0)
    nxt = lambda bi, i: (bi, jnp.minimum(i * r + r, nh - 1), 0)
    ctx = lambda bi, i: (bi, 0, 0)
    est = 16 * tq * (tq + 2 * halo + c) * 4 + 8 * MIB
    return pl.pallas_call(
        _swa_kernel,
        out_shape=jax.ShapeDtypeStruct((b, n, 256), BF16),
        grid=(b, n // tq),
        in_specs=[pl.BlockSpec((1, tq, 256), cur),
                  pl.BlockSpec((1, halo, LANES), prev), pl.BlockSpec((1, tq, LANES), cur), pl.BlockSpec((1, halo, LANES), nxt),
                  pl.BlockSpec((1, halo, LANES), prev), pl.BlockSpec((1, tq, LANES), cur), pl.BlockSpec((1, halo, LANES), nxt),
                  pl.BlockSpec((1, c, LANES), ctx), pl.BlockSpec((1, c, LANES), ctx),
                  pl.BlockSpec((4, LANES), lambda bi, i: (0, 0))],
        out_specs=pl.BlockSpec((1, tq, 256), cur),
        compiler_params=_cparams(("arbitrary", "arbitrary"), est),
    )(q, k, k, k, v, v, v, kc, vc, sink_rows)


def _na_kernel(q_ref, kp_ref, kcur_ref, kn_ref, vp_ref, vcur_ref, vn_ref, kctx_ref, vctx_ref, bias_ref, o_ref):
    tq = q_ref.shape[1]
    m0, m1 = _half_masks()
    for pair in range(2):
        sl = slice(pair * LANES, (pair + 1) * LANES)
        qp = q_ref[0, :, sl]
        pieces = []
        for idx, (kr, vr) in enumerate(((kp_ref, vp_ref), (kcur_ref, vcur_ref), (kn_ref, vn_ref))):
            biases = tuple(bias_ref[0, 2 * pair + s, :, idx * tq:(idx + 1) * tq] for s in (0, 1))
            pieces.append(((kr[0, :, sl],) * 2, vr[0, :, sl], biases))
        pieces.append(((kctx_ref[0, :, sl],) * 2, vctx_ref[0, :, sl], (None, None)))
        o_ref[0, :, sl] = _attend_pair((qp * m0, qp * m1), pieces, None).astype(BF16)


def _na_bias_tables(rpb, rows):
    nb = rows // NA_ROWS
    heads = rpb.shape[0]
    width = 2 * NA_KW - 1
    span = 2 * GRID_W - 1
    lead = GRID_W - 2 - (NA_KW - 1)
    rpb = rpb.astype(F32)
    per_row = jnp.stack([rpb[:, NA_ROWS - 1 - a:NA_ROWS - 1 - a + 3 * NA_ROWS, :] for a in range(NA_ROWS)], axis=1)
    padded = jnp.pad(per_row, ((0, 0), (0, 0), (0, 0), (lead, span - width - lead)))
    tiled = jnp.broadcast_to(padded[:, :, :, None, :], (heads, NA_ROWS, 3 * NA_ROWS, GRID_W, span))
    skew = tiled.reshape(heads, NA_ROWS, 3 * NA_ROWS, GRID_W * span)[..., :GRID_W * (span - 1)]
    toep = skew.reshape(heads, NA_ROWS, 3 * NA_ROWS, GRID_W, span - 1)[..., GRID_W - 2:]
    table = jnp.transpose(toep, (0, 1, 3, 2, 4)).reshape(heads, NA_ROWS * GRID_W, 3 * NA_ROWS * GRID_W)
    a = np.arange(NA_ROWS)[:, None, None, None]
    cq = np.arange(GRID_W)[None, :, None, None]
    kr = np.arange(3 * NA_ROWS)[None, None, :, None]
    ck = np.arange(GRID_W)[None, None, None, :]
    valids = []
    for j in (0, 1, nb - 1):
        r = NA_ROWS * j + a
        rs = np.clip(r - NA_KH // 2, 0, rows - NA_KH)
        rk = NA_ROWS * (j - 1) + kr
        cs = np.clip(cq - NA_KW // 2, 0, GRID_W - NA_KW)
        valid = (rk >= rs) & (rk < rs + NA_KH) & (ck >= cs) & (ck < cs + NA_KW)
        valids.append(valid.reshape(NA_ROWS * GRID_W, 3 * NA_ROWS * GRID_W))
    valid = jnp.asarray(np.stack(valids))
    return jnp.where(valid[:, None], table[None], NEG_INF)


def _na_attention(q, k, v, kc, vc, bias):
    b, n, _ = q.shape
    c = kc.shape[1]
    tq = NA_ROWS * GRID_W
    nb = n // tq
    cur = lambda bi, i: (bi, i, 0)
    prev = lambda bi, i: (bi, jnp.maximum(i - 1, 0), 0)
    nxt = lambda bi, i: (bi, jnp.minimum(i + 1, nb - 1), 0)
    ctx = lambda bi, i: (bi, 0, 0)
    variant = lambda bi, i: (jnp.where(i == 0, 0, jnp.where(i == nb - 1, 2, 1)), 0, 0, 0)
    blk = pl.BlockSpec((1, tq, 256), cur)
    est = 2 * 4 * tq * 3 * tq * 4 + 16 * tq * (3 * tq + c) * 4 + 8 * MIB
    return pl.pallas_call(
        _na_kernel,
        out_shape=jax.ShapeDtypeStruct((b, n, 256), BF16),
        grid=(b, nb),
        in_specs=[blk, pl.BlockSpec((1, tq, 256), prev), blk, pl.BlockSpec((1, tq, 256), nxt),
                  pl.BlockSpec((1, tq, 256), prev), blk, pl.BlockSpec((1, tq, 256), nxt),
                  pl.BlockSpec((1, c, 256), ctx), pl.BlockSpec((1, c, 256), ctx),
                  pl.BlockSpec((1, 4, tq, 3 * tq), variant)],
        out_specs=blk,
        compiler_params=_cparams(("arbitrary", "arbitrary"), est),
    )(q, k, k, k, v, v, v, kc, vc, bias)


def _ctx_attn_kernel(mq_ref, mk_ref, mv_ref, sq_ref, sk_ref, sv_ref, sink_ref, nq_ref, nk_ref, nv_ref,
                     ya_ref, yb_ref, yc_ref):
    m0, m1 = _half_masks()
    none2 = (None, None)
    for pair in range(2):
        sl = slice(pair * LANES, (pair + 1) * LANES)
        h0 = slice(2 * pair * LANES, (2 * pair + 1) * LANES)
        h1 = slice((2 * pair + 1) * LANES, (2 * pair + 2) * LANES)
        ya_ref[0, :, sl] = _attend_pair((mq_ref[0, :, h0], mq_ref[0, :, h1]),
                                        [((mk_ref[0, :, h0], mk_ref[0, :, h1]), mv_ref[0, :, sl], none2)],
                                        None).astype(BF16)
        qg = sq_ref[0, :, sl]
        sinks = [sink_ref[2 * pair + s:2 * pair + s + 1, 0:1] for s in (0, 1)]
        yb_ref[0, :, sl] = _attend_pair((qg * m0, qg * m1), [((sk_ref[0],) * 2, sv_ref[0], none2)],
                                        sinks).astype(BF16)
        qn = nq_ref[0, :, sl]
        yc_ref[0, :, sl] = _attend_pair((qn * m0, qn * m1), [((nk_ref[0, :, sl],) * 2, nv_ref[0, :, sl], none2)],
                                        None).astype(BF16)


def _ctx_attention(mq, mk, mv, sq, sk, sv, sink_rows, nq, nk, nv):
    b, c, _ = mq.shape
    spec = lambda wd: pl.BlockSpec((1, c, wd), lambda bi: (bi, 0, 0))
    return pl.pallas_call(
        _ctx_attn_kernel,
        out_shape=[jax.ShapeDtypeStruct((b, c, 256), BF16)] * 3,
        grid=(b,),
        in_specs=[spec(512), spec(512), spec(256), spec(256), spec(LANES), spec(LANES),
                  pl.BlockSpec((4, LANES), lambda bi: (0, 0)), spec(256), spec(256), spec(256)],
        out_specs=[spec(256)] * 3,
        compiler_params=_cparams(("arbitrary",), 24 * MIB),
    )(mq, mk, mv, sq, sk, sv, sink_rows, nq, nk, nv)


def _ret_chunk(q, k, v, g, dec_lane, gn_lane, s_scr, reverse):
    c = RET_CHUNK
    lo = _low_half((1, LANES))
    lg = -(jnp.maximum(-dec_lane, 0.0) + jnp.log1p(jnp.exp(-jnp.abs(dec_lane))))
    ti = lax.broadcasted_iota(jnp.int32, (c, LANES), 0).astype(F32)
    if reverse:
        qpow, kpow = c - ti, ti
    else:
        qpow, kpow = ti + 1.0, c - 1.0 - ti
    qdec = jnp.exp(lg * qpow)
    kdec = jnp.exp(lg * kpow)
    cdec = jnp.exp(lg * float(c))
    ii = lax.broadcasted_iota(jnp.int32, (c, c), 0)
    jj = lax.broadcasted_iota(jnp.int32, (c, c), 1)
    dist = (jj - ii) if reverse else (ii - jj)
    distf = jnp.maximum(dist, 0).astype(F32)
    m0, m1 = _half_masks()
    outs = []
    for s, msk in ((0, m0), (1, m1)):
        lg_s = lg[:, s * HEAD_DIM:s * HEAD_DIM + 1]
        decay = jnp.where(dist >= 0, jnp.exp(lg_s * distf), 0.0)
        inner = _dot_nt(q * msk, k) * decay
        outs.append(_dot(inner.astype(BF16), v))
    state = s_scr[...]
    o = jnp.where(lo, outs[0], outs[1]) + _dot((q.astype(F32) * qdec).astype(BF16), state.astype(BF16))
    kd = (k.astype(F32) * kdec).astype(BF16)
    same_head = (ii < HEAD_DIM) == (jj < HEAD_DIM)
    s_scr[...] = state * cdec + jnp.where(same_head, _dot_tn(kd, v), 0.0)

    def head_mean(x):
        s_lo = jnp.sum(jnp.where(lo, x, 0.0), axis=-1, keepdims=True)
        s_hi = jnp.sum(jnp.where(lo, 0.0, x), axis=-1, keepdims=True)
        return jnp.where(lo, s_lo, s_hi) * (1.0 / HEAD_DIM)

    dev = o - head_mean(o)
    normed = dev * lax.rsqrt(head_mean(dev * dev) + EPS)
    gf = g.astype(F32)
    return (gf * jax.nn.sigmoid(gf)) * (normed * gn_lane)


def _ret_kernel(decf_ref, decb_ref, gnf_ref, gnb_ref, s0f_ref, s0b_ref,
                qf_ref, kf_ref, vf_ref, gf_ref, qb_ref, kb_ref, vb_ref, gb_ref,
                yf_ref, yb_ref, sf_ref, sb_ref, sf_scr, sb_scr):
    t = pl.program_id(2)

    @pl.when(t == 0)
    def _():
        sf_scr[...] = s0f_ref[0, 0]
        sb_scr[...] = s0b_ref[0, 0]

    yf_ref[0] = _ret_chunk(qf_ref[0], kf_ref[0], vf_ref[0], gf_ref[0], decf_ref[0], gnf_ref[...], sf_scr,
                           False).astype(BF16)
    yb_ref[0] = _ret_chunk(qb_ref[0], kb_ref[0], vb_ref[0], gb_ref[0], decb_ref[0], gnb_ref[...], sb_scr,
                           True).astype(BF16)

    @pl.when(t == pl.num_programs(2) - 1)
    def _():
        sf_ref[0, 0] = sf_scr[...]
        sb_ref[0, 0] = sb_scr[...]


def _retention(q, k, v, gf, gb, dec_f, dec_b, gn_f, gn_b, s0f, s0b):
    b, t, _ = q.shape
    c = RET_CHUNK
    nt = t // c
    fwd = pl.BlockSpec((1, c, LANES), lambda bi, p, i: (bi, i, p))
    bwd = pl.BlockSpec((1, c, LANES), lambda bi, p, i: (bi, nt - 1 - i, p))
    dec = pl.BlockSpec((1, 1, LANES), lambda bi, p, i: (p, 0, 0))
    gn = pl.BlockSpec((1, LANES), lambda bi, p, i: (0, p))
    st = pl.BlockSpec((1, 1, LANES, LANES), lambda bi, p, i: (bi, p, 0, 0))
    return pl.pallas_call(
        _ret_kernel,
        out_shape=[jax.ShapeDtypeStruct((b, t, 256), BF16)] * 2 + [jax.ShapeDtypeStruct((b, 2, LANES, LANES), F32)] * 2,
        grid=(b, 2, nt),
        in_specs=[dec, dec, gn, gn, st, st, fwd, fwd, fwd, fwd, bwd, bwd, bwd, bwd],
        out_specs=[fwd, bwd, st, st],
        scratch_shapes=[pltpu.VMEM((LANES, LANES), F32)] * 2,
        compiler_params=_cparams(("arbitrary", "arbitrary", "arbitrary"), 16 * MIB),
    )(dec_f, dec_b, gn_f, gn_b, s0f, s0b, q, k, v, gf, q, k, v, gb)


def _outproj_kernel(x_ref, ya_ref, yb_ref, yc_ref, ydf_ref, ydb_ref, beta_ref, w_ref, gate_ref, o_ref):
    ys = (ya_ref[0].astype(F32), yb_ref[0].astype(F32), yc_ref[0].astype(F32),
          ydf_ref[0].astype(F32) + ydb_ref[0].astype(F32))
    acc = 0.0
    for s, y in enumerate(ys):
        sl = slice(s * 256, (s + 1) * 256)
        acc = acc + _dot((y * beta_ref[:, sl]).astype(BF16), w_ref[sl, :])
    o_ref[0] = x_ref[0] + gate_ref[0] * acc


def _out_projection(x, ya, yb, yc, ydf, ydb, beta, w, gate):
    b, t, d = x.shape
    tm = min(512, t)
    row = lambda wd: pl.BlockSpec((1, tm, wd), lambda bi, i: (bi, i, 0))
    est = 2 * (2 * tm * d * 4 + 5 * tm * 256 * 2 + d * d * 2) + 8 * MIB
    return pl.pallas_call(
        _outproj_kernel,
        out_shape=jax.ShapeDtypeStruct((b, t, d), F32),
        grid=(b, t // tm),
        in_specs=[row(d), row(256), row(256), row(256), row(256), row(256),
                  pl.BlockSpec((1, d), lambda bi, i: (0, 0)), pl.BlockSpec((d, d), lambda bi, i: (0, 0)),
                  pl.BlockSpec((1, 1, d), lambda bi, i: (bi, 0, 0))],
        out_specs=row(d),
        compiler_params=_cparams(("arbitrary", "arbitrary"), est),
    )(x, ya, yb, yc, ydf, ydb, beta, w, gate)


def _top_values(work, out_ref, count):
    for r in range(count):
        m = jnp.max(work, axis=0, keepdims=True)
        out_ref[r:r + 1, :] = m
        work = jnp.where(work == m, -jnp.inf, work)


def _peer_route(hd, q_scr, k1_ref, k2_ref, s1_scr, s2_scr, e1_scr, e2_scr, thr_scr, v1_scr, v2_scr, top_scr):
    half = PEER_NKEYS
    base = pl.multiple_of(hd * 2 * half, 2 * half)
    s1 = _dot(k1_ref[...], q_scr[pl.ds(base, half), :].astype(BF16))
    s2 = _dot(k2_ref[...], q_scr[pl.ds(base + half, half), :].astype(BF16))
    s1_scr[hd] = s1
    s2_scr[hd] = s2
    _top_values(s1, v1_scr, PEER_TOPK)
    _top_values(s2, v2_scr, PEER_TOPK)
    v1, v2 = v1_scr[...], v2_scr[...]
    row8 = lax.broadcasted_iota(jnp.int32, (8, v1.shape[1]), 0)
    cands = [v1[0:1] + v2]
    for a in range(1, 8):
        cands.append(jnp.where(row8 < PEER_TOPK // (a + 1), v1[a:a + 1] + v2[0:8], -jnp.inf))
    cands.append(v1[8:16] + v2[0:1])
    _top_values(jnp.concatenate(cands, axis=0), top_scr, PEER_TOPK)
    top = top_scr[...]
    z = jnp.sum(jnp.exp(top - top[0:1]), axis=0, keepdims=True)
    thr_scr[pl.ds(hd, 1), :] = top[PEER_TOPK - 1:PEER_TOPK]
    e1_scr[hd] = jnp.exp(s1 - v1[0:1]) / z
    e2_scr[hd] = jnp.exp(s2 - v2[0:1])


def _peer_kernel(*refs, final_norm):
    if final_norm:
        (x_ref, sh_ref, sc_ref, gt_ref, g_ref, wqt_ref, k1_ref, k2_ref, u_ref, vt_ref, fg_ref, o_ref,
         ht_scr, q_scr, s1_scr, s2_scr, e1_scr, e2_scr, thr_scr, v1_scr, v2_scr, top_scr, acc_scr) = refs
    else:
        (x_ref, sh_ref, sc_ref, gt_ref, g_ref, wqt_ref, k1_ref, k2_ref, u_ref, vt_ref, o_ref,
         ht_scr, q_scr, s1_scr, s2_scr, e1_scr, e2_scr, thr_scr, v1_scr, v2_scr, top_scr, acc_scr) = refs
        fg_ref = None
    step = pl.program_id(2)
    tokens = ht_scr.shape[1]

    @pl.when(step == 0)
    def _():
        h = _norm_mod(x_ref[0], g_ref[...], sh_ref[0], sc_ref[0])
        ht = jnp.transpose(h).astype(BF16)
        ht_scr[...] = ht
        q_scr[...] = _dot(wqt_ref[...], ht)
        acc_scr[...] = jnp.zeros(acc_scr.shape, F32)

        def route(hd, carry):
            _peer_route(hd, q_scr, k1_ref, k2_ref, s1_scr, s2_scr, e1_scr, e2_scr, thr_scr,
                        v1_scr, v2_scr, top_scr)
            return carry

        lax.fori_loop(0, PEER_HEADS, route, 0)

    def gate_rows(i):
        gate = jnp.zeros((PEER_NKEYS, tokens), F32)
        for hd in range(PEER_HEADS):
            total = s2_scr[hd] + s1_scr[hd, pl.ds(i, 1), :]
            val = e2_scr[hd] * e1_scr[hd, pl.ds(i, 1), :]
            gate = gate + jnp.where(total >= thr_scr[hd:hd + 1, :], val, 0.0)
        return gate

    ht = ht_scr[...]
    sub_rows = PEER_SUB // PEER_NKEYS
    y = None
    for k in range(PEER_ROWS // sub_rows):
        sl = slice(k * PEER_SUB, (k + 1) * PEER_SUB)
        gate = jnp.concatenate([gate_rows(step * PEER_ROWS + k * sub_rows + r) for r in range(sub_rows)], axis=0)
        a = _dot(u_ref[sl, :], ht)
        gelu = 0.5 * a * (1.0 + lax.erf(a * np.float32(np.sqrt(0.5))))
        part = _dot(vt_ref[:, sl], (gelu * gate).astype(BF16))
        y = part if y is None else y + part
    acc_scr[...] += y

    @pl.when(step == pl.num_programs(2) - 1)
    def _():
        out = x_ref[0] + gt_ref[0] * jnp.transpose(acc_scr[...])
        if final_norm:
            out = _rmsnorm(out, fg_ref[...])
        o_ref[0] = out


def _peer_ffn(x, shift, scale, gate, g, wqt, k1, k2, u, vt, final_g=None):
    b, t, d = x.shape
    tt = PEER_TOKENS
    ec = PEER_ROWS * PEER_NKEYS
    nq = wqt.shape[0]
    final_norm = final_g is not None
    tok = pl.BlockSpec((1, tt, d), lambda bi, i, s: (bi, i, 0))
    vec = pl.BlockSpec((1, 1, d), lambda bi, i, s: (bi, 0, 0))
    const = lambda shape: pl.BlockSpec(shape, lambda bi, i, s: (0,) * len(shape))
    in_specs = [tok, vec, vec, vec, const((1, d)), const((nq, d)), const(k1.shape), const(k2.shape),
                pl.BlockSpec((ec, d), lambda bi, i, s: (s, 0)), pl.BlockSpec((d, ec), lambda bi, i, s: (0, s))]
    args = [x, shift, scale, gate, g, wqt, k1, k2, u, vt]
    if final_norm:
        in_specs.append(const((1, d)))
        args.append(final_g)
    table = pltpu.VMEM((PEER_HEADS, PEER_NKEYS, tt), F32)
    scratch = [pltpu.VMEM((d, tt), BF16), pltpu.VMEM((nq, tt), F32), table, table, table, table,
               pltpu.VMEM((PEER_HEADS, tt), F32), pltpu.VMEM((PEER_TOPK, tt), F32), pltpu.VMEM((PEER_TOPK, tt), F32),
               pltpu.VMEM((PEER_TOPK, tt), F32), pltpu.VMEM((d, tt), F32)]
    est = (4 * tt * d * 4 + 2 * nq * d * 2 + 4 * ec * d * 2 + tt * d * 2 + nq * tt * 4
           + 4 * PEER_HEADS * PEER_NKEYS * tt * 4 + ec * tt * 4 + d * tt * 4 + 6 * ec * tt * 4 + 8 * MIB)
    return pl.pallas_call(
        functools.partial(_peer_kernel, final_norm=final_norm),
        out_shape=jax.ShapeDtypeStruct((b, t, d), F32),
        grid=(b, t // tt, u.shape[0] // ec),
        in_specs=in_specs,
        out_specs=tok,
        scratch_shapes=scratch,
        compiler_params=_cparams(("arbitrary", "arbitrary", "arbitrary"), est),
    )(*args)


def _prep_in_weights(w_in):
    d = w_in.shape[0]
    zeros = lambda n: jnp.zeros((d, n), w_in.dtype)
    cols = []
    for hd in range(4):
        cols += [w_in[:, hd * MLA_QK:(hd + 1) * MLA_QK], zeros(LANES - MLA_QK)]
    cols.append(w_in[:, 384:512])
    cols += [zeros(HEAD_DIM), w_in[:, 512:544], zeros(LANES - HEAD_DIM - MLA_ROPE)]
    for hd in (0, 2, 1, 3):
        cols.append(w_in[:, 544 + hd * HEAD_DIM:544 + (hd + 1) * HEAD_DIM])
    cols.append(w_in[:, 800:])
    return jnp.concatenate(cols, axis=1).astype(BF16)


def _prep_kv_weights(w_ukv):
    zeros = jnp.zeros((MLA_RANK, HEAD_DIM), w_ukv.dtype)
    wkn = jnp.concatenate([blk for hd in range(4) for blk in (w_ukv[:, hd * LANES:hd * LANES + HEAD_DIM], zeros)], axis=1)
    wv = jnp.concatenate([w_ukv[:, hd * LANES + HEAD_DIM:(hd + 1) * LANES] for hd in range(4)], axis=1)
    sel = np.zeros((LANES, 4 * LANES), np.float32)
    for hd in range(4):
        for r in range(MLA_ROPE):
            sel[HEAD_DIM + r, hd * LANES + HEAD_DIM + r] = 1.0
    return wkn.astype(BF16), jnp.asarray(sel, BF16), wv.astype(BF16)


def _rope_lane_tables(n):
    def tables(rot_dim):
        quarter = rot_dim // 4
        inv = ROPE_BASE ** (-jnp.arange(quarter, dtype=F32) / quarter)
        t = jnp.arange(n, dtype=jnp.int32)
        pos = jnp.stack([t // GRID_W, t % GRID_W], axis=-1).astype(F32)
        ang = pos[:, :, None] * inv
        lanes = lambda tb: jnp.concatenate([tb[:, 0], tb[:, 0], tb[:, 1], tb[:, 1]], axis=-1)
        return lanes(jnp.cos(ang)), lanes(jnp.sin(ang))

    c32, s32 = tables(MLA_ROPE)
    c64, s64 = tables(HEAD_DIM)
    ones = lambda w: jnp.ones((n, w), F32)
    zeros = lambda w: jnp.zeros((n, w), F32)
    return (jnp.concatenate([ones(HEAD_DIM), c32, ones(LANES - HEAD_DIM - MLA_ROPE)], axis=-1),
            jnp.concatenate([zeros(HEAD_DIM), s32, zeros(LANES - HEAD_DIM - MLA_ROPE)], axis=-1),
            jnp.concatenate([c64, c64], axis=-1), jnp.concatenate([s64, s64], axis=-1))


_SWA_SLOT_HEADS = (0, 2, 1, 3)


def kernel(x, c, ctx, c_ctx, ada_w, ada_b, norm_mix_g, w_in, mla_kv_norm_g, mla_w_ukv, swa_sink, na_rpb, ret_decay_f, ret_decay_b, ret_gn_f, ret_gn_b, mix_beta, w_out, norm_ffn_g, peer_wq, peer_k1, peer_k2, peer_u, peer_v, final_norm_g):
    b, n, d = x.shape
    n_ctx = ctx.shape[1]
    depth = ada_w.shape[0]
    rows = n // GRID_W

    pad_rows = -(b + 1) % 8
    cc = jnp.concatenate([c, c_ctx[None, :], jnp.zeros((pad_rows, d), F32)], axis=0)
    mod = _ada_modulation(cc, ada_w, ada_b)

    lat_tabs = _rope_lane_tables(n)
    ctx_tabs = (jnp.ones((n_ctx, LANES), F32), jnp.zeros((n_ctx, LANES), F32)) * 2
    slot_heads = np.asarray(_SWA_SLOT_HEADS)
    out_perm = np.arange(d)
    out_perm[256:512] = 256 + (slot_heads[:, None] * HEAD_DIM + np.arange(HEAD_DIM)[None, :]).reshape(-1)
    zero_state = jnp.zeros((b, 2, LANES, LANES), F32)
    pair_lanes = lambda p: jnp.repeat(p.astype(F32), HEAD_DIM).reshape(2, 1, LANES)

    xc = ctx
    for layer in range(depth):
        with_ctx = layer < depth - 1
        last = layer == depth - 1
        chunk = lambda k, lo, hi: mod[layer, lo:hi, k * d:(k + 1) * d][:, None, :]
        mod_l = [chunk(k, 0, b) for k in range(6)]
        mod_c = [jnp.broadcast_to(chunk(k, b, b + 1), (b, 1, d)) for k in range(6)]

        w_cols = _prep_in_weights(w_in[layer])
        wkn, emat, wv = _prep_kv_weights(mla_w_ukv[layer])
        g_mix = norm_mix_g[layer][None, :]
        kvg = mla_kv_norm_g[layer][None, :]
        p_l = _in_projection(x, mod_l[0], mod_l[1], g_mix, w_cols, kvg, wkn, emat, wv, lat_tabs)
        p_c = _in_projection(xc, mod_c[0], mod_c[1], g_mix, w_cols, kvg, wkn, emat, wv, ctx_tabs)
        mq, mk, mv, sq, sk, sv, nq, nk, nv, rq, rk, rv, gf, gb = p_l
        cmq, cmk, cmv, csq, csk, csv, cnq, cnk, cnv, crq, crk, crv, cgf, cgb = p_c

        sink_rows = jnp.broadcast_to(swa_sink[layer].astype(F32)[slot_heads][:, None], (4, LANES))
        bias = _na_bias_tables(na_rpb[layer], rows)
        dec_f, dec_b = pair_lanes(ret_decay_f[layer]), pair_lanes(ret_decay_b[layer])
        gn_f, gn_b = ret_gn_f[layer][None, :], ret_gn_b[layer][None, :]

        ya = _mla_attention(mq, cmk, cmv, mk, mv)
        yb = _swa_attention(sq, sk, sv, csk, csv, sink_rows)
        yc = _na_attention(nq, nk, nv, cnk, cnv, bias)
        cyf, cyb, s_f, s_b = _retention(crq, crk, crv, cgf, cgb, dec_f, dec_b, gn_f, gn_b, zero_state, zero_state)
        ydf, ydb, _, _ = _retention(rq, rk, rv, gf, gb, dec_f, dec_b, gn_f, gn_b, s_f, s_b)

        beta = mix_beta[layer][out_perm][None, :]
        w_o = w_out[layer][out_perm, :].astype(BF16)
        x = _out_projection(x, ya, yb, yc, ydf, ydb, beta, w_o, mod_l[2])

        g_ffn = norm_ffn_g[layer][None, :]
        wqt = peer_wq[layer].T.astype(BF16)
        k1, k2 = peer_k1[layer].astype(BF16), peer_k2[layer].astype(BF16)
        u, vt = peer_u[layer].astype(BF16), peer_v[layer].T.astype(BF16)
        x = _peer_ffn(x, mod_l[3], mod_l[4], mod_l[5], g_ffn, wqt, k1, k2, u, vt,
                      final_norm_g[None, :] if last else None)
        if with_ctx:
            cya, cyb_, cyc = _ctx_attention(cmq, cmk, cmv, csq, csk, csv, sink_rows, cnq, cnk, cnv)
            xc = _out_projection(xc, cya, cyb_, cyc, cyf, cyb, beta, w_o, mod_c[2])
            xc = _peer_ffn(xc, mod_c[3], mod_c[4], mod_c[5], g_ffn, wqt, k1, k2, u, vt)
    return x
```

```python
import functools

import numpy as np
import jax
import jax.numpy as jnp
from jax import lax
from jax.experimental import pallas as pl
from jax.experimental.pallas import tpu as pltpu

F32 = jnp.float32
BF16 = jnp.bfloat16

EPS = 1e-6
NEG_INF = -1e30
ROPE_BASE = 10000.0
GRID_W = 64
HEAD_DIM = 64
LANES = 128
V7X_VMEM_BYTES = 64 * 1024 * 1024
MIB = 1024 * 1024

MLA_ROPE = 32
MLA_QK = 96
MLA_RANK = 128
SWA_WINDOW = 128
NA_KH, NA_KW = 8, 16
NA_ROWS = 4
RET_CHUNK = 128
PEER_HEADS = 8
PEER_NKEYS = 128
PEER_TOPK = 16
PEER_TOKENS = 256
PEER_ROWS = 8

_MQ, _CKV, _KR, _SQ, _SK, _SV, _NQ, _NK, _NV, _RQ, _RK, _RV, _GF, _GB, _IN_COLS = (
    0, 512, 640, 768, 1024, 1152, 1280, 1536, 1792, 2048, 2304, 2560, 2816, 3072, 3328)


def _cparams(semantics, vmem_bytes):
    limit = int(min(vmem_bytes, V7X_VMEM_BYTES * 7 // 8))
    return pltpu.CompilerParams(dimension_semantics=semantics, vmem_limit_bytes=limit)


def _dot(a, b):
    return jnp.dot(a, b, preferred_element_type=F32)


def _dot_nt(a, b):
    return lax.dot_general(a, b, (((1,), (1,)), ((), ())), preferred_element_type=F32)


def _dot_tn(a, b):
    return lax.dot_general(a, b, (((0,), (0,)), ((), ())), preferred_element_type=F32)


def _low_half(shape):
    return lax.broadcasted_iota(jnp.int32, shape, len(shape) - 1) < HEAD_DIM


def _half_masks():
    lo = _low_half((1, LANES))
    return (jnp.where(lo, 1.0, 0.0).astype(BF16), jnp.where(lo, 0.0, 1.0).astype(BF16))


def _norm_mod(x, g, shift, scale):
    ms = jnp.mean(x * x, axis=-1, keepdims=True)
    return (x * lax.rsqrt(ms + EPS) * g) * (1.0 + scale) + shift


def _rmsnorm(x, g):
    ms = jnp.mean(x * x, axis=-1, keepdims=True)
    return x * lax.rsqrt(ms + EPS) * g


def _ada_kernel(c_ref, w_ref, b_ref, o_ref):
    c = c_ref[...]
    o_ref[0] = _dot(c * jax.nn.sigmoid(c), w_ref[0]) + b_ref[0]


def _ada_modulation(cc, ada_w, ada_b):
    depth, d, width = ada_w.shape
    rows = cc.shape[0]
    tn = 1024
    return pl.pallas_call(
        _ada_kernel,
        out_shape=jax.ShapeDtypeStruct((depth, rows, width), F32),
        grid=(depth, width // tn),
        in_specs=[pl.BlockSpec((rows, d), lambda l, j: (0, 0)),
                  pl.BlockSpec((1, d, tn), lambda l, j: (l, 0, j)),
                  pl.BlockSpec((1, 1, tn), lambda l, j: (l, 0, j))],
        out_specs=pl.BlockSpec((1, rows, tn), lambda l, j: (l, 0, j)),
        compiler_params=_cparams(("arbitrary", "arbitrary"), 2 * d * tn * 4 + 8 * MIB),
    )(cc, ada_w, ada_b.reshape(depth, 1, width))


def _rope(a, cos, sin, half):
    lane = lax.broadcasted_iota(jnp.int32, a.shape, 1)
    first = (lane % (2 * half)) < half
    rot = jnp.where(first, -pltpu.roll(a, LANES - half, 1), pltpu.roll(a, half, 1))
    return a * cos + rot * sin


def _inproj_kernel(x_ref, sh_ref, sc_ref, g_ref, w_ref, kvg_ref, wkn_ref, e_ref, wv_ref,
                   c32_ref, s32_ref, c64_ref, s64_ref,
                   mq_ref, mk_ref, mv_ref, sq_ref, sk_ref, sv_ref, nq_ref, nk_ref, nv_ref,
                   rq_ref, rk_ref, rv_ref, gf_ref, gb_ref):
    h = _norm_mod(x_ref[0], g_ref[...], sh_ref[0], sc_ref[0]).astype(BF16)

    def seg(lo, width):
        return _dot(h, w_ref[:, lo:lo + width])

    c32, s32, c64, s64 = c32_ref[...], s32_ref[...], c64_ref[...], s64_ref[...]
    mla_scale = MLA_QK ** -0.5
    head_scale = HEAD_DIM ** -0.5
    for hd in range(4):
        a = seg(_MQ + hd * LANES, LANES)
        mq_ref[0, :, hd * LANES:(hd + 1) * LANES] = (_rope(a, c32, s32, MLA_ROPE // 4) * mla_scale).astype(BF16)
    kvn = _rmsnorm(seg(_CKV, MLA_RANK), kvg_ref[...]).astype(BF16)
    kr = _rope(seg(_KR, LANES), c32, s32, MLA_ROPE // 4).astype(BF16)
    mk_ref[0] = (_dot(kvn, wkn_ref[...]) + _dot(kr, e_ref[...])).astype(BF16)
    mv_ref[0] = _dot(kvn, wv_ref[...]).astype(BF16)
    for grp in range(2):
        a = seg(_SQ + grp * LANES, LANES)
        sq_ref[0, :, grp * LANES:(grp + 1) * LANES] = (_rope(a, c64, s64, HEAD_DIM // 4) * head_scale).astype(BF16)
    sk_ref[0] = _rope(seg(_SK, LANES), c64, s64, HEAD_DIM // 4).astype(BF16)
    sv_ref[0] = seg(_SV, LANES).astype(BF16)
    nq_ref[0] = (seg(_NQ, 256) * head_scale).astype(BF16)
    nk_ref[0] = seg(_NK, 256).astype(BF16)
    nv_ref[0] = seg(_NV, 256).astype(BF16)
    rq_ref[0] = seg(_RQ, 256).astype(BF16)
    rk_ref[0] = (seg(_RK, 256) * head_scale).astype(BF16)
    rv_ref[0] = seg(_RV, 256).astype(BF16)
    gf_ref[0] = seg(_GF, 256).astype(BF16)
    gb_ref[0] = seg(_GB, 256).astype(BF16)


_INPROJ_WIDTHS = (512, 512, 256, 256, 128, 128, 256, 256, 256, 256, 256, 256, 256, 256)


def _in_projection(x, shift, scale, g, w, kvg, wkn, emat, wv, tabs):
    b, t, d = x.shape
    tm = min(512, t)
    const = lambda shape: pl.BlockSpec(shape, lambda bi, i: (0,) * len(shape))
    tab = pl.BlockSpec((tm, LANES), lambda bi, i: (i, 0))
    vec = pl.BlockSpec((1, 1, d), lambda bi, i: (bi, 0, 0))
    est = 2 * (tm * d * 4 + d * _IN_COLS * 2 + tm * sum(_INPROJ_WIDTHS) * 2 + 4 * tm * LANES * 4) + 12 * MIB
    return pl.pallas_call(
        _inproj_kernel,
        out_shape=[jax.ShapeDtypeStruct((b, t, wd), BF16) for wd in _INPROJ_WIDTHS],
        grid=(b, t // tm),
        in_specs=[pl.BlockSpec((1, tm, d), lambda bi, i: (bi, i, 0)), vec, vec, const((1, d)),
                  const((d, _IN_COLS)), const((1, MLA_RANK)), const((MLA_RANK, 512)), const((LANES, 512)),
                  const((MLA_RANK, 256)), tab, tab, tab, tab],
        out_specs=[pl.BlockSpec((1, tm, wd), lambda bi, i: (bi, i, 0)) for wd in _INPROJ_WIDTHS],
        compiler_params=_cparams(("arbitrary", "arbitrary"), est),
    )(x, shift, scale, g, w, kvg, wkn, emat, wv, *tabs)


def _attend_pair(qs, pieces, sinks):
    outs = []
    for s in (0, 1):
        scores = []
        for ks, _, biases in pieces:
            sc = _dot_nt(qs[s], ks[s])
            if biases[s] is not None:
                sc = sc + biases[s]
            scores.append(sc)
        m = functools.reduce(jnp.maximum, [jnp.max(sc, axis=-1, keepdims=True) for sc in scores])
        if sinks is not None:
            m = jnp.maximum(m, sinks[s])
        den = jnp.exp(sinks[s] - m) if sinks is not None else 0.0
        o = 0.0
        for sc, (_, v, _) in zip(scores, pieces):
            p = jnp.exp(sc - m)
            den = den + jnp.sum(p, axis=-1, keepdims=True)
            o = o + _dot(p.astype(BF16), v)
        outs.append(o / den)
    return jnp.where(_low_half(outs[0].shape), outs[0], outs[1])


def _mla_kernel(q_ref, kc_ref, vc_ref, kl_ref, vl_ref, o_ref, m_ref, l_ref, acc_ref):
    j = pl.program_id(2)

    @pl.when(j == 0)
    def _():
        m_ref[...] = jnp.full(m_ref.shape, NEG_INF, F32)
        l_ref[...] = jnp.zeros(l_ref.shape, F32)
        acc_ref[...] = jnp.zeros(acc_ref.shape, F32)

    def update(k, v):
        lo = _low_half((1, LANES))
        m0, m1 = _half_masks()
        reps = k.shape[0] // LANES
        for pair in range(2):
            vp = v[:, pair * LANES:(pair + 1) * LANES]
            vals = (vp * m0 + m1, vp * m1 + m0)
            alphas, pvs = [], []
            for s in (0, 1):
                hd = 2 * pair + s
                q = q_ref[0, :, hd * LANES:(hd + 1) * LANES]
                sc = _dot_nt(q, k[:, hd * LANES:(hd + 1) * LANES])
                m_prev = m_ref[hd]
                m_new = jnp.maximum(m_prev, jnp.max(sc, axis=-1, keepdims=True))
                m_ref[hd] = m_new
                alphas.append(jnp.exp(m_prev - m_new))
                p = jnp.exp(sc - jnp.tile(m_new, (1, reps)))
                pvs.append(_dot(p.astype(BF16), vals[s]))
            acc_ref[pair] = (acc_ref[pair] * jnp.where(lo, alphas[0], alphas[1])
                             + jnp.where(lo, pvs[0], pvs[1]))
            l_ref[pair] = (l_ref[pair] * jnp.where(lo, alphas[1], alphas[0])
                           + jnp.where(lo, pvs[1], pvs[0]))

    @pl.when(j == 0)
    def _():
        update(kc_ref[0], vc_ref[0])

    @pl.when(j > 0)
    def _():
        update(kl_ref[0], vl_ref[0])

    @pl.when(j == pl.num_programs(2) - 1)
    def _():
        for pair in range(2):
            inv = pltpu.roll(1.0 / l_ref[pair], HEAD_DIM, 1)
            o_ref[0, :, pair * LANES:(pair + 1) * LANES] = (acc_ref[pair] * inv).astype(BF16)


def _mla_attention(q, kc, vc, kl, vl):
    b, n, _ = q.shape
    c = kc.shape[1]
    tq = min(512, n)
    tk = min(512, n)
    est = 2 * (tq * 512 * 2 + c * 768 * 2 + tk * 768 * 2 + tq * 256 * 2) + 6 * tq * LANES * 4 + 10 * tq * tk * 4 + 8 * MIB
    return pl.pallas_call(
        _mla_kernel,
        out_shape=jax.ShapeDtypeStruct((b, n, 256), BF16),
        grid=(b, n // tq, 1 + n // tk),
        in_specs=[pl.BlockSpec((1, tq, 512), lambda bi, i, j: (bi, i, 0)),
                  pl.BlockSpec((1, c, 512), lambda bi, i, j: (bi, 0, 0)),
                  pl.BlockSpec((1, c, 256), lambda bi, i, j: (bi, 0, 0)),
                  pl.BlockSpec((1, tk, 512), lambda bi, i, j: (bi, jnp.maximum(j - 1, 0), 0)),
                  pl.BlockSpec((1, tk, 256), lambda bi, i, j: (bi, jnp.maximum(j - 1, 0), 0))],
        out_specs=pl.BlockSpec((1, tq, 256), lambda bi, i, j: (bi, i, 0)),
        scratch_shapes=[pltpu.VMEM((4, tq, LANES), F32), pltpu.VMEM((2, tq, LANES), F32), pltpu.VMEM((2, tq, LANES), F32)],
        compiler_params=_cparams(("arbitrary", "arbitrary", "arbitrary"), est),
    )(q, kc, vc, kl, vl)


def _swa_kernel(q_ref, kp_ref, kcur_ref, kn_ref, vp_ref, vcur_ref, vn_ref, kctx_ref, vctx_ref, sink_ref, o_ref):
    i = pl.program_id(1)
    last = pl.num_programs(1) - 1
    tq = q_ref.shape[1]

    def band_bias(width, offset, edge_penalty):
        r = lax.broadcasted_iota(jnp.int32, (tq, width), 0)
        cidx = lax.broadcasted_iota(jnp.int32, (tq, width), 1)
        rel = cidx + offset - r
        inside = jnp.where(rel >= -SWA_WINDOW, jnp.where(rel <= SWA_WINDOW, 1, 0), 0)
        return jnp.where(inside == 1, edge_penalty, NEG_INF)

    halo = kp_ref.shape[1]
    b_prev = band_bias(halo, -halo, jnp.where(i > 0, 0.0, NEG_INF))
    b_cur = band_bias(tq, 0, 0.0)
    b_next = band_bias(halo, tq, jnp.where(i < last, 0.0, NEG_INF))
    m0, m1 = _half_masks()
    kctx, vctx = kctx_ref[0], vctx_ref[0]
    pieces = [((kp_ref[0],) * 2, vp_ref[0], (b_prev,) * 2),
              ((kcur_ref[0],) * 2, vcur_ref[0], (b_cur,) * 2),
              ((kn_ref[0],) * 2, vn_ref[0], (b_next,) * 2),
              ((kctx,) * 2, vctx, (None, None))]
    for grp in range(2):
        qg = q_ref[0, :, grp * LANES:(grp + 1) * LANES]
        sinks = [sink_ref[2 * grp + s:2 * grp + s + 1, 0:1] for s in (0, 1)]
        o_ref[0, :, grp * LANES:(grp + 1) * LANES] = _attend_pair((qg * m0, qg * m1), pieces, sinks).astype(BF16)


def _swa_attention(q, k, v, kc, vc, sink_rows):
    b, n, _ = q.shape
    c = kc.shape[1]
    tq = 256
    halo = SWA_WINDOW
    r = tq // halo
    nh = n // halo
    cur = lambda bi, i: (bi, i, 0)
    prev = lambda bi, i: (bi, jnp.maximum(i * r - 1, 0), 0)
    nxt = lambda bi, i: (bi, jnp.minimum(i * r + r, nh - 1), 0)
    ctx = lambda bi, i: (bi, 0, 0)
    est = 16 * tq * (tq + 2 * halo + c) * 4 + 8 * MIB
    return pl.pallas_call(
        _swa_kernel,
        out_shape=jax.ShapeDtypeStruct((b, n, 256), BF16),
        grid=(b, n // tq),
        in_specs=[pl.BlockSpec((1, tq, 256), cur),
                  pl.BlockSpec((1, halo, LANES), prev), pl.BlockSpec((1, tq, LANES), cur), pl.BlockSpec((1, halo, LANES), nxt),
                  pl.BlockSpec((1, halo, LANES), prev), pl.BlockSpec((1, tq, LANES), cur), pl.BlockSpec((1, halo, LANES), nxt),
                  pl.BlockSpec((1, c, LANES), ctx), pl.BlockSpec((1, c, LANES), ctx),
                  pl.BlockSpec((4, LANES), lambda bi, i: (0, 0))],
        out_specs=pl.BlockSpec((1, tq, 256), cur),
        compiler_params=_cparams(("arbitrary", "arbitrary"), est),
    )(q, k, k, k, v, v, v, kc, vc, sink_rows)


def _na_kernel(q_ref, kp_ref, kcur_ref, kn_ref, vp_ref, vcur_ref, vn_ref, kctx_ref, vctx_ref, bias_ref, o_ref):
    tq = q_ref.shape[1]
    m0, m1 = _half_masks()
    for pair in range(2):
        sl = slice(pair * LANES, (pair + 1) * LANES)
        qp = q_ref[0, :, sl]
        pieces = []
        for idx, (kr, vr) in enumerate(((kp_ref, vp_ref), (kcur_ref, vcur_ref), (kn_ref, vn_ref))):
            biases = tuple(bias_ref[0, 2 * pair + s, :, idx * tq:(idx + 1) * tq] for s in (0, 1))
            pieces.append(((kr[0, :, sl],) * 2, vr[0, :, sl], biases))
        pieces.append(((kctx_ref[0, :, sl],) * 2, vctx_ref[0, :, sl], (None, None)))
        o_ref[0, :, sl] = _attend_pair((qp * m0, qp * m1), pieces, None).astype(BF16)


def _na_bias_tables(rpb, rows):
    nb = rows // NA_ROWS
    heads = rpb.shape[0]
    width = 2 * NA_KW - 1
    span = 2 * GRID_W - 1
    lead = GRID_W - 2 - (NA_KW - 1)
    rpb = rpb.astype(F32)
    per_row = jnp.stack([rpb[:, NA_ROWS - 1 - a:NA_ROWS - 1 - a + 3 * NA_ROWS, :] for a in range(NA_ROWS)], axis=1)
    padded = jnp.pad(per_row, ((0, 0), (0, 0), (0, 0), (lead, span - width - lead)))
    tiled = jnp.broadcast_to(padded[:, :, :, None, :], (heads, NA_ROWS, 3 * NA_ROWS, GRID_W, span))
    skew = tiled.reshape(heads, NA_ROWS, 3 * NA_ROWS, GRID_W * span)[..., :GRID_W * (span - 1)]
    toep = skew.reshape(heads, NA_ROWS, 3 * NA_ROWS, GRID_W, span - 1)[..., GRID_W - 2:]
    table = jnp.transpose(toep, (0, 1, 3, 2, 4)).reshape(heads, NA_ROWS * GRID_W, 3 * NA_ROWS * GRID_W)
    a = np.arange(NA_ROWS)[:, None, None, None]
    cq = np.arange(GRID_W)[None, :, None, None]
    kr = np.arange(3 * NA_ROWS)[None, None, :, None]
    ck = np.arange(GRID_W)[None, None, None, :]
    valids = []
    for j in (0, 1, nb - 1):
        r = NA_ROWS * j + a
        rs = np.clip(r - NA_KH // 2, 0, rows - NA_KH)
        rk = NA_ROWS * (j - 1) + kr
        cs = np.clip(cq - NA_KW // 2, 0, GRID_W - NA_KW)
        valid = (rk >= rs) & (rk < rs + NA_KH) & (ck >= cs) & (ck < cs + NA_KW)
        valids.append(valid.reshape(NA_ROWS * GRID_W, 3 * NA_ROWS * GRID_W))
    valid = jnp.asarray(np.stack(valids))
    return jnp.where(valid[:, None], table[None], NEG_INF)


def _na_attention(q, k, v, kc, vc, bias):
    b, n, _ = q.shape
    c = kc.shape[1]
    tq = NA_ROWS * GRID_W
    nb = n // tq
    cur = lambda bi, i: (bi, i, 0)
    prev = lambda bi, i: (bi, jnp.maximum(i - 1, 0), 0)
    nxt = lambda bi, i: (bi, jnp.minimum(i + 1, nb - 1), 0)
    ctx = lambda bi, i: (bi, 0, 0)
    variant = lambda bi, i: (jnp.where(i == 0, 0, jnp.where(i == nb - 1, 2, 1)), 0, 0, 0)
    blk = pl.BlockSpec((1, tq, 256), cur)
    est = 2 * 4 * tq * 3 * tq * 4 + 16 * tq * (3 * tq + c) * 4 + 8 * MIB
    return pl.pallas_call(
        _na_kernel,
        out_shape=jax.ShapeDtypeStruct((b, n, 256), BF16),
        grid=(b, nb),
        in_specs=[blk, pl.BlockSpec((1, tq, 256), prev), blk, pl.BlockSpec((1, tq, 256), nxt),
                  pl.BlockSpec((1, tq, 256), prev), blk, pl.BlockSpec((1, tq, 256), nxt),
                  pl.BlockSpec((1, c, 256), ctx), pl.BlockSpec((1, c, 256), ctx),
                  pl.BlockSpec((1, 4, tq, 3 * tq), variant)],
        out_specs=blk,
        compiler_params=_cparams(("arbitrary", "arbitrary"), est),
    )(q, k, k, k, v, v, v, kc, vc, bias)


def _ctx_attn_kernel(mq_ref, mk_ref, mv_ref, sq_ref, sk_ref, sv_ref, sink_ref, nq_ref, nk_ref, nv_ref,
                     ya_ref, yb_ref, yc_ref):
    m0, m1 = _half_masks()
    none2 = (None, None)
    for pair in range(2):
        sl = slice(pair * LANES, (pair + 1) * LANES)
        h0 = slice(2 * pair * LANES, (2 * pair + 1) * LANES)
        h1 = slice((2 * pair + 1) * LANES, (2 * pair + 2) * LANES)
        ya_ref[0, :, sl] = _attend_pair((mq_ref[0, :, h0], mq_ref[0, :, h1]),
                                        [((mk_ref[0, :, h0], mk_ref[0, :, h1]), mv_ref[0, :, sl], none2)],
                                        None).astype(BF16)
        qg = sq_ref[0, :, sl]
        sinks = [sink_ref[2 * pair + s:2 * pair + s + 1, 0:1] for s in (0, 1)]
        yb_ref[0, :, sl] = _attend_pair((qg * m0, qg * m1), [((sk_ref[0],) * 2, sv_ref[0], none2)],
                                        sinks).astype(BF16)
        qn = nq_ref[0, :, sl]
        yc_ref[0, :, sl] = _attend_pair((qn * m0, qn * m1), [((nk_ref[0, :, sl],) * 2, nv_ref[0, :, sl], none2)],
                                        None).astype(BF16)


def _ctx_attention(mq, mk, mv, sq, sk, sv, sink_rows, nq, nk, nv):
    b, c, _ = mq.shape
    spec = lambda wd: pl.BlockSpec((1, c, wd), lambda bi: (bi, 0, 0))
    return pl.pallas_call(
        _ctx_attn_kernel,
        out_shape=[jax.ShapeDtypeStruct((b, c, 256), BF16)] * 3,
        grid=(b,),
        in_specs=[spec(512), spec(512), spec(256), spec(256), spec(LANES), spec(LANES),
                  pl.BlockSpec((4, LANES), lambda bi: (0, 0)), spec(256), spec(256), spec(256)],
        out_specs=[spec(256)] * 3,
        compiler_params=_cparams(("arbitrary",), 24 * MIB),
    )(mq, mk, mv, sq, sk, sv, sink_rows, nq, nk, nv)


def _ret_chunk(q, k, v, g, dec_lane, gn_lane, s_scr, reverse):
    c = RET_CHUNK
    lo = _low_half((1, LANES))
    lg = -(jnp.maximum(-dec_lane, 0.0) + jnp.log1p(jnp.exp(-jnp.abs(dec_lane))))
    ti = lax.broadcasted_iota(jnp.int32, (c, LANES), 0).astype(F32)
    if reverse:
        qpow, kpow = c - ti, ti
    else:
        qpow, kpow = ti + 1.0, c - 1.0 - ti
    qdec = jnp.exp(lg * qpow)
    kdec = jnp.exp(lg * kpow)
    cdec = jnp.exp(lg * float(c))
    ii = lax.broadcasted_iota(jnp.int32, (c, c), 0)
    jj = lax.broadcasted_iota(jnp.int32, (c, c), 1)
    dist = (jj - ii) if reverse else (ii - jj)
    distf = jnp.maximum(dist, 0).astype(F32)
    m0, m1 = _half_masks()
    outs = []
    for s, msk in ((0, m0), (1, m1)):
        lg_s = lg[:, s * HEAD_DIM:s * HEAD_DIM + 1]
        decay = jnp.where(dist >= 0, jnp.exp(lg_s * distf), 0.0)
        inner = _dot_nt(q * msk, k) * decay
        outs.append(_dot(inner.astype(BF16), v))
    state = s_scr[...]
    o = jnp.where(lo, outs[0], outs[1]) + _dot((q.astype(F32) * qdec).astype(BF16), state.astype(BF16))
    kd = (k.astype(F32) * kdec).astype(BF16)
    same_head = (ii < HEAD_DIM) == (jj < HEAD_DIM)
    s_scr[...] = state * cdec + jnp.where(same_head, _dot_tn(kd, v), 0.0)

    def head_mean(x):
        s_lo = jnp.sum(jnp.where(lo, x, 0.0), axis=-1, keepdims=True)
        s_hi = jnp.sum(jnp.where(lo, 0.0, x), axis=-1, keepdims=True)
        return jnp.where(lo, s_lo, s_hi) * (1.0 / HEAD_DIM)

    dev = o - head_mean(o)
    normed = dev * lax.rsqrt(head_mean(dev * dev) + EPS)
    gf = g.astype(F32)
    return (gf * jax.nn.sigmoid(gf)) * (normed * gn_lane)


def _ret_kernel(decf_ref, decb_ref, gnf_ref, gnb_ref, s0f_ref, s0b_ref,
                qf_ref, kf_ref, vf_ref, gf_ref, qb_ref, kb_ref, vb_ref, gb_ref,
                yf_ref, yb_ref, sf_ref, sb_ref, sf_scr, sb_scr):
    t = pl.program_id(2)

    @pl.when(t == 0)
    def _():
        sf_scr[...] = s0f_ref[0, 0]
        sb_scr[...] = s0b_ref[0, 0]

    yf_ref[0] = _ret_chunk(qf_ref[0], kf_ref[0], vf_ref[0], gf_ref[0], decf_ref[0], gnf_ref[...], sf_scr,
                           False).astype(BF16)
    yb_ref[0] = _ret_chunk(qb_ref[0], kb_ref[0], vb_ref[0], gb_ref[0], decb_ref[0], gnb_ref[...], sb_scr,
                           True).astype(BF16)

    @pl.when(t == pl.num_programs(2) - 1)
    def _():
        sf_ref[0, 0] = sf_scr[...]
        sb_ref[0, 0] = sb_scr[...]


def _retention(q, k, v, gf, gb, dec_f, dec_b, gn_f, gn_b, s0f, s0b):
    b, t, _ = q.shape
    c = RET_CHUNK
    nt = t // c
    fwd = pl.BlockSpec((1, c, LANES), lambda bi, p, i: (bi, i, p))
    bwd = pl.BlockSpec((1, c, LANES), lambda bi, p, i: (bi, nt - 1 - i, p))
    dec = pl.BlockSpec((1, 1, LANES), lambda bi, p, i: (p, 0, 0))
    gn = pl.BlockSpec((1, LANES), lambda bi, p, i: (0, p))
    st = pl.BlockSpec((1, 1, LANES, LANES), lambda bi, p, i: (bi, p, 0, 0))
    return pl.pallas_call(
        _ret_kernel,
        out_shape=[jax.ShapeDtypeStruct((b, t, 256), BF16)] * 2 + [jax.ShapeDtypeStruct((b, 2, LANES, LANES), F32)] * 2,
        grid=(b, 2, nt),
        in_specs=[dec, dec, gn, gn, st, st, fwd, fwd, fwd, fwd, bwd, bwd, bwd, bwd],
        out_specs=[fwd, bwd, st, st],
        scratch_shapes=[pltpu.VMEM((LANES, LANES), F32)] * 2,
        compiler_params=_cparams(("arbitrary", "arbitrary", "arbitrary"), 16 * MIB),
    )(dec_f, dec_b, gn_f, gn_b, s0f, s0b, q, k, v, gf, q, k, v, gb)


def _outproj_kernel(x_ref, ya_ref, yb_ref, yc_ref, ydf_ref, ydb_ref, beta_ref, w_ref, gate_ref, o_ref):
    ys = (ya_ref[0].astype(F32), yb_ref[0].astype(F32), yc_ref[0].astype(F32),
          ydf_ref[0].astype(F32) + ydb_ref[0].astype(F32))
    acc = 0.0
    for s, y in enumerate(ys):
        sl = slice(s * 256, (s + 1) * 256)
        acc = acc + _dot((y * beta_ref[:, sl]).astype(BF16), w_ref[sl, :])
    o_ref[0] = x_ref[0] + gate_ref[0] * acc


def _out_projection(x, ya, yb, yc, ydf, ydb, beta, w, gate):
    b, t, d = x.shape
    tm = min(512, t)
    row = lambda wd: pl.BlockSpec((1, tm, wd), lambda bi, i: (bi, i, 0))
    est = 2 * (2 * tm * d * 4 + 5 * tm * 256 * 2 + d * d * 2) + 8 * MIB
    return pl.pallas_call(
        _outproj_kernel,
        out_shape=jax.ShapeDtypeStruct((b, t, d), F32),
        grid=(b, t // tm),
        in_specs=[row(d), row(256), row(256), row(256), row(256), row(256),
                  pl.BlockSpec((1, d), lambda bi, i: (0, 0)), pl.BlockSpec((d, d), lambda bi, i: (0, 0)),
                  pl.BlockSpec((1, 1, d), lambda bi, i: (bi, 0, 0))],
        out_specs=row(d),
        compiler_params=_cparams(("arbitrary", "arbitrary"), est),
    )(x, ya, yb, yc, ydf, ydb, beta, w, gate)


def _top_values(work, out_ref, count):
    for r in range(count):
        m = jnp.max(work, axis=0, keepdims=True)
        out_ref[r:r + 1, :] = m
        work = jnp.where(work == m, -jnp.inf, work)


def _staircase(w1, w2):
    row8 = lax.broadcasted_iota(jnp.int32, (8, w1.shape[1]), 0)
    cands = [w1[0:1] + w2]
    for a in range(1, 8):
        cands.append(jnp.where(row8 < PEER_TOPK // (a + 1), w1[a:a + 1] + w2[0:8], -jnp.inf))
    cands.append(w1[8:16] + w2[0:1])
    return jnp.concatenate(cands, axis=0)


def _peer_route(hd, q_scr, k1_ref, k2_ref, s2_scr, e2_scr, tau_scr, e1_scr, v1_scr, v2_scr, top_scr):
    half = PEER_NKEYS
    base = pl.multiple_of(hd * 2 * half, 2 * half)
    s1 = _dot(k1_ref[...], q_scr[pl.ds(base, half), :].astype(BF16))
    s2 = _dot(k2_ref[...], q_scr[pl.ds(base + half, half), :].astype(BF16))
    _top_values(s1, v1_scr, PEER_TOPK)
    _top_values(s2, v2_scr, PEER_TOPK)
    v1, v2 = v1_scr[...], v2_scr[...]
    cand = _staircase(v1, v2)
    _top_values(cand, top_scr, PEER_TOPK)
    top = top_scr[...]
    z = jnp.sum(jnp.exp(top - top[0:1]), axis=0, keepdims=True)
    sel = cand >= top[PEER_TOPK - 1:PEER_TOPK]
    row_min = lambda picked: jnp.min(picked, axis=0, keepdims=True)
    taus = [row_min(jnp.where(sel[0:16], v2, jnp.inf))]
    for a in range(1, 8):
        taus.append(row_min(jnp.where(sel[8 + 8 * a:16 + 8 * a], v2[0:8], jnp.inf)))
    tail = jnp.where(sel[72:80], v2[0:1], jnp.inf)
    taus += [tail[a:a + 1] for a in range(8)]
    tau = jnp.full(s1.shape, jnp.inf, F32)
    for a in range(PEER_TOPK):
        tau = jnp.minimum(tau, jnp.where(s1 >= v1[a:a + 1], taus[a], jnp.inf))
    s2_scr[hd] = s2
    tau_scr[hd] = tau
    e1_scr[hd] = jnp.exp(s1 - v1[0:1]) / z
    e2_scr[hd] = jnp.exp(s2 - v2[0:1]) * 0.5


def _peer_kernel(*refs, final_norm):
    if final_norm:
        (x_ref, sh_ref, sc_ref, gt_ref, g_ref, wqt_ref, k1_ref, k2_ref, u_ref, vt_ref, fg_ref, o_ref,
         ht_scr, q_scr, s2_scr, e2_scr, tau_scr, e1_scr, v1_scr, v2_scr, top_scr,
         a0_scr, a1_scr, acc_scr) = refs
    else:
        (x_ref, sh_ref, sc_ref, gt_ref, g_ref, wqt_ref, k1_ref, k2_ref, u_ref, vt_ref, o_ref,
         ht_scr, q_scr, s2_scr, e2_scr, tau_scr, e1_scr, v1_scr, v2_scr, top_scr,
         a0_scr, a1_scr, acc_scr) = refs
        fg_ref = None
    step = pl.program_id(2)
    tokens = ht_scr.shape[1]

    @pl.when(step == 0)
    def _():
        h = _norm_mod(x_ref[0], g_ref[...], sh_ref[0], sc_ref[0])
        ht = jnp.transpose(h).astype(BF16)
        ht_scr[...] = ht
        q_scr[...] = _dot(wqt_ref[...], ht)
        acc_scr[...] = jnp.zeros(acc_scr.shape, F32)
        a1_scr[...] = jnp.zeros(a1_scr.shape, F32)

        def route(hd, carry):
            _peer_route(hd, q_scr, k1_ref, k2_ref, s2_scr, e2_scr, tau_scr, e1_scr, v1_scr, v2_scr, top_scr)
            return carry

        lax.fori_loop(0, PEER_HEADS, route, 0)

    def streams(a_write, a_read):
        a_write[...] = _dot(u_ref[...], ht_scr[...])
        y = None
        piece = 2 * PEER_NKEYS
        for kt in range(PEER_ROWS // 2):
            ws = []
            for r in (2 * kt, 2 * kt + 1):
                rows = slice(r * PEER_NKEYS, (r + 1) * PEER_NKEYS)
                i = jnp.clip((step - 1) * PEER_ROWS + r, 0, PEER_NKEYS - 1)
                gate = jnp.zeros((PEER_NKEYS, tokens), F32)
                for hd in range(PEER_HEADS):
                    val = e2_scr[hd] * e1_scr[hd, pl.ds(i, 1), :]
                    gate = gate + jnp.where(s2_scr[hd] >= tau_scr[hd, pl.ds(i, 1), :], val, 0.0)
                a = a_read[rows, :]
                gelu2 = a * (1.0 + lax.erf(a * np.float32(np.sqrt(0.5))))
                ws.append((gelu2 * gate).astype(BF16))
            part = _dot(vt_ref[:, kt * piece:(kt + 1) * piece], jnp.concatenate(ws, axis=0))
            y = part if y is None else y + part
        acc_scr[...] += y

    @pl.when(step % 2 == 0)
    def _():
        streams(a0_scr, a1_scr)

    @pl.when(step % 2 == 1)
    def _():
        streams(a1_scr, a0_scr)

    @pl.when(step == pl.num_programs(2) - 1)
    def _():
        out = x_ref[0] + gt_ref[0] * jnp.transpose(acc_scr[...])
        if final_norm:
            out = _rmsnorm(out, fg_ref[...])
        o_ref[0] = out


def _peer_ffn(x, shift, scale, gate, g, wqt, k1, k2, u, vt, final_g=None):
    b, t, d = x.shape
    tt = PEER_TOKENS
    ec = PEER_ROWS * PEER_NKEYS
    n_blocks = u.shape[0] // ec
    nq = wqt.shape[0]
    final_norm = final_g is not None
    tok = pl.BlockSpec((1, tt, d), lambda bi, i, s: (bi, i, 0))
    vec = pl.BlockSpec((1, 1, d), lambda bi, i, s: (bi, 0, 0))
    const = lambda shape: pl.BlockSpec(shape, lambda bi, i, s: (0,) * len(shape))
    in_specs = [tok, vec, vec, vec, const((1, d)), const((nq, d)), const(k1.shape), const(k2.shape),
                pl.BlockSpec((ec, d), lambda bi, i, s: (jnp.minimum(s, n_blocks - 1), 0)),
                pl.BlockSpec((d, ec), lambda bi, i, s: (0, jnp.clip(s - 1, 0, n_blocks - 1)))]
    args = [x, shift, scale, gate, g, wqt, k1, k2, u, vt]
    if final_norm:
        in_specs.append(const((1, d)))
        args.append(final_g)
    table = pltpu.VMEM((PEER_HEADS, PEER_NKEYS, tt), F32)
    top = pltpu.VMEM((PEER_TOPK, tt), F32)
    a_buf = pltpu.VMEM((ec, tt), F32)
    scratch = [pltpu.VMEM((d, tt), BF16), pltpu.VMEM((nq, tt), F32), table, table, table, table,
               top, top, top, a_buf, a_buf, pltpu.VMEM((d, tt), F32)]
    est = (4 * tt * d * 4 + 2 * nq * d * 2 + 4 * ec * d * 2 + tt * d * 2 + nq * tt * 4
           + 4 * PEER_HEADS * PEER_NKEYS * tt * 4 + d * tt * 4 + 2 * ec * tt * 6 + 8 * MIB)
    return pl.pallas_call(
        functools.partial(_peer_kernel, final_norm=final_norm),
        out_shape=jax.ShapeDtypeStruct((b, t, d), F32),
        grid=(b, t // tt, n_blocks + 1),
        in_specs=in_specs,
        out_specs=tok,
        scratch_shapes=scratch,
        compiler_params=_cparams(("arbitrary", "arbitrary", "arbitrary"), est),
    )(*args)


def _prep_in_weights(w_in):
    d = w_in.shape[0]
    zeros = lambda n: jnp.zeros((d, n), w_in.dtype)
    cols = []
    for hd in range(4):
        cols += [w_in[:, hd * MLA_QK:(hd + 1) * MLA_QK], zeros(LANES - MLA_QK)]
    cols.append(w_in[:, 384:512])
    cols += [zeros(HEAD_DIM), w_in[:, 512:544], zeros(LANES - HEAD_DIM - MLA_ROPE)]
    for hd in (0, 2, 1, 3):
        cols.append(w_in[:, 544 + hd * HEAD_DIM:544 + (hd + 1) * HEAD_DIM])
    cols.append(w_in[:, 800:])
    return jnp.concatenate(cols, axis=1).astype(BF16)


def _prep_kv_weights(w_ukv):
    zeros = jnp.zeros((MLA_RANK, HEAD_DIM), w_ukv.dtype)
    wkn = jnp.concatenate([blk for hd in range(4) for blk in (w_ukv[:, hd * LANES:hd * LANES + HEAD_DIM], zeros)], axis=1)
    wv = jnp.concatenate([w_ukv[:, hd * LANES + HEAD_DIM:(hd + 1) * LANES] for hd in range(4)], axis=1)
    sel = np.zeros((LANES, 4 * LANES), np.float32)
    for hd in range(4):
        for r in range(MLA_ROPE):
            sel[HEAD_DIM + r, hd * LANES + HEAD_DIM + r] = 1.0
    return wkn.astype(BF16), jnp.asarray(sel, BF16), wv.astype(BF16)


def _rope_lane_tables(n):
    def tables(rot_dim):
        quarter = rot_dim // 4
        inv = ROPE_BASE ** (-jnp.arange(quarter, dtype=F32) / quarter)
        t = jnp.arange(n, dtype=jnp.int32)
        pos = jnp.stack([t // GRID_W, t % GRID_W], axis=-1).astype(F32)
        ang = pos[:, :, None] * inv
        lanes = lambda tb: jnp.concatenate([tb[:, 0], tb[:, 0], tb[:, 1], tb[:, 1]], axis=-1)
        return lanes(jnp.cos(ang)), lanes(jnp.sin(ang))

    c32, s32 = tables(MLA_ROPE)
    c64, s64 = tables(HEAD_DIM)
    ones = lambda w: jnp.ones((n, w), F32)
    zeros = lambda w: jnp.zeros((n, w), F32)
    return (jnp.concatenate([ones(HEAD_DIM), c32, ones(LANES - HEAD_DIM - MLA_ROPE)], axis=-1),
            jnp.concatenate([zeros(HEAD_DIM), s32, zeros(LANES - HEAD_DIM - MLA_ROPE)], axis=-1),
            jnp.concatenate([c64, c64], axis=-1), jnp.concatenate([s64, s64], axis=-1))


_SWA_SLOT_HEADS = (0, 2, 1, 3)


def kernel(x, c, ctx, c_ctx, ada_w, ada_b, norm_mix_g, w_in, mla_kv_norm_g, mla_w_ukv, swa_sink, na_rpb, ret_decay_f, ret_decay_b, ret_gn_f, ret_gn_b, mix_beta, w_out, norm_ffn_g, peer_wq, peer_k1, peer_k2, peer_u, peer_v, final_norm_g):
    b, n, d = x.shape
    n_ctx = ctx.shape[1]
    depth = ada_w.shape[0]
    rows = n // GRID_W

    pad_rows = -(b + 1) % 8
    cc = jnp.concatenate([c, c_ctx[None, :], jnp.zeros((pad_rows, d), F32)], axis=0)
    mod = _ada_modulation(cc, ada_w, ada_b)

    lat_tabs = _rope_lane_tables(n)
    ctx_tabs = (jnp.ones((n_ctx, LANES), F32), jnp.zeros((n_ctx, LANES), F32)) * 2
    slot_heads = np.asarray(_SWA_SLOT_HEADS)
    out_perm = np.arange(d)
    out_perm[256:512] = 256 + (slot_heads[:, None] * HEAD_DIM + np.arange(HEAD_DIM)[None, :]).reshape(-1)
    zero_state = jnp.zeros((b, 2, LANES, LANES), F32)
    pair_lanes = lambda p: jnp.repeat(p.astype(F32), HEAD_DIM).reshape(2, 1, LANES)

    xc = ctx
    for layer in range(depth):
        with_ctx = layer < depth - 1
        last = layer == depth - 1
        chunk = lambda k, lo, hi: mod[layer, lo:hi, k * d:(k + 1) * d][:, None, :]
        mod_l = [chunk(k, 0, b) for k in range(6)]
        mod_c = [jnp.broadcast_to(chunk(k, b, b + 1), (b, 1, d)) for k in range(6)]

        w_cols = _prep_in_weights(w_in[layer])
        wkn, emat, wv = _prep_kv_weights(mla_w_ukv[layer])
        g_mix = norm_mix_g[layer][None, :]
        kvg = mla_kv_norm_g[layer][None, :]
        p_l = _in_projection(x, mod_l[0], mod_l[1], g_mix, w_cols, kvg, wkn, emat, wv, lat_tabs)
        p_c = _in_projection(xc, mod_c[0], mod_c[1], g_mix, w_cols, kvg, wkn, emat, wv, ctx_tabs)
        mq, mk, mv, sq, sk, sv, nq, nk, nv, rq, rk, rv, gf, gb = p_l
        cmq, cmk, cmv, csq, csk, csv, cnq, cnk, cnv, crq, crk, crv, cgf, cgb = p_c

        sink_rows = jnp.broadcast_to(swa_sink[layer].astype(F32)[slot_heads][:, None], (4, LANES))
        bias = _na_bias_tables(na_rpb[layer], rows)
        dec_f, dec_b = pair_lanes(ret_decay_f[layer]), pair_lanes(ret_decay_b[layer])
        gn_f, gn_b = ret_gn_f[layer][None, :], ret_gn_b[layer][None, :]

        ya = _mla_attention(mq, cmk, cmv, mk, mv)
        yb = _swa_attention(sq, sk, sv, csk, csv, sink_rows)
        yc = _na_attention(nq, nk, nv, cnk, cnv, bias)
        cyf, cyb, s_f, s_b = _retention(crq, crk, crv, cgf, cgb, dec_f, dec_b, gn_f, gn_b, zero_state, zero_state)
        ydf, ydb, _, _ = _retention(rq, rk, rv, gf, gb, dec_f, dec_b, gn_f, gn_b, s_f, s_b)

        beta = mix_beta[layer][out_perm][None, :]
        w_o = w_out[layer][out_perm, :].astype(BF16)
        x = _out_projection(x, ya, yb, yc, ydf, ydb, beta, w_o, mod_l[2])

        g_ffn = norm_ffn_g[layer][None, :]
        wqt = peer_wq[layer].T.astype(BF16)
        k1, k2 = peer_k1[layer].astype(BF16), peer_k2[layer].astype(BF16)
        u, vt = peer_u[layer].astype(BF16), peer_v[layer].T.astype(BF16)
        x = _peer_ffn(x, mod_l[3], mod_l[4], mod_l[5], g_ffn, wqt, k1, k2, u, vt,
                      final_norm_g[None, :] if last else None)
        if with_ctx:
            cya, cyb_, cyc = _ctx_attention(cmq, cmk, cmv, csq, csk, csv, sink_rows, cnq, cnk, cnv)
            xc = _out_projection(xc, cya, cyb_, cyc, cyf, cyb, beta, w_o, mod_c[2])
            xc = _peer_ffn(xc, mod_c[3], mod_c[4], mod_c[5], g_ffn, wqt, k1, k2, u, vt)
    return x
```

```python
import functools

import numpy as np
import jax
import jax.numpy as jnp
from jax import lax
from jax.experimental import pallas as pl
from jax.experimental.pallas import tpu as pltpu

F32 = jnp.float32
BF16 = jnp.bfloat16

EPS = 1e-6
NEG_INF = -1e30
ROPE_BASE = 10000.0
GRID_W = 64
HEAD_DIM = 64
LANES = 128
V7X_VMEM_BYTES = 64 * 1024 * 1024
MIB = 1024 * 1024

MLA_ROPE = 32
MLA_QK = 96
MLA_RANK = 128
SWA_WINDOW = 128
NA_KH, NA_KW = 8, 16
NA_ROWS = 4
RET_CHUNK = 128
PEER_HEADS = 8
PEER_NKEYS = 128
PEER_TOPK = 16
PEER_TOKENS = 256
PEER_ROWS = 16

_MQ, _CKV, _KR, _SQ, _SK, _SV, _NQ, _NK, _NV, _RQ, _RK, _RV, _GF, _GB, _IN_COLS = (
    0, 512, 640, 768, 1024, 1152, 1280, 1536, 1792, 2048, 2304, 2560, 2816, 3072, 3328)


def _cparams(semantics, vmem_bytes):
    limit = int(min(vmem_bytes, V7X_VMEM_BYTES * 7 // 8))
    return pltpu.CompilerParams(dimension_semantics=semantics, vmem_limit_bytes=limit)


def _dot(a, b):
    return jnp.dot(a, b, preferred_element_type=F32)


def _dot_nt(a, b):
    return lax.dot_general(a, b, (((1,), (1,)), ((), ())), preferred_element_type=F32)


def _dot_tn(a, b):
    return lax.dot_general(a, b, (((0,), (0,)), ((), ())), preferred_element_type=F32)


def _low_half(shape):
    return lax.broadcasted_iota(jnp.int32, shape, len(shape) - 1) < HEAD_DIM


def _half_masks():
    lo = _low_half((1, LANES))
    return (jnp.where(lo, 1.0, 0.0).astype(BF16), jnp.where(lo, 0.0, 1.0).astype(BF16))


def _norm_mod(x, g, shift, scale):
    ms = jnp.mean(x * x, axis=-1, keepdims=True)
    return (x * lax.rsqrt(ms + EPS) * g) * (1.0 + scale) + shift


def _rmsnorm(x, g):
    ms = jnp.mean(x * x, axis=-1, keepdims=True)
    return x * lax.rsqrt(ms + EPS) * g


def _ada_kernel(c_ref, w_ref, b_ref, o_ref):
    c = c_ref[...]
    o_ref[0] = _dot(c * jax.nn.sigmoid(c), w_ref[0]) + b_ref[0]


def _ada_modulation(cc, ada_w, ada_b):
    depth, d, width = ada_w.shape
    rows = cc.shape[0]
    tn = 1024
    return pl.pallas_call(
        _ada_kernel,
        out_shape=jax.ShapeDtypeStruct((depth, rows, width), F32),
        grid=(depth, width // tn),
        in_specs=[pl.BlockSpec((rows, d), lambda l, j: (0, 0)),
                  pl.BlockSpec((1, d, tn), lambda l, j: (l, 0, j)),
                  pl.BlockSpec((1, 1, tn), lambda l, j: (l, 0, j))],
        out_specs=pl.BlockSpec((1, rows, tn), lambda l, j: (l, 0, j)),
        compiler_params=_cparams(("arbitrary", "arbitrary"), 2 * d * tn * 4 + 8 * MIB),
    )(cc, ada_w, ada_b.reshape(depth, 1, width))


def _rope(a, cos, sin, half):
    lane = lax.broadcasted_iota(jnp.int32, a.shape, 1)
    first = (lane % (2 * half)) < half
    rot = jnp.where(first, -pltpu.roll(a, LANES - half, 1), pltpu.roll(a, half, 1))
    return a * cos + rot * sin


def _inproj_kernel(x_ref, sh_ref, sc_ref, g_ref, w_ref, kvg_ref, wkn_ref, e_ref, wv_ref,
                   c32_ref, s32_ref, c64_ref, s64_ref,
                   mq_ref, mk_ref, mv_ref, sq_ref, sk_ref, sv_ref, nq_ref, nk_ref, nv_ref,
                   rq_ref, rk_ref, rv_ref, gf_ref, gb_ref):
    h = _norm_mod(x_ref[0], g_ref[...], sh_ref[0], sc_ref[0]).astype(BF16)

    def seg(lo, width):
        return _dot(h, w_ref[:, lo:lo + width])

    c32, s32, c64, s64 = c32_ref[...], s32_ref[...], c64_ref[...], s64_ref[...]
    mla_scale = MLA_QK ** -0.5
    head_scale = HEAD_DIM ** -0.5
    for hd in range(4):
        a = seg(_MQ + hd * LANES, LANES)
        mq_ref[0, :, hd * LANES:(hd + 1) * LANES] = (_rope(a, c32, s32, MLA_ROPE // 4) * mla_scale).astype(BF16)
    kvn = _rmsnorm(seg(_CKV, MLA_RANK), kvg_ref[...]).astype(BF16)
    kr = _rope(seg(_KR, LANES), c32, s32, MLA_ROPE // 4).astype(BF16)
    mk_ref[0] = (_dot(kvn, wkn_ref[...]) + _dot(kr, e_ref[...])).astype(BF16)
    mv_ref[0] = _dot(kvn, wv_ref[...]).astype(BF16)
    for grp in range(2):
        a = seg(_SQ + grp * LANES, LANES)
        sq_ref[0, :, grp * LANES:(grp + 1) * LANES] = (_rope(a, c64, s64, HEAD_DIM // 4) * head_scale).astype(BF16)
    sk_ref[0] = _rope(seg(_SK, LANES), c64, s64, HEAD_DIM // 4).astype(BF16)
    sv_ref[0] = seg(_SV, LANES).astype(BF16)
    nq_ref[0] = (seg(_NQ, 256) * head_scale).astype(BF16)
    nk_ref[0] = seg(_NK, 256).astype(BF16)
    nv_ref[0] = seg(_NV, 256).astype(BF16)
    rq_ref[0] = seg(_RQ, 256).astype(BF16)
    rk_ref[0] = (seg(_RK, 256) * head_scale).astype(BF16)
    rv_ref[0] = seg(_RV, 256).astype(BF16)
    gf_ref[0] = seg(_GF, 256).astype(BF16)
    gb_ref[0] = seg(_GB, 256).astype(BF16)


_INPROJ_WIDTHS = (512, 512, 256, 256, 128, 128, 256, 256, 256, 256, 256, 256, 256, 256)


def _in_projection(x, shift, scale, g, w, kvg, wkn, emat, wv, tabs):
    b, t, d = x.shape
    tm = min(512, t)
    const = lambda shape: pl.BlockSpec(shape, lambda bi, i: (0,) * len(shape))
    tab = pl.BlockSpec((tm, LANES), lambda bi, i: (i, 0))
    vec = pl.BlockSpec((1, 1, d), lambda bi, i: (bi, 0, 0))
    est = 2 * (tm * d * 4 + d * _IN_COLS * 2 + tm * sum(_INPROJ_WIDTHS) * 2 + 4 * tm * LANES * 4) + 12 * MIB
    return pl.pallas_call(
        _inproj_kernel,
        out_shape=[jax.ShapeDtypeStruct((b, t, wd), BF16) for wd in _INPROJ_WIDTHS],
        grid=(b, t // tm),
        in_specs=[pl.BlockSpec((1, tm, d), lambda bi, i: (bi, i, 0)), vec, vec, const((1, d)),
                  const((d, _IN_COLS)), const((1, MLA_RANK)), const((MLA_RANK, 512)), const((LANES, 512)),
                  const((MLA_RANK, 256)), tab, tab, tab, tab],
        out_specs=[pl.BlockSpec((1, tm, wd), lambda bi, i: (bi, i, 0)) for wd in _INPROJ_WIDTHS],
        compiler_params=_cparams(("arbitrary", "arbitrary"), est),
    )(x, shift, scale, g, w, kvg, wkn, emat, wv, *tabs)


def _attend_pair(qs, pieces, sinks):
    outs = []
    for s in (0, 1):
        scores = []
        for ks, _, biases in pieces:
            sc = _dot_nt(qs[s], ks[s])
            if biases[s] is not None:
                sc = sc + biases[s]
            scores.append(sc)
        m = functools.reduce(jnp.maximum, [jnp.max(sc, axis=-1, keepdims=True) for sc in scores])
        if sinks is not None:
            m = jnp.maximum(m, sinks[s])
        den = jnp.exp(sinks[s] - m) if sinks is not None else 0.0
        o = 0.0
        for sc, (_, v, _) in zip(scores, pieces):
            p = jnp.exp(sc - m)
            den = den + jnp.sum(p, axis=-1, keepdims=True)
            o = o + _dot(p.astype(BF16), v)
        outs.append(o / den)
    return jnp.where(_low_half(outs[0].shape), outs[0], outs[1])


def _mla_kernel(q_ref, kc_ref, vc_ref, kl_ref, vl_ref, o_ref, m_ref, l_ref, acc_ref):
    j = pl.program_id(2)

    @pl.when(j == 0)
    def _():
        m_ref[...] = jnp.full(m_ref.shape, NEG_INF, F32)
        l_ref[...] = jnp.zeros(l_ref.shape, F32)
        acc_ref[...] = jnp.zeros(acc_ref.shape, F32)

    def update(k, v):
        lo = _low_half((1, LANES))
        m0, m1 = _half_masks()
        reps = k.shape[0] // LANES
        for pair in range(2):
            vp = v[:, pair * LANES:(pair + 1) * LANES]
            vals = (vp * m0 + m1, vp * m1 + m0)
            alphas, pvs = [], []
            for s in (0, 1):
                hd = 2 * pair + s
                q = q_ref[0, :, hd * LANES:(hd + 1) * LANES]
                sc = _dot_nt(q, k[:, hd * LANES:(hd + 1) * LANES])
                m_prev = m_ref[hd]
                m_new = jnp.maximum(m_prev, jnp.max(sc, axis=-1, keepdims=True))
                m_ref[hd] = m_new
                alphas.append(jnp.exp(m_prev - m_new))
                p = jnp.exp(sc - jnp.tile(m_new, (1, reps)))
                pvs.append(_dot(p.astype(BF16), vals[s]))
            acc_ref[pair] = (acc_ref[pair] * jnp.where(lo, alphas[0], alphas[1])
                             + jnp.where(lo, pvs[0], pvs[1]))
            l_ref[pair] = (l_ref[pair] * jnp.where(lo, alphas[1], alphas[0])
                           + jnp.where(lo, pvs[1], pvs[0]))

    @pl.when(j == 0)
    def _():
        update(kc_ref[0], vc_ref[0])

    @pl.when(j > 0)
    def _():
        update(kl_ref[0], vl_ref[0])

    @pl.when(j == pl.num_programs(2) - 1)
    def _():
        for pair in range(2):
            inv = pltpu.roll(1.0 / l_ref[pair], HEAD_DIM, 1)
            o_ref[0, :, pair * LANES:(pair + 1) * LANES] = (acc_ref[pair] * inv).astype(BF16)


def _mla_attention(q, kc, vc, kl, vl):
    b, n, _ = q.shape
    c = kc.shape[1]
    tq = min(1024, n)
    tk = min(512, n)
    est = 2 * (tq * 512 * 2 + c * 768 * 2 + tk * 768 * 2 + tq * 256 * 2) + 6 * tq * LANES * 4 + 10 * tq * tk * 4 + 8 * MIB
    return pl.pallas_call(
        _mla_kernel,
        out_shape=jax.ShapeDtypeStruct((b, n, 256), BF16),
        grid=(b, n // tq, 1 + n // tk),
        in_specs=[pl.BlockSpec((1, tq, 512), lambda bi, i, j: (bi, i, 0)),
                  pl.BlockSpec((1, c, 512), lambda bi, i, j: (bi, 0, 0)),
                  pl.BlockSpec((1, c, 256), lambda bi, i, j: (bi, 0, 0)),
                  pl.BlockSpec((1, tk, 512), lambda bi, i, j: (bi, jnp.maximum(j - 1, 0), 0)),
                  pl.BlockSpec((1, tk, 256), lambda bi, i, j: (bi, jnp.maximum(j - 1, 0), 0))],
        out_specs=pl.BlockSpec((1, tq, 256), lambda bi, i, j: (bi, i, 0)),
        scratch_shapes=[pltpu.VMEM((4, tq, LANES), F32), pltpu.VMEM((2, tq, LANES), F32), pltpu.VMEM((2, tq, LANES), F32)],
        compiler_params=_cparams(("arbitrary", "arbitrary", "arbitrary"), est),
    )(q, kc, vc, kl, vl)


def _swa_kernel(q_ref, kp_ref, kcur_ref, kn_ref, vp_ref, vcur_ref, vn_ref, kctx_ref, vctx_ref, sink_ref, o_ref):
    i = pl.program_id(1)
    last = pl.num_programs(1) - 1
    tq = q_ref.shape[1]

    def band_bias(width, offset, edge_penalty):
        r = lax.broadcasted_iota(jnp.int32, (tq, width), 0)
        cidx = lax.broadcasted_iota(jnp.int32, (tq, width), 1)
        rel = cidx + offset - r
        inside = jnp.where(rel >= -SWA_WINDOW, jnp.where(rel <= SWA_WINDOW, 1, 0), 0)
        return jnp.where(inside == 1, edge_penalty, NEG_INF)

    halo = kp_ref.shape[1]
    b_prev = band_bias(halo, -halo, jnp.where(i > 0, 0.0, NEG_INF))
    b_cur = band_bias(tq, 0, 0.0)
    b_next = band_bias(halo, tq, jnp.where(i < last, 0.0, NEG_INF))
    m0, m1 = _half_masks()
    kctx, vctx = kctx_ref[0], vctx_ref[0]
    pieces = [((kp_ref[0],) * 2, vp_ref[0], (b_prev,) * 2),
              ((kcur_ref[0],) * 2, vcur_ref[0], (b_cur,) * 2),
              ((kn_ref[0],) * 2, vn_ref[0], (b_next,) * 2),
              ((kctx,) * 2, vctx, (None, None))]
    for grp in range(2):
        qg = q_ref[0, :, grp * LANES:(grp + 1) * LANES]
        sinks = [sink_ref[2 * grp + s:2 * grp + s + 1, 0:1] for s in (0, 1)]
        o_ref[0, :, grp * LANES:(grp + 1) * LANES] = _attend_pair((qg * m0, qg * m1), pieces, sinks).astype(BF16)


def _swa_attention(q, k, v, kc, vc, sink_rows):
    b, n, _ = q.shape
    c = kc.shape[1]
    tq = 256
    halo = SWA_WINDOW
    r = tq // halo
    nh = n // halo
    cur = lambda bi, i: (bi, i, 0)
    prev = lambda bi, i: (bi, jnp.maximum(i * r - 1, 0), 0)
    nxt = lambda bi, i: (bi, jnp.minimum(i * r + r, nh - 1), 0)
    ctx = lambda bi, i: (bi, 0, 0)
    est = 16 * tq * (tq + 2 * halo + c) * 4 + 8 * MIB
    return pl.pallas_call(
        _swa_kernel,
        out_shape=jax.ShapeDtypeStruct((b, n, 256), BF16),
        grid=(b, n // tq),
        in_specs=[pl.BlockSpec((1, tq, 256), cur),
                  pl.BlockSpec((1, halo, LANES), prev), pl.BlockSpec((1, tq, LANES), cur), pl.BlockSpec((1, halo, LANES), nxt),
                  pl.BlockSpec((1, halo, LANES), prev), pl.BlockSpec((1, tq, LANES), cur), pl.BlockSpec((1, halo, LANES), nxt),
                  pl.BlockSpec((1, c, LANES), ctx), pl.BlockSpec((1, c, LANES), ctx),
                  pl.BlockSpec((4, LANES), lambda bi, i: (0, 0))],
        out_specs=pl.BlockSpec((1, tq, 256), cur),
        compiler_params=_cparams(("arbitrary", "arbitrary"), est),
    )(q, k, k, k, v, v, v, kc, vc, sink_rows)


def _na_kernel(q_ref, kp_ref, kcur_ref, kn_ref, vp_ref, vcur_ref, vn_ref, kctx_ref, vctx_ref, bias_ref, o_ref):
    tq = q_ref.shape[1]
    m0, m1 = _half_masks()
    for pair in range(2):
        sl = slice(pair * LANES, (pair + 1) * LANES)
        qp = q_ref[0, :, sl]
        pieces = []
        for idx, (kr, vr) in enumerate(((kp_ref, vp_ref), (kcur_ref, vcur_ref), (kn_ref, vn_ref))):
            biases = tuple(bias_ref[0, 2 * pair + s, :, idx * tq:(idx + 1) * tq] for s in (0, 1))
            pieces.append(((kr[0, :, sl],) * 2, vr[0, :, sl], biases))
        pieces.append(((kctx_ref[0, :, sl],) * 2, vctx_ref[0, :, sl], (None, None)))
        o_ref[0, :, sl] = _attend_pair((qp * m0, qp * m1), pieces, None).astype(BF16)


def _na_bias_tables(rpb, rows):
    nb = rows // NA_ROWS
    heads = rpb.shape[0]
    width = 2 * NA_KW - 1
    span = 2 * GRID_W - 1
    lead = GRID_W - 2 - (NA_KW - 1)
    rpb = rpb.astype(F32)
    per_row = jnp.stack([rpb[:, NA_ROWS - 1 - a:NA_ROWS - 1 - a + 3 * NA_ROWS, :] for a in range(NA_ROWS)], axis=1)
    padded = jnp.pad(per_row, ((0, 0), (0, 0), (0, 0), (lead, span - width - lead)))
    tiled = jnp.broadcast_to(padded[:, :, :, None, :], (heads, NA_ROWS, 3 * NA_ROWS, GRID_W, span))
    skew = tiled.reshape(heads, NA_ROWS, 3 * NA_ROWS, GRID_W * span)[..., :GRID_W * (span - 1)]
    toep = skew.reshape(heads, NA_ROWS, 3 * NA_ROWS, GRID_W, span - 1)[..., GRID_W - 2:]
    table = jnp.transpose(toep, (0, 1, 3, 2, 4)).reshape(heads, NA_ROWS * GRID_W, 3 * NA_ROWS * GRID_W)
    a = np.arange(NA_ROWS)[:, None, None, None]
    cq = np.arange(GRID_W)[None, :, None, None]
    kr = np.arange(3 * NA_ROWS)[None, None, :, None]
    ck = np.arange(GRID_W)[None, None, None, :]
    valids = []
    for j in (0, 1, nb - 1):
        r = NA_ROWS * j + a
        rs = np.clip(r - NA_KH // 2, 0, rows - NA_KH)
        rk = NA_ROWS * (j - 1) + kr
        cs = np.clip(cq - NA_KW // 2, 0, GRID_W - NA_KW)
        valid = (rk >= rs) & (rk < rs + NA_KH) & (ck >= cs) & (ck < cs + NA_KW)
        valids.append(valid.reshape(NA_ROWS * GRID_W, 3 * NA_ROWS * GRID_W))
    valid = jnp.asarray(np.stack(valids))
    return jnp.where(valid[:, None], table[None], NEG_INF)


def _na_attention(q, k, v, kc, vc, bias):
    b, n, _ = q.shape
    c = kc.shape[1]
    tq = NA_ROWS * GRID_W
    nb = n // tq
    cur = lambda bi, i: (bi, i, 0)
    prev = lambda bi, i: (bi, jnp.maximum(i - 1, 0), 0)
    nxt = lambda bi, i: (bi, jnp.minimum(i + 1, nb - 1), 0)
    ctx = lambda bi, i: (bi, 0, 0)
    variant = lambda bi, i: (jnp.where(i == 0, 0, jnp.where(i == nb - 1, 2, 1)), 0, 0, 0)
    blk = pl.BlockSpec((1, tq, 256), cur)
    est = 2 * 4 * tq * 3 * tq * 4 + 16 * tq * (3 * tq + c) * 4 + 8 * MIB
    return pl.pallas_call(
        _na_kernel,
        out_shape=jax.ShapeDtypeStruct((b, n, 256), BF16),
        grid=(b, nb),
        in_specs=[blk, pl.BlockSpec((1, tq, 256), prev), blk, pl.BlockSpec((1, tq, 256), nxt),
                  pl.BlockSpec((1, tq, 256), prev), blk, pl.BlockSpec((1, tq, 256), nxt),
                  pl.BlockSpec((1, c, 256), ctx), pl.BlockSpec((1, c, 256), ctx),
                  pl.BlockSpec((1, 4, tq, 3 * tq), variant)],
        out_specs=blk,
        compiler_params=_cparams(("arbitrary", "arbitrary"), est),
    )(q, k, k, k, v, v, v, kc, vc, bias)


def _ctx_attn_kernel(mq_ref, mk_ref, mv_ref, sq_ref, sk_ref, sv_ref, sink_ref, nq_ref, nk_ref, nv_ref,
                     ya_ref, yb_ref, yc_ref):
    m0, m1 = _half_masks()
    none2 = (None, None)
    for pair in range(2):
        sl = slice(pair * LANES, (pair + 1) * LANES)
        h0 = slice(2 * pair * LANES, (2 * pair + 1) * LANES)
        h1 = slice((2 * pair + 1) * LANES, (2 * pair + 2) * LANES)
        ya_ref[0, :, sl] = _attend_pair((mq_ref[0, :, h0], mq_ref[0, :, h1]),
                                        [((mk_ref[0, :, h0], mk_ref[0, :, h1]), mv_ref[0, :, sl], none2)],
                                        None).astype(BF16)
        qg = sq_ref[0, :, sl]
        sinks = [sink_ref[2 * pair + s:2 * pair + s + 1, 0:1] for s in (0, 1)]
        yb_ref[0, :, sl] = _attend_pair((qg * m0, qg * m1), [((sk_ref[0],) * 2, sv_ref[0], none2)],
                                        sinks).astype(BF16)
        qn = nq_ref[0, :, sl]
        yc_ref[0, :, sl] = _attend_pair((qn * m0, qn * m1), [((nk_ref[0, :, sl],) * 2, nv_ref[0, :, sl], none2)],
                                        None).astype(BF16)


def _ctx_attention(mq, mk, mv, sq, sk, sv, sink_rows, nq, nk, nv):
    b, c, _ = mq.shape
    spec = lambda wd: pl.BlockSpec((1, c, wd), lambda bi: (bi, 0, 0))
    return pl.pallas_call(
        _ctx_attn_kernel,
        out_shape=[jax.ShapeDtypeStruct((b, c, 256), BF16)] * 3,
        grid=(b,),
        in_specs=[spec(512), spec(512), spec(256), spec(256), spec(LANES), spec(LANES),
                  pl.BlockSpec((4, LANES), lambda bi: (0, 0)), spec(256), spec(256), spec(256)],
        out_specs=[spec(256)] * 3,
        compiler_params=_cparams(("arbitrary",), 24 * MIB),
    )(mq, mk, mv, sq, sk, sv, sink_rows, nq, nk, nv)


def _ret_chunk(q, k, v, g, dec_lane, gn_lane, s_scr, reverse):
    c = RET_CHUNK
    lo = _low_half((1, LANES))
    lg = -(jnp.maximum(-dec_lane, 0.0) + jnp.log1p(jnp.exp(-jnp.abs(dec_lane))))
    ti = lax.broadcasted_iota(jnp.int32, (c, LANES), 0).astype(F32)
    if reverse:
        qpow, kpow = c - ti, ti
    else:
        qpow, kpow = ti + 1.0, c - 1.0 - ti
    qdec = jnp.exp(lg * qpow)
    kdec = jnp.exp(lg * kpow)
    cdec = jnp.exp(lg * float(c))
    ii = lax.broadcasted_iota(jnp.int32, (c, c), 0)
    jj = lax.broadcasted_iota(jnp.int32, (c, c), 1)
    dist = (jj - ii) if reverse else (ii - jj)
    distf = jnp.maximum(dist, 0).astype(F32)
    m0, m1 = _half_masks()
    outs = []
    for s, msk in ((0, m0), (1, m1)):
        lg_s = lg[:, s * HEAD_DIM:s * HEAD_DIM + 1]
        decay = jnp.where(dist >= 0, jnp.exp(lg_s * distf), 0.0)
        inner = _dot_nt(q * msk, k) * decay
        outs.append(_dot(inner.astype(BF16), v))
    state = s_scr[...]
    o = jnp.where(lo, outs[0], outs[1]) + _dot((q.astype(F32) * qdec).astype(BF16), state.astype(BF16))
    kd = (k.astype(F32) * kdec).astype(BF16)
    same_head = (ii < HEAD_DIM) == (jj < HEAD_DIM)
    s_scr[...] = state * cdec + jnp.where(same_head, _dot_tn(kd, v), 0.0)

    def head_mean(x):
        s_lo = jnp.sum(jnp.where(lo, x, 0.0), axis=-1, keepdims=True)
        s_hi = jnp.sum(jnp.where(lo, 0.0, x), axis=-1, keepdims=True)
        return jnp.where(lo, s_lo, s_hi) * (1.0 / HEAD_DIM)

    dev = o - head_mean(o)
    normed = dev * lax.rsqrt(head_mean(dev * dev) + EPS)
    gf = g.astype(F32)
    return (gf * jax.nn.sigmoid(gf)) * (normed * gn_lane)


def _ret_kernel(decf_ref, decb_ref, gnf_ref, gnb_ref, s0f_ref, s0b_ref,
                qf_ref, kf_ref, vf_ref, gf_ref, qb_ref, kb_ref, vb_ref, gb_ref,
                yf_ref, yb_ref, sf_ref, sb_ref, sf_scr, sb_scr):
    t = pl.program_id(2)

    @pl.when(t == 0)
    def _():
        sf_scr[...] = s0f_ref[0, 0]
        sb_scr[...] = s0b_ref[0, 0]

    yf_ref[0] = _ret_chunk(qf_ref[0], kf_ref[0], vf_ref[0], gf_ref[0], decf_ref[0], gnf_ref[...], sf_scr,
                           False).astype(BF16)
    yb_ref[0] = _ret_chunk(qb_ref[0], kb_ref[0], vb_ref[0], gb_ref[0], decb_ref[0], gnb_ref[...], sb_scr,
                           True).astype(BF16)

    @pl.when(t == pl.num_programs(2) - 1)
    def _():
        sf_ref[0, 0] = sf_scr[...]
        sb_ref[0, 0] = sb_scr[...]


def _retention(q, k, v, gf, gb, dec_f, dec_b, gn_f, gn_b, s0f, s0b):
    b, t, _ = q.shape
    c = RET_CHUNK
    nt = t // c
    fwd = pl.BlockSpec((1, c, LANES), lambda bi, p, i: (bi, i, p))
    bwd = pl.BlockSpec((1, c, LANES), lambda bi, p, i: (bi, nt - 1 - i, p))
    dec = pl.BlockSpec((1, 1, LANES), lambda bi, p, i: (p, 0, 0))
    gn = pl.BlockSpec((1, LANES), lambda bi, p, i: (0, p))
    st = pl.BlockSpec((1, 1, LANES, LANES), lambda bi, p, i: (bi, p, 0, 0))
    return pl.pallas_call(
        _ret_kernel,
        out_shape=[jax.ShapeDtypeStruct((b, t, 256), BF16)] * 2 + [jax.ShapeDtypeStruct((b, 2, LANES, LANES), F32)] * 2,
        grid=(b, 2, nt),
        in_specs=[dec, dec, gn, gn, st, st, fwd, fwd, fwd, fwd, bwd, bwd, bwd, bwd],
        out_specs=[fwd, bwd, st, st],
        scratch_shapes=[pltpu.VMEM((LANES, LANES), F32)] * 2,
        compiler_params=_cparams(("arbitrary", "arbitrary", "arbitrary"), 16 * MIB),
    )(dec_f, dec_b, gn_f, gn_b, s0f, s0b, q, k, v, gf, q, k, v, gb)


def _outproj_kernel(x_ref, ya_ref, yb_ref, yc_ref, ydf_ref, ydb_ref, beta_ref, w_ref, gate_ref, o_ref):
    ys = (ya_ref[0].astype(F32), yb_ref[0].astype(F32), yc_ref[0].astype(F32),
          ydf_ref[0].astype(F32) + ydb_ref[0].astype(F32))
    acc = 0.0
    for s, y in enumerate(ys):
        sl = slice(s * 256, (s + 1) * 256)
        acc = acc + _dot((y * beta_ref[:, sl]).astype(BF16), w_ref[sl, :])
    o_ref[0] = x_ref[0] + gate_ref[0] * acc


def _out_projection(x, ya, yb, yc, ydf, ydb, beta, w, gate):
    b, t, d = x.shape
    tm = min(512, t)
    row = lambda wd: pl.BlockSpec((1, tm, wd), lambda bi, i: (bi, i, 0))
    est = 2 * (2 * tm * d * 4 + 5 * tm * 256 * 2 + d * d * 2) + 8 * MIB
    return pl.pallas_call(
        _outproj_kernel,
        out_shape=jax.ShapeDtypeStruct((b, t, d), F32),
        grid=(b, t // tm),
        in_specs=[row(d), row(256), row(256), row(256), row(256), row(256),
                  pl.BlockSpec((1, d), lambda bi, i: (0, 0)), pl.BlockSpec((d, d), lambda bi, i: (0, 0)),
                  pl.BlockSpec((1, 1, d), lambda bi, i: (bi, 0, 0))],
        out_specs=row(d),
        compiler_params=_cparams(("arbitrary", "arbitrary"), est),
    )(x, ya, yb, yc, ydf, ydb, beta, w, gate)


def _sorting_network(n):
    size = 1
    while size < n:
        size *= 2
    pairs = []
    p = 1
    while p < size:
        k = p
        while k >= 1:
            for j in range(k % p, size - k, 2 * k):
                for i in range(min(k, size - j - k)):
                    if (i + j) // (2 * p) == (i + j + k) // (2 * p):
                        pairs.append((i + j, i + j + k))
            k //= 2
        p *= 2
    return [(i, j) for i, j in pairs if j < n]


def _top_values(scores, out_ref, count):
    groups = scores.shape[0] // 8
    lists = [scores[8 * g:8 * g + 8, :] for g in range(groups)]
    for i, j in _sorting_network(groups):
        lists[i], lists[j] = jnp.maximum(lists[i], lists[j]), jnp.minimum(lists[i], lists[j])
    for r in range(count):
        m = jnp.max(lists[0], axis=0, keepdims=True)
        out_ref[r:r + 1, :] = m
        hit = lists[0] == m
        need = count - r - 1
        for p in range(min(groups - 1, need)):
            lists[p] = jnp.where(hit, lists[p + 1], lists[p])
        if need >= groups:
            lists[groups - 1] = jnp.where(hit, -jnp.inf, lists[groups - 1])


def _staircase(w1, w2):
    row8 = lax.broadcasted_iota(jnp.int32, (8, w1.shape[1]), 0)
    cands = [w1[0:1] + w2]
    for a in range(1, 8):
        cands.append(jnp.where(row8 < PEER_TOPK // (a + 1), w1[a:a + 1] + w2[0:8], -jnp.inf))
    cands.append(w1[8:16] + w2[0:1])
    return jnp.concatenate(cands, axis=0)


def _peer_route(hd, q_scr, k1_ref, k2_ref, s2_scr, e2_scr, tau_scr, e1_scr, v1_scr, v2_scr, top_scr):
    half = PEER_NKEYS
    base = pl.multiple_of(hd * 2 * half, 2 * half)
    s1 = _dot(k1_ref[...], q_scr[pl.ds(base, half), :].astype(BF16))
    s2 = _dot(k2_ref[...], q_scr[pl.ds(base + half, half), :].astype(BF16))
    _top_values(s1, v1_scr, PEER_TOPK)
    _top_values(s2, v2_scr, PEER_TOPK)
    v1, v2 = v1_scr[...], v2_scr[...]
    cand = _staircase(v1, v2)
    _top_values(cand, top_scr, PEER_TOPK)
    top = top_scr[...]
    z = jnp.sum(jnp.exp(top - top[0:1]), axis=0, keepdims=True)
    sel = cand >= top[PEER_TOPK - 1:PEER_TOPK]
    row_min = lambda picked: jnp.min(picked, axis=0, keepdims=True)
    taus = [row_min(jnp.where(sel[0:16], v2, jnp.inf))]
    for a in range(1, 8):
        taus.append(row_min(jnp.where(sel[8 + 8 * a:16 + 8 * a], v2[0:8], jnp.inf)))
    tail = jnp.where(sel[72:80], v2[0:1], jnp.inf)
    taus += [tail[a:a + 1] for a in range(8)]
    tau = jnp.full(s1.shape, jnp.inf, F32)
    for a in reversed(range(PEER_TOPK)):
        tau = jnp.where(s1 >= v1[a:a + 1], taus[a], tau)
    s2_scr[hd] = s2
    tau_scr[hd] = tau
    e1_scr[hd] = jnp.exp(s1 - v1[0:1]) / z
    e2_scr[hd] = jnp.exp(s2 - v2[0:1]) * 0.5


def _peer_kernel(*refs, final_norm):
    if final_norm:
        (x_ref, sh_ref, sc_ref, gt_ref, g_ref, wqt_ref, k1_ref, k2_ref, u_ref, vt_ref, fg_ref, o_ref,
         ht_scr, q_scr, s2_scr, e2_scr, tau_scr, e1_scr, v1_scr, v2_scr, top_scr,
         a0_scr, a1_scr, acc_scr) = refs
    else:
        (x_ref, sh_ref, sc_ref, gt_ref, g_ref, wqt_ref, k1_ref, k2_ref, u_ref, vt_ref, o_ref,
         ht_scr, q_scr, s2_scr, e2_scr, tau_scr, e1_scr, v1_scr, v2_scr, top_scr,
         a0_scr, a1_scr, acc_scr) = refs
        fg_ref = None
    step = pl.program_id(2)
    tokens = ht_scr.shape[1]

    @pl.when(step == 0)
    def _():
        h = _norm_mod(x_ref[0], g_ref[...], sh_ref[0], sc_ref[0])
        ht = jnp.transpose(h).astype(BF16)
        ht_scr[...] = ht
        q_scr[...] = _dot(wqt_ref[...], ht)
        acc_scr[...] = jnp.zeros(acc_scr.shape, F32)

        def route(hd, carry):
            _peer_route(hd, q_scr, k1_ref, k2_ref, s2_scr, e2_scr, tau_scr, e1_scr, v1_scr, v2_scr, top_scr)
            return carry

        lax.fori_loop(0, PEER_HEADS, route, 0)

    def project(a_write):
        a_write[...] = _dot(u_ref[...], ht_scr[...])

    def consume(a_read):
        y = None
        piece = 2 * PEER_NKEYS
        for kt in range(PEER_ROWS // 2):
            ws = []
            for r in (2 * kt, 2 * kt + 1):
                rows = slice(r * PEER_NKEYS, (r + 1) * PEER_NKEYS)
                i = (step - 1) * PEER_ROWS + r
                gate = jnp.zeros((PEER_NKEYS, tokens), F32)
                for hd in range(PEER_HEADS):
                    val = e2_scr[hd] * e1_scr[hd, pl.ds(i, 1), :]
                    gate = gate + jnp.where(s2_scr[hd] >= tau_scr[hd, pl.ds(i, 1), :], val, 0.0)
                a = a_read[rows, :]
                gelu2 = a * (1.0 + lax.erf(a * np.float32(np.sqrt(0.5))))
                ws.append((gelu2 * gate).astype(BF16))
            part = _dot(vt_ref[:, kt * piece:(kt + 1) * piece], jnp.concatenate(ws, axis=0))
            y = part if y is None else y + part
        acc_scr[...] += y

    last = pl.num_programs(2) - 1
    bufs = (a0_scr, a1_scr)

    @pl.when(step == 0)
    def _():
        project(a0_scr)

    for parity in (0, 1):
        @pl.when(jnp.logical_and(jnp.logical_and(step > 0, step < last), step % 2 == parity))
        def _():
            project(bufs[parity])
            consume(bufs[1 - parity])

        @pl.when(jnp.logical_and(step == last, step % 2 == parity))
        def _():
            consume(bufs[1 - parity])

    @pl.when(step == pl.num_programs(2) - 1)
    def _():
        out = x_ref[0] + gt_ref[0] * jnp.transpose(acc_scr[...])
        if final_norm:
            out = _rmsnorm(out, fg_ref[...])
        o_ref[0] = out


def _peer_ffn(x, shift, scale, gate, g, wqt, k1, k2, u, vt, final_g=None):
    b, t, d = x.shape
    tt = PEER_TOKENS
    ec = PEER_ROWS * PEER_NKEYS
    n_blocks = u.shape[0] // ec
    nq = wqt.shape[0]
    final_norm = final_g is not None
    tok = pl.BlockSpec((1, tt, d), lambda bi, i, s: (bi, i, 0))
    vec = pl.BlockSpec((1, 1, d), lambda bi, i, s: (bi, 0, 0))
    const = lambda shape: pl.BlockSpec(shape, lambda bi, i, s: (0,) * len(shape))
    in_specs = [tok, vec, vec, vec, const((1, d)), const((nq, d)), const(k1.shape), const(k2.shape),
                pl.BlockSpec((ec, d), lambda bi, i, s: (jnp.minimum(s, n_blocks - 1), 0)),
                pl.BlockSpec((d, ec), lambda bi, i, s: (0, jnp.clip(s - 1, 0, n_blocks - 1)))]
    args = [x, shift, scale, gate, g, wqt, k1, k2, u, vt]
    if final_norm:
        in_specs.append(const((1, d)))
        args.append(final_g)
    table = pltpu.VMEM((PEER_HEADS, PEER_NKEYS, tt), F32)
    top = pltpu.VMEM((PEER_TOPK, tt), F32)
    a_buf = pltpu.VMEM((ec, tt), F32)
    scratch = [pltpu.VMEM((d, tt), BF16), pltpu.VMEM((nq, tt), F32), table, table, table, table,
               top, top, top, a_buf, a_buf, pltpu.VMEM((d, tt), F32)]
    est = (4 * tt * d * 4 + 2 * nq * d * 2 + 4 * ec * d * 2 + tt * d * 2 + nq * tt * 4
           + 4 * PEER_HEADS * PEER_NKEYS * tt * 4 + d * tt * 4 + 2 * ec * tt * 6 + 8 * MIB)
    return pl.pallas_call(
        functools.partial(_peer_kernel, final_norm=final_norm),
        out_shape=jax.ShapeDtypeStruct((b, t, d), F32),
        grid=(b, t // tt, n_blocks + 1),
        in_specs=in_specs,
        out_specs=tok,
        scratch_shapes=scratch,
        compiler_params=_cparams(("arbitrary", "arbitrary", "arbitrary"), est),
    )(*args)


def _prep_in_weights(w_in):
    d = w_in.shape[0]
    zeros = lambda n: jnp.zeros((d, n), w_in.dtype)
    cols = []
    for hd in range(4):
        cols += [w_in[:, hd * MLA_QK:(hd + 1) * MLA_QK], zeros(LANES - MLA_QK)]
    cols.append(w_in[:, 384:512])
    cols += [zeros(HEAD_DIM), w_in[:, 512:544], zeros(LANES - HEAD_DIM - MLA_ROPE)]
    for hd in (0, 2, 1, 3):
        cols.append(w_in[:, 544 + hd * HEAD_DIM:544 + (hd + 1) * HEAD_DIM])
    cols.append(w_in[:, 800:])
    return jnp.concatenate(cols, axis=1).astype(BF16)


def _prep_kv_weights(w_ukv):
    zeros = jnp.zeros((MLA_RANK, HEAD_DIM), w_ukv.dtype)
    wkn = jnp.concatenate([blk for hd in range(4) for blk in (w_ukv[:, hd * LANES:hd * LANES + HEAD_DIM], zeros)], axis=1)
    wv = jnp.concatenate([w_ukv[:, hd * LANES + HEAD_DIM:(hd + 1) * LANES] for hd in range(4)], axis=1)
    sel = np.zeros((LANES, 4 * LANES), np.float32)
    for hd in range(4):
        for r in range(MLA_ROPE):
            sel[HEAD_DIM + r, hd * LANES + HEAD_DIM + r] = 1.0
    return wkn.astype(BF16), jnp.asarray(sel, BF16), wv.astype(BF16)


def _rope_lane_tables(n):
    def tables(rot_dim):
        quarter = rot_dim // 4
        inv = ROPE_BASE ** (-jnp.arange(quarter, dtype=F32) / quarter)
        t = jnp.arange(n, dtype=jnp.int32)
        pos = jnp.stack([t // GRID_W, t % GRID_W], axis=-1).astype(F32)
        ang = pos[:, :, None] * inv
        lanes = lambda tb: jnp.concatenate([tb[:, 0], tb[:, 0], tb[:, 1], tb[:, 1]], axis=-1)
        return lanes(jnp.cos(ang)), lanes(jnp.sin(ang))

    c32, s32 = tables(MLA_ROPE)
    c64, s64 = tables(HEAD_DIM)
    ones = lambda w: jnp.ones((n, w), F32)
    zeros = lambda w: jnp.zeros((n, w), F32)
    return (jnp.concatenate([ones(HEAD_DIM), c32, ones(LANES - HEAD_DIM - MLA_ROPE)], axis=-1),
            jnp.concatenate([zeros(HEAD_DIM), s32, zeros(LANES - HEAD_DIM - MLA_ROPE)], axis=-1),
            jnp.concatenate([c64, c64], axis=-1), jnp.concatenate([s64, s64], axis=-1))


_SWA_SLOT_HEADS = (0, 2, 1, 3)


def kernel(x, c, ctx, c_ctx, ada_w, ada_b, norm_mix_g, w_in, mla_kv_norm_g, mla_w_ukv, swa_sink, na_rpb, ret_decay_f, ret_decay_b, ret_gn_f, ret_gn_b, mix_beta, w_out, norm_ffn_g, peer_wq, peer_k1, peer_k2, peer_u, peer_v, final_norm_g):
    b, n, d = x.shape
    n_ctx = ctx.shape[1]
    depth = ada_w.shape[0]
    rows = n // GRID_W

    pad_rows = -(b + 1) % 8
    cc = jnp.concatenate([c, c_ctx[None, :], jnp.zeros((pad_rows, d), F32)], axis=0)
    mod = _ada_modulation(cc, ada_w, ada_b)

    lat_tabs = _rope_lane_tables(n)
    ctx_tabs = (jnp.ones((n_ctx, LANES), F32), jnp.zeros((n_ctx, LANES), F32)) * 2
    slot_heads = np.asarray(_SWA_SLOT_HEADS)
    out_perm = np.arange(d)
    out_perm[256:512] = 256 + (slot_heads[:, None] * HEAD_DIM + np.arange(HEAD_DIM)[None, :]).reshape(-1)
    zero_state = jnp.zeros((b, 2, LANES, LANES), F32)
    pair_lanes = lambda p: jnp.repeat(p.astype(F32), HEAD_DIM).reshape(2, 1, LANES)

    xc = ctx
    for layer in range(depth):
        with_ctx = layer < depth - 1
        last = layer == depth - 1
        chunk = lambda k, lo, hi: mod[layer, lo:hi, k * d:(k + 1) * d][:, None, :]
        mod_l = [chunk(k, 0, b) for k in range(6)]
        mod_c = [jnp.broadcast_to(chunk(k, b, b + 1), (b, 1, d)) for k in range(6)]

        w_cols = _prep_in_weights(w_in[layer])
        wkn, emat, wv = _prep_kv_weights(mla_w_ukv[layer])
        g_mix = norm_mix_g[layer][None, :]
        kvg = mla_kv_norm_g[layer][None, :]
        p_l = _in_projection(x, mod_l[0], mod_l[1], g_mix, w_cols, kvg, wkn, emat, wv, lat_tabs)
        p_c = _in_projection(xc, mod_c[0], mod_c[1], g_mix, w_cols, kvg, wkn, emat, wv, ctx_tabs)
        mq, mk, mv, sq, sk, sv, nq, nk, nv, rq, rk, rv, gf, gb = p_l
        cmq, cmk, cmv, csq, csk, csv, cnq, cnk, cnv, crq, crk, crv, cgf, cgb = p_c

        sink_rows = jnp.broadcast_to(swa_sink[layer].astype(F32)[slot_heads][:, None], (4, LANES))
        bias = _na_bias_tables(na_rpb[layer], rows)
        dec_f, dec_b = pair_lanes(ret_decay_f[layer]), pair_lanes(ret_decay_b[layer])
        gn_f, gn_b = ret_gn_f[layer][None, :], ret_gn_b[layer][None, :]

        ya = _mla_attention(mq, cmk, cmv, mk, mv)
        yb = _swa_attention(sq, sk, sv, csk, csv, sink_rows)
        yc = _na_attention(nq, nk, nv, cnk, cnv, bias)
        cyf, cyb, s_f, s_b = _retention(crq, crk, crv, cgf, cgb, dec_f, dec_b, gn_f, gn_b, zero_state, zero_state)
        ydf, ydb, _, _ = _retention(rq, rk, rv, gf, gb, dec_f, dec_b, gn_f, gn_b, s_f, s_b)

        beta = mix_beta[layer][out_perm][None, :]
        w_o = w_out[layer][out_perm, :].astype(BF16)
        x = _out_projection(x, ya, yb, yc, ydf, ydb, beta, w_o, mod_l[2])

        g_ffn = norm_ffn_g[layer][None, :]
        wqt = peer_wq[layer].T.astype(BF16)
        k1, k2 = peer_k1[layer].astype(BF16), peer_k2[layer].astype(BF16)
        u, vt = peer_u[layer].astype(BF16), peer_v[layer].T.astype(BF16)
        x = _peer_ffn(x, mod_l[3], mod_l[4], mod_l[5], g_ffn, wqt, k1, k2, u, vt,
                      final_norm_g[None, :] if last else None)
        if with_ctx:
            cya, cyb_, cyc = _ctx_attention(cmq, cmk, cmv, csq, csk, csv, sink_rows, cnq, cnk, cnv)
            xc = _out_projection(xc, cya, cyb_, cyc, cyf, cyb, beta, w_o, mod_c[2])
            xc = _peer_ffn(xc, mod_c[3], mod_c[4], mod_c[5], g_ffn, wqt, k1, k2, u, vt)
    return x
```

```python
import functools

import numpy as np
import jax
import jax.numpy as jnp
from jax import lax
from jax.experimental import pallas as pl
from jax.experimental.pallas import tpu as pltpu

F32 = jnp.float32
BF16 = jnp.bfloat16

EPS = 1e-6
NEG_INF = -1e30
ROPE_BASE = 10000.0
GRID_W = 64
HEAD_DIM = 64
LANES = 128
V7X_VMEM_BYTES = 64 * 1024 * 1024
MIB = 1024 * 1024

MLA_ROPE = 32
MLA_QK = 96
MLA_RANK = 128
SWA_WINDOW = 128
NA_KH, NA_KW = 8, 16
NA_ROWS = 4
RET_CHUNK = 128
PEER_HEADS = 8
PEER_NKEYS = 128
PEER_TOPK = 16
PEER_TOKENS = 256
PEER_ROWS = 16

_MQ, _CKV, _KR, _SQ, _SK, _SV, _NQ, _NK, _NV, _RQ, _RK, _RV, _GF, _GB, _IN_COLS = (
    0, 512, 640, 768, 1024, 1152, 1280, 1536, 1792, 2048, 2304, 2560, 2816, 3072, 3328)


def _cparams(semantics, vmem_bytes):
    limit = int(min(vmem_bytes, V7X_VMEM_BYTES * 7 // 8))
    return pltpu.CompilerParams(dimension_semantics=semantics, vmem_limit_bytes=limit)


def _dot(a, b):
    return jnp.dot(a, b, preferred_element_type=F32)


def _dot_nt(a, b):
    return lax.dot_general(a, b, (((1,), (1,)), ((), ())), preferred_element_type=F32)


def _dot_tn(a, b):
    return lax.dot_general(a, b, (((0,), (0,)), ((), ())), preferred_element_type=F32)


def _low_half(shape):
    return lax.broadcasted_iota(jnp.int32, shape, len(shape) - 1) < HEAD_DIM


def _half_masks():
    lo = _low_half((1, LANES))
    return (jnp.where(lo, 1.0, 0.0).astype(BF16), jnp.where(lo, 0.0, 1.0).astype(BF16))


def _norm_mod(x, g, shift, scale):
    ms = jnp.mean(x * x, axis=-1, keepdims=True)
    return (x * lax.rsqrt(ms + EPS) * g) * (1.0 + scale) + shift


def _rmsnorm(x, g):
    ms = jnp.mean(x * x, axis=-1, keepdims=True)
    return x * lax.rsqrt(ms + EPS) * g


def _ada_kernel(c_ref, w_ref, b_ref, o_ref):
    c = c_ref[...]
    o_ref[0] = _dot(c * jax.nn.sigmoid(c), w_ref[0]) + b_ref[0]


def _ada_modulation(cc, ada_w, ada_b):
    depth, d, width = ada_w.shape
    rows = cc.shape[0]
    tn = 1024
    return pl.pallas_call(
        _ada_kernel,
        out_shape=jax.ShapeDtypeStruct((depth, rows, width), F32),
        grid=(depth, width // tn),
        in_specs=[pl.BlockSpec((rows, d), lambda l, j: (0, 0)),
                  pl.BlockSpec((1, d, tn), lambda l, j: (l, 0, j)),
                  pl.BlockSpec((1, 1, tn), lambda l, j: (l, 0, j))],
        out_specs=pl.BlockSpec((1, rows, tn), lambda l, j: (l, 0, j)),
        compiler_params=_cparams(("arbitrary", "arbitrary"), 2 * d * tn * 4 + 8 * MIB),
    )(cc, ada_w, ada_b.reshape(depth, 1, width))


def _rope(a, cos, sin, half):
    lane = lax.broadcasted_iota(jnp.int32, a.shape, 1)
    first = (lane % (2 * half)) < half
    rot = jnp.where(first, -pltpu.roll(a, LANES - half, 1), pltpu.roll(a, half, 1))
    return a * cos + rot * sin


def _inproj_kernel(x_ref, sh_ref, sc_ref, g_ref, w_ref, kvg_ref, wkn_ref, e_ref, wv_ref,
                   c32_ref, s32_ref, c64_ref, s64_ref,
                   mq_ref, mk_ref, mv_ref, sq_ref, sk_ref, sv_ref, nq_ref, nk_ref, nv_ref,
                   rq_ref, rk_ref, rv_ref, gf_ref, gb_ref):
    h = _norm_mod(x_ref[0], g_ref[...], sh_ref[0], sc_ref[0]).astype(BF16)

    def seg(lo, width):
        return _dot(h, w_ref[:, lo:lo + width])

    c32, s32, c64, s64 = c32_ref[...], s32_ref[...], c64_ref[...], s64_ref[...]
    mla_scale = MLA_QK ** -0.5
    head_scale = HEAD_DIM ** -0.5
    for hd in range(4):
        a = seg(_MQ + hd * LANES, LANES)
        mq_ref[0, :, hd * LANES:(hd + 1) * LANES] = (_rope(a, c32, s32, MLA_ROPE // 4) * mla_scale).astype(BF16)
    kvn = _rmsnorm(seg(_CKV, MLA_RANK), kvg_ref[...]).astype(BF16)
    kr = _rope(seg(_KR, LANES), c32, s32, MLA_ROPE // 4).astype(BF16)
    mk_ref[0] = (_dot(kvn, wkn_ref[...]) + _dot(kr, e_ref[...])).astype(BF16)
    mv_ref[0] = _dot(kvn, wv_ref[...]).astype(BF16)
    for grp in range(2):
        a = seg(_SQ + grp * LANES, LANES)
        sq_ref[0, :, grp * LANES:(grp + 1) * LANES] = (_rope(a, c64, s64, HEAD_DIM // 4) * head_scale).astype(BF16)
    sk_ref[0] = _rope(seg(_SK, LANES), c64, s64, HEAD_DIM // 4).astype(BF16)
    sv_ref[0] = seg(_SV, LANES).astype(BF16)
    nq_ref[0] = (seg(_NQ, 256) * head_scale).astype(BF16)
    nk_ref[0] = seg(_NK, 256).astype(BF16)
    nv_ref[0] = seg(_NV, 256).astype(BF16)
    rq_ref[0] = seg(_RQ, 256).astype(BF16)
    rk_ref[0] = (seg(_RK, 256) * head_scale).astype(BF16)
    rv_ref[0] = seg(_RV, 256).astype(BF16)
    gf_ref[0] = seg(_GF, 256).astype(BF16)
    gb_ref[0] = seg(_GB, 256).astype(BF16)


_INPROJ_WIDTHS = (512, 512, 256, 256, 128, 128, 256, 256, 256, 256, 256, 256, 256, 256)


def _in_projection(x, shift, scale, g, w, kvg, wkn, emat, wv, tabs):
    b, t, d = x.shape
    tm = min(512, t)
    const = lambda shape: pl.BlockSpec(shape, lambda bi, i: (0,) * len(shape))
    tab = pl.BlockSpec((tm, LANES), lambda bi, i: (i, 0))
    vec = pl.BlockSpec((1, 1, d), lambda bi, i: (bi, 0, 0))
    est = 2 * (tm * d * 4 + d * _IN_COLS * 2 + tm * sum(_INPROJ_WIDTHS) * 2 + 4 * tm * LANES * 4) + 12 * MIB
    return pl.pallas_call(
        _inproj_kernel,
        out_shape=[jax.ShapeDtypeStruct((b, t, wd), BF16) for wd in _INPROJ_WIDTHS],
        grid=(b, t // tm),
        in_specs=[pl.BlockSpec((1, tm, d), lambda bi, i: (bi, i, 0)), vec, vec, const((1, d)),
                  const((d, _IN_COLS)), const((1, MLA_RANK)), const((MLA_RANK, 512)), const((LANES, 512)),
                  const((MLA_RANK, 256)), tab, tab, tab, tab],
        out_specs=[pl.BlockSpec((1, tm, wd), lambda bi, i: (bi, i, 0)) for wd in _INPROJ_WIDTHS],
        compiler_params=_cparams(("arbitrary", "arbitrary"), est),
    )(x, shift, scale, g, w, kvg, wkn, emat, wv, *tabs)


def _attend_pair(qs, pieces, sinks):
    lo = _low_half((1, LANES))
    m0, m1 = _half_masks()
    outs = []
    for s in (0, 1):
        own, other = (m0, m1) if s == 0 else (m1, m0)
        scores = []
        for ks, _, biases in pieces:
            sc = _dot_nt(qs[s], ks[s])
            if biases[s] is not None:
                sc = sc + biases[s]
            scores.append(sc)
        m = functools.reduce(jnp.maximum, [jnp.max(sc, axis=-1, keepdims=True) for sc in scores])
        if sinks is not None:
            m = jnp.maximum(m, sinks[s])
        o = 0.0
        for sc, (_, v, _) in zip(scores, pieces):
            o = o + _dot(jnp.exp(sc - m).astype(BF16), v * own + other)
        own_lanes = lo if s == 0 else jnp.logical_not(lo)
        den = jnp.where(own_lanes, 1.0, o)
        if sinks is not None:
            den = den + jnp.exp(sinks[s] - m)
        outs.append(o * pltpu.roll(1.0 / den, HEAD_DIM, 1))
    return jnp.where(lo, outs[0], outs[1])


def _mla_kernel(q_ref, kc_ref, vc_ref, kl_ref, vl_ref, o_ref, m_ref, l_ref, acc_ref):
    j = pl.program_id(2)

    @pl.when(j == 0)
    def _():
        m_ref[...] = jnp.full(m_ref.shape, NEG_INF, F32)
        l_ref[...] = jnp.zeros(l_ref.shape, F32)
        acc_ref[...] = jnp.zeros(acc_ref.shape, F32)

    def update(k, v):
        lo = _low_half((1, LANES))
        m0, m1 = _half_masks()
        reps = k.shape[0] // LANES
        for pair in range(2):
            vp = v[:, pair * LANES:(pair + 1) * LANES]
            vals = (vp * m0 + m1, vp * m1 + m0)
            alphas, pvs = [], []
            for s in (0, 1):
                hd = 2 * pair + s
                q = q_ref[0, :, hd * LANES:(hd + 1) * LANES]
                sc = _dot_nt(q, k[:, hd * LANES:(hd + 1) * LANES])
                m_prev = m_ref[hd]
                m_new = jnp.maximum(m_prev, jnp.max(sc, axis=-1, keepdims=True))
                m_ref[hd] = m_new
                alphas.append(jnp.exp(m_prev - m_new))
                p = jnp.exp(sc - jnp.tile(m_new, (1, reps)))
                pvs.append(_dot(p.astype(BF16), vals[s]))
            acc_ref[pair] = (acc_ref[pair] * jnp.where(lo, alphas[0], alphas[1])
                             + jnp.where(lo, pvs[0], pvs[1]))
            l_ref[pair] = (l_ref[pair] * jnp.where(lo, alphas[1], alphas[0])
                           + jnp.where(lo, pvs[1], pvs[0]))

    @pl.when(j == 0)
    def _():
        update(kc_ref[0], vc_ref[0])

    @pl.when(j > 0)
    def _():
        update(kl_ref[0], vl_ref[0])

    @pl.when(j == pl.num_programs(2) - 1)
    def _():
        for pair in range(2):
            inv = pltpu.roll(1.0 / l_ref[pair], HEAD_DIM, 1)
            o_ref[0, :, pair * LANES:(pair + 1) * LANES] = (acc_ref[pair] * inv).astype(BF16)


def _mla_attention(q, kc, vc, kl, vl):
    b, n, _ = q.shape
    c = kc.shape[1]
    tq = min(1024, n)
    tk = min(512, n)
    est = 2 * (tq * 512 * 2 + c * 768 * 2 + tk * 768 * 2 + tq * 256 * 2) + 6 * tq * LANES * 4 + 10 * tq * tk * 4 + 8 * MIB
    return pl.pallas_call(
        _mla_kernel,
        out_shape=jax.ShapeDtypeStruct((b, n, 256), BF16),
        grid=(b, n // tq, 1 + n // tk),
        in_specs=[pl.BlockSpec((1, tq, 512), lambda bi, i, j: (bi, i, 0)),
                  pl.BlockSpec((1, c, 512), lambda bi, i, j: (bi, 0, 0)),
                  pl.BlockSpec((1, c, 256), lambda bi, i, j: (bi, 0, 0)),
                  pl.BlockSpec((1, tk, 512), lambda bi, i, j: (bi, jnp.maximum(j - 1, 0), 0)),
                  pl.BlockSpec((1, tk, 256), lambda bi, i, j: (bi, jnp.maximum(j - 1, 0), 0))],
        out_specs=pl.BlockSpec((1, tq, 256), lambda bi, i, j: (bi, i, 0)),
        scratch_shapes=[pltpu.VMEM((4, tq, LANES), F32), pltpu.VMEM((2, tq, LANES), F32), pltpu.VMEM((2, tq, LANES), F32)],
        compiler_params=_cparams(("arbitrary", "arbitrary", "arbitrary"), est),
    )(q, kc, vc, kl, vl)


def _swa_kernel(q_ref, kp_ref, kcur_ref, kn_ref, vp_ref, vcur_ref, vn_ref, kctx_ref, vctx_ref, sink_ref, o_ref):
    i = pl.program_id(1)
    last = pl.num_programs(1) - 1
    tq = q_ref.shape[1]

    def band_bias(width, offset, edge_penalty):
        r = lax.broadcasted_iota(jnp.int32, (tq, width), 0)
        cidx = lax.broadcasted_iota(jnp.int32, (tq, width), 1)
        rel = cidx + offset - r
        inside = jnp.where(rel >= -SWA_WINDOW, jnp.where(rel <= SWA_WINDOW, 1, 0), 0)
        return jnp.where(inside == 1, edge_penalty, NEG_INF)

    halo = kp_ref.shape[1]
    b_prev = band_bias(halo, -halo, jnp.where(i > 0, 0.0, NEG_INF))
    b_cur = band_bias(tq, 0, 0.0)
    b_next = band_bias(halo, tq, jnp.where(i < last, 0.0, NEG_INF))
    m0, m1 = _half_masks()
    kctx, vctx = kctx_ref[0], vctx_ref[0]
    pieces = [((kp_ref[0],) * 2, vp_ref[0], (b_prev,) * 2),
              ((kcur_ref[0],) * 2, vcur_ref[0], (b_cur,) * 2),
              ((kn_ref[0],) * 2, vn_ref[0], (b_next,) * 2),
              ((kctx,) * 2, vctx, (None, None))]
    for grp in range(2):
        qg = q_ref[0, :, grp * LANES:(grp + 1) * LANES]
        sinks = [sink_ref[2 * grp + s:2 * grp + s + 1, 0:1] for s in (0, 1)]
        o_ref[0, :, grp * LANES:(grp + 1) * LANES] = _attend_pair((qg * m0, qg * m1), pieces, sinks).astype(BF16)


def _swa_attention(q, k, v, kc, vc, sink_rows):
    b, n, _ = q.shape
    c = kc.shape[1]
    tq = 256
    halo = SWA_WINDOW
    r = tq // halo
    nh = n // halo
    cur = lambda bi, i: (bi, i, 0)
    prev = lambda bi, i: (bi, jnp.maximum(i * r - 1, 0), 0)
    nxt = lambda bi, i: (bi, jnp.minimum(i * r + r, nh - 1), 0)
    ctx = lambda bi, i: (bi, 0, 0)
    est = 16 * tq * (tq + 2 * halo + c) * 4 + 8 * MIB
    return pl.pallas_call(
        _swa_kernel,
        out_shape=jax.ShapeDtypeStruct((b, n, 256), BF16),
        grid=(b, n // tq),
        in_specs=[pl.BlockSpec((1, tq, 256), cur),
                  pl.BlockSpec((1, halo, LANES), prev), pl.BlockSpec((1, tq, LANES), cur), pl.BlockSpec((1, halo, LANES), nxt),
                  pl.BlockSpec((1, halo, LANES), prev), pl.BlockSpec((1, tq, LANES), cur), pl.BlockSpec((1, halo, LANES), nxt),
                  pl.BlockSpec((1, c, LANES), ctx), pl.BlockSpec((1, c, LANES), ctx),
                  pl.BlockSpec((4, LANES), lambda bi, i: (0, 0))],
        out_specs=pl.BlockSpec((1, tq, 256), cur),
        compiler_params=_cparams(("arbitrary", "arbitrary"), est),
    )(q, k, k, k, v, v, v, kc, vc, sink_rows)


def _na_kernel(q_ref, kp_ref, kcur_ref, kn_ref, vp_ref, vcur_ref, vn_ref, kctx_ref, vctx_ref, bias_ref, o_ref):
    tq = q_ref.shape[1]
    m0, m1 = _half_masks()
    for pair in range(2):
        sl = slice(pair * LANES, (pair + 1) * LANES)
        qp = q_ref[0, :, sl]
        pieces = []
        for idx, (kr, vr) in enumerate(((kp_ref, vp_ref), (kcur_ref, vcur_ref), (kn_ref, vn_ref))):
            biases = tuple(bias_ref[0, 2 * pair + s, :, idx * tq:(idx + 1) * tq] for s in (0, 1))
            pieces.append(((kr[0, :, sl],) * 2, vr[0, :, sl], biases))
        pieces.append(((kctx_ref[0, :, sl],) * 2, vctx_ref[0, :, sl], (None, None)))
        o_ref[0, :, sl] = _attend_pair((qp * m0, qp * m1), pieces, None).astype(BF16)


def _na_bias_tables(rpb, rows):
    nb = rows // NA_ROWS
    heads = rpb.shape[0]
    width = 2 * NA_KW - 1
    span = 2 * GRID_W - 1
    lead = GRID_W - 2 - (NA_KW - 1)
    rpb = rpb.astype(F32)
    per_row = jnp.stack([rpb[:, NA_ROWS - 1 - a:NA_ROWS - 1 - a + 3 * NA_ROWS, :] for a in range(NA_ROWS)], axis=1)
    padded = jnp.pad(per_row, ((0, 0), (0, 0), (0, 0), (lead, span - width - lead)))
    tiled = jnp.broadcast_to(padded[:, :, :, None, :], (heads, NA_ROWS, 3 * NA_ROWS, GRID_W, span))
    skew = tiled.reshape(heads, NA_ROWS, 3 * NA_ROWS, GRID_W * span)[..., :GRID_W * (span - 1)]
    toep = skew.reshape(heads, NA_ROWS, 3 * NA_ROWS, GRID_W, span - 1)[..., GRID_W - 2:]
    table = jnp.transpose(toep, (0, 1, 3, 2, 4)).reshape(heads, NA_ROWS * GRID_W, 3 * NA_ROWS * GRID_W)
    a = np.arange(NA_ROWS)[:, None, None, None]
    cq = np.arange(GRID_W)[None, :, None, None]
    kr = np.arange(3 * NA_ROWS)[None, None, :, None]
    ck = np.arange(GRID_W)[None, None, None, :]
    valids = []
    for j in (0, 1, nb - 1):
        r = NA_ROWS * j + a
        rs = np.clip(r - NA_KH // 2, 0, rows - NA_KH)
        rk = NA_ROWS * (j - 1) + kr
        cs = np.clip(cq - NA_KW // 2, 0, GRID_W - NA_KW)
        valid = (rk >= rs) & (rk < rs + NA_KH) & (ck >= cs) & (ck < cs + NA_KW)
        valids.append(valid.reshape(NA_ROWS * GRID_W, 3 * NA_ROWS * GRID_W))
    valid = jnp.asarray(np.stack(valids))
    return jnp.where(valid[:, None], table[None], NEG_INF)


def _na_attention(q, k, v, kc, vc, bias):
    b, n, _ = q.shape
    c = kc.shape[1]
    tq = NA_ROWS * GRID_W
    nb = n // tq
    cur = lambda bi, i: (bi, i, 0)
    prev = lambda bi, i: (bi, jnp.maximum(i - 1, 0), 0)
    nxt = lambda bi, i: (bi, jnp.minimum(i + 1, nb - 1), 0)
    ctx = lambda bi, i: (bi, 0, 0)
    variant = lambda bi, i: (jnp.where(i == 0, 0, jnp.where(i == nb - 1, 2, 1)), 0, 0, 0)
    blk = pl.BlockSpec((1, tq, 256), cur)
    est = 2 * 4 * tq * 3 * tq * 4 + 16 * tq * (3 * tq + c) * 4 + 8 * MIB
    return pl.pallas_call(
        _na_kernel,
        out_shape=jax.ShapeDtypeStruct((b, n, 256), BF16),
        grid=(b, nb),
        in_specs=[blk, pl.BlockSpec((1, tq, 256), prev), blk, pl.BlockSpec((1, tq, 256), nxt),
                  pl.BlockSpec((1, tq, 256), prev), blk, pl.BlockSpec((1, tq, 256), nxt),
                  pl.BlockSpec((1, c, 256), ctx), pl.BlockSpec((1, c, 256), ctx),
                  pl.BlockSpec((1, 4, tq, 3 * tq), variant)],
        out_specs=blk,
        compiler_params=_cparams(("arbitrary", "arbitrary"), est),
    )(q, k, k, k, v, v, v, kc, vc, bias)


def _ctx_attn_kernel(mq_ref, mk_ref, mv_ref, sq_ref, sk_ref, sv_ref, sink_ref, nq_ref, nk_ref, nv_ref,
                     ya_ref, yb_ref, yc_ref):
    m0, m1 = _half_masks()
    none2 = (None, None)
    for pair in range(2):
        sl = slice(pair * LANES, (pair + 1) * LANES)
        h0 = slice(2 * pair * LANES, (2 * pair + 1) * LANES)
        h1 = slice((2 * pair + 1) * LANES, (2 * pair + 2) * LANES)
        ya_ref[0, :, sl] = _attend_pair((mq_ref[0, :, h0], mq_ref[0, :, h1]),
                                        [((mk_ref[0, :, h0], mk_ref[0, :, h1]), mv_ref[0, :, sl], none2)],
                                        None).astype(BF16)
        qg = sq_ref[0, :, sl]
        sinks = [sink_ref[2 * pair + s:2 * pair + s + 1, 0:1] for s in (0, 1)]
        yb_ref[0, :, sl] = _attend_pair((qg * m0, qg * m1), [((sk_ref[0],) * 2, sv_ref[0], none2)],
                                        sinks).astype(BF16)
        qn = nq_ref[0, :, sl]
        yc_ref[0, :, sl] = _attend_pair((qn * m0, qn * m1), [((nk_ref[0, :, sl],) * 2, nv_ref[0, :, sl], none2)],
                                        None).astype(BF16)


def _ctx_attention(mq, mk, mv, sq, sk, sv, sink_rows, nq, nk, nv):
    b, c, _ = mq.shape
    spec = lambda wd: pl.BlockSpec((1, c, wd), lambda bi: (bi, 0, 0))
    return pl.pallas_call(
        _ctx_attn_kernel,
        out_shape=[jax.ShapeDtypeStruct((b, c, 256), BF16)] * 3,
        grid=(b,),
        in_specs=[spec(512), spec(512), spec(256), spec(256), spec(LANES), spec(LANES),
                  pl.BlockSpec((4, LANES), lambda bi: (0, 0)), spec(256), spec(256), spec(256)],
        out_specs=[spec(256)] * 3,
        compiler_params=_cparams(("arbitrary",), 24 * MIB),
    )(mq, mk, mv, sq, sk, sv, sink_rows, nq, nk, nv)


def _ret_tables(dec_lane, reverse):
    c = RET_CHUNK
    lg = -(jnp.maximum(-dec_lane, 0.0) + jnp.log1p(jnp.exp(-jnp.abs(dec_lane))))
    ti = lax.broadcasted_iota(jnp.int32, (c, LANES), 0).astype(F32)
    if reverse:
        qpow, kpow = c - ti, ti
    else:
        qpow, kpow = ti + 1.0, c - 1.0 - ti
    ii = lax.broadcasted_iota(jnp.int32, (c, c), 0)
    jj = lax.broadcasted_iota(jnp.int32, (c, c), 1)
    dist = (jj - ii) if reverse else (ii - jj)
    distf = jnp.maximum(dist, 0).astype(F32)
    masks = [jnp.where(dist >= 0, jnp.exp(lg[:, s * HEAD_DIM:s * HEAD_DIM + 1] * distf), 0.0) for s in (0, 1)]
    return jnp.exp(lg * qpow), jnp.exp(lg * kpow), masks[0], masks[1], jnp.exp(lg * float(c))


def _ret_chunk(q, k, v, g, qdec, kdec, decays, cdec, gn_lane, s_ref):
    c = RET_CHUNK
    lo = _low_half((1, LANES))
    ii = lax.broadcasted_iota(jnp.int32, (c, c), 0)
    jj = lax.broadcasted_iota(jnp.int32, (c, c), 1)
    outs = []
    for msk, decay in zip(_half_masks(), decays):
        inner = _dot_nt(q * msk, k) * decay
        outs.append(_dot(inner.astype(BF16), v))
    state = s_ref[...]
    o = jnp.where(lo, outs[0], outs[1]) + _dot((q.astype(F32) * qdec).astype(BF16), state.astype(BF16))
    kd = (k.astype(F32) * kdec).astype(BF16)
    same_head = (ii < HEAD_DIM) == (jj < HEAD_DIM)
    s_ref[...] = state * cdec + jnp.where(same_head, _dot_tn(kd, v), 0.0)

    def head_mean(x):
        s_lo = jnp.sum(jnp.where(lo, x, 0.0), axis=-1, keepdims=True)
        s_hi = jnp.sum(jnp.where(lo, 0.0, x), axis=-1, keepdims=True)
        return jnp.where(lo, s_lo, s_hi) * (1.0 / HEAD_DIM)

    dev = o - head_mean(o)
    normed = dev * lax.rsqrt(head_mean(dev * dev) + EPS)
    gf = g.astype(F32)
    return (gf * jax.nn.sigmoid(gf)) * (normed * gn_lane)


def _ret_kernel(decf_ref, decb_ref, gnf_ref, gnb_ref, s0f_ref, s0b_ref,
                qf_ref, kf_ref, vf_ref, gf_ref, qb_ref, kb_ref, vb_ref, gb_ref,
                yf_ref, yb_ref, sf_ref, sb_ref, st_scr, tab_scr, cdec_scr):
    t = pl.program_id(1)
    directions = ((decf_ref, gnf_ref, qf_ref, kf_ref, vf_ref, gf_ref, yf_ref),
                  (decb_ref, gnb_ref, qb_ref, kb_ref, vb_ref, gb_ref, yb_ref))

    @pl.when(t == 0)
    def _():
        for pair in range(2):
            st_scr[0, pair] = s0f_ref[0, pair]
            st_scr[1, pair] = s0b_ref[0, pair]
            for d, refs in enumerate(directions):
                qdec, kdec, mask0, mask1, cdec = _ret_tables(refs[0][pair], d == 1)
                for idx, tab in enumerate((qdec, kdec, mask0, mask1)):
                    tab_scr[d, pair, idx] = tab
                cdec_scr[d, pair] = jnp.broadcast_to(cdec, (8, LANES))

    for d, (_, gn_ref, q_ref, k_ref, v_ref, g_ref, y_ref) in enumerate(directions):
        for pair in range(2):
            sl = slice(pair * LANES, (pair + 1) * LANES)
            y = _ret_chunk(q_ref[0, :, sl], k_ref[0, :, sl], v_ref[0, :, sl], g_ref[0, :, sl],
                           tab_scr[d, pair, 0], tab_scr[d, pair, 1], (tab_scr[d, pair, 2], tab_scr[d, pair, 3]),
                           cdec_scr[d, pair, 0:1, :], gn_ref[:, sl], st_scr.at[d, pair])
            y_ref[0, :, sl] = y.astype(BF16)

    @pl.when(t == pl.num_programs(1) - 1)
    def _():
        sf_ref[0] = st_scr[0]
        sb_ref[0] = st_scr[1]


def _retention(q, k, v, gf, gb, dec_f, dec_b, gn_f, gn_b, s0f, s0b):
    b, t, width = q.shape
    c = RET_CHUNK
    nt = t // c
    fwd = pl.BlockSpec((1, c, width), lambda bi, i: (bi, i, 0))
    bwd = pl.BlockSpec((1, c, width), lambda bi, i: (bi, nt - 1 - i, 0))
    dec = pl.BlockSpec((2, 1, LANES), lambda bi, i: (0, 0, 0))
    gn = pl.BlockSpec((1, width), lambda bi, i: (0, 0))
    st = pl.BlockSpec((1, 2, LANES, LANES), lambda bi, i: (bi, 0, 0, 0))
    return pl.pallas_call(
        _ret_kernel,
        out_shape=[jax.ShapeDtypeStruct((b, t, width), BF16)] * 2 + [jax.ShapeDtypeStruct((b, 2, LANES, LANES), F32)] * 2,
        grid=(b, nt),
        in_specs=[dec, dec, gn, gn, st, st, fwd, fwd, fwd, fwd, bwd, bwd, bwd, bwd],
        out_specs=[fwd, bwd, st, st],
        scratch_shapes=[pltpu.VMEM((2, 2, LANES, LANES), F32), pltpu.VMEM((2, 2, 4, c, LANES), F32),
                        pltpu.VMEM((2, 2, 8, LANES), F32)],
        compiler_params=_cparams(("arbitrary", "arbitrary"), 24 * MIB),
    )(dec_f, dec_b, gn_f, gn_b, s0f, s0b, q, k, v, gf, q, k, v, gb)


def _outproj_kernel(x_ref, ya_ref, yb_ref, yc_ref, ydf_ref, ydb_ref, beta_ref, w_ref, gate_ref, o_ref):
    ys = (ya_ref[0].astype(F32), yb_ref[0].astype(F32), yc_ref[0].astype(F32),
          ydf_ref[0].astype(F32) + ydb_ref[0].astype(F32))
    acc = 0.0
    for s, y in enumerate(ys):
        sl = slice(s * 256, (s + 1) * 256)
        acc = acc + _dot((y * beta_ref[:, sl]).astype(BF16), w_ref[sl, :])
    o_ref[0] = x_ref[0] + gate_ref[0] * acc


def _out_projection(x, ya, yb, yc, ydf, ydb, beta, w, gate):
    b, t, d = x.shape
    tm = min(512, t)
    row = lambda wd: pl.BlockSpec((1, tm, wd), lambda bi, i: (bi, i, 0))
    est = 2 * (2 * tm * d * 4 + 5 * tm * 256 * 2 + d * d * 2) + 8 * MIB
    return pl.pallas_call(
        _outproj_kernel,
        out_shape=jax.ShapeDtypeStruct((b, t, d), F32),
        grid=(b, t // tm),
        in_specs=[row(d), row(256), row(256), row(256), row(256), row(256),
                  pl.BlockSpec((1, d), lambda bi, i: (0, 0)), pl.BlockSpec((d, d), lambda bi, i: (0, 0)),
                  pl.BlockSpec((1, 1, d), lambda bi, i: (bi, 0, 0))],
        out_specs=row(d),
        compiler_params=_cparams(("arbitrary", "arbitrary"), est),
    )(x, ya, yb, yc, ydf, ydb, beta, w, gate)


def _sorting_network(n):
    size = 1
    while size < n:
        size *= 2
    pairs = []
    p = 1
    while p < size:
        k = p
        while k >= 1:
            for j in range(k % p, size - k, 2 * k):
                for i in range(min(k, size - j - k)):
                    if (i + j) // (2 * p) == (i + j + k) // (2 * p):
                        pairs.append((i + j, i + j + k))
            k //= 2
        p *= 2
    return [(i, j) for i, j in pairs if j < n]


def _top_values(scores, out_ref, count):
    groups = scores.shape[0] // 8
    lists = [scores[8 * g:8 * g + 8, :] for g in range(groups)]
    for i, j in _sorting_network(groups):
        lists[i], lists[j] = jnp.maximum(lists[i], lists[j]), jnp.minimum(lists[i], lists[j])
    for r in range(count):
        m = jnp.max(lists[0], axis=0, keepdims=True)
        out_ref[r:r + 1, :] = m
        hit = lists[0] == m
        need = count - r - 1
        for p in range(min(groups - 1, need)):
            lists[p] = jnp.where(hit, lists[p + 1], lists[p])
        if need >= groups:
            lists[groups - 1] = jnp.where(hit, -jnp.inf, lists[groups - 1])


def _staircase(w1, w2):
    row8 = lax.broadcasted_iota(jnp.int32, (8, w1.shape[1]), 0)
    cands = [w1[0:1] + w2]
    for a in range(1, 8):
        cands.append(jnp.where(row8 < PEER_TOPK // (a + 1), w1[a:a + 1] + w2[0:8], -jnp.inf))
    cands.append(w1[8:16] + w2[0:1])
    return jnp.concatenate(cands, axis=0)


def _peer_route(hd, q_scr, k1_ref, k2_ref, rk_scr, e2_scr, n_scr, e1_scr, v1_scr, v2_scr, top_scr):
    half = PEER_NKEYS
    base = pl.multiple_of(hd * 2 * half, 2 * half)
    s1 = _dot(k1_ref[...], q_scr[pl.ds(base, half), :].astype(BF16))
    s2 = _dot(k2_ref[...], q_scr[pl.ds(base + half, half), :].astype(BF16))
    _top_values(s1, v1_scr, PEER_TOPK)
    _top_values(s2, v2_scr, PEER_TOPK)
    v1, v2 = v1_scr[...], v2_scr[...]
    cand = _staircase(v1, v2)
    _top_values(cand, top_scr, PEER_TOPK)
    top = top_scr[...]
    z = jnp.sum(jnp.exp(top - top[0:1]), axis=0, keepdims=True)
    sel = jnp.where(cand >= top[PEER_TOPK - 1:PEER_TOPK], 1.0, 0.0)
    count = lambda picked: jnp.sum(picked, axis=0, keepdims=True)
    lens = [count(sel[0:16])] + [count(sel[8 + 8 * a:16 + 8 * a]) for a in range(1, 8)]
    lens += [sel[72 + a:73 + a] for a in range(8)]
    n_sel = jnp.zeros(s1.shape, F32)
    for a in reversed(range(PEER_TOPK)):
        n_sel = jnp.where(s1 >= v1[a:a + 1], lens[a], n_sel)
    rank = jnp.zeros(s2.shape, F32)
    for b in range(PEER_TOPK):
        rank = rank + jnp.where(s2 < v2[b:b + 1], 1.0, 0.0)
    rk_scr[hd] = rank.astype(BF16)
    n_scr[hd] = n_sel
    e1_scr[hd] = jnp.exp(s1 - v1[0:1]) / z
    e2_scr[hd] = (jnp.exp(s2 - v2[0:1]) * 0.5).astype(BF16)


def _peer_kernel(*refs, final_norm):
    if final_norm:
        (x_ref, sh_ref, sc_ref, gt_ref, g_ref, wqt_ref, k1_ref, k2_ref, u_ref, vt_ref, fg_ref, o_ref,
         ht_scr, q_scr, rk_scr, e2_scr, n_scr, e1_scr, v1_scr, v2_scr, top_scr,
         a0_scr, a1_scr, acc_scr) = refs
    else:
        (x_ref, sh_ref, sc_ref, gt_ref, g_ref, wqt_ref, k1_ref, k2_ref, u_ref, vt_ref, o_ref,
         ht_scr, q_scr, rk_scr, e2_scr, n_scr, e1_scr, v1_scr, v2_scr, top_scr,
         a0_scr, a1_scr, acc_scr) = refs
        fg_ref = None
    step = pl.program_id(2)
    tokens = ht_scr.shape[1]

    @pl.when(step == 0)
    def _():
        h = _norm_mod(x_ref[0], g_ref[...], sh_ref[0], sc_ref[0])
        ht = jnp.transpose(h).astype(BF16)
        ht_scr[...] = ht
        q_scr[...] = _dot(wqt_ref[...], ht)
        acc_scr[...] = jnp.zeros(acc_scr.shape, F32)

        def route(hd, carry):
            _peer_route(hd, q_scr, k1_ref, k2_ref, rk_scr, e2_scr, n_scr, e1_scr, v1_scr, v2_scr, top_scr)
            return carry

        lax.fori_loop(0, PEER_HEADS, route, 0)

    def project(a_write):
        a_write[...] = _dot(u_ref[...], ht_scr[...])

    def consume(a_read):
        y = None
        piece = 2 * PEER_NKEYS
        for kt in range(PEER_ROWS // 2):
            ws = []
            for r in (2 * kt, 2 * kt + 1):
                rows = slice(r * PEER_NKEYS, (r + 1) * PEER_NKEYS)
                i = (step - 1) * PEER_ROWS + r
                gate = jnp.zeros((PEER_NKEYS, tokens), BF16)
                for hd in range(PEER_HEADS):
                    n_row = jnp.broadcast_to(n_scr[hd, pl.ds(i, 1), :].astype(BF16), gate.shape)
                    e1_row = jnp.broadcast_to(e1_scr[hd, pl.ds(i, 1), :].astype(BF16), gate.shape)
                    gate = gate + jnp.where(rk_scr[hd] < n_row, e2_scr[hd] * e1_row, jnp.zeros_like(gate))
                a = a_read[rows, :]
                gelu2 = a * (1.0 + lax.erf(a * np.float32(np.sqrt(0.5))))
                ws.append(gelu2.astype(BF16) * gate)
            part = _dot(vt_ref[:, kt * piece:(kt + 1) * piece], jnp.concatenate(ws, axis=0))
            y = part if y is None else y + part
        acc_scr[...] += y

    last = pl.num_programs(2) - 1
    bufs = (a0_scr, a1_scr)

    @pl.when(step == 0)
    def _():
        project(a0_scr)

    for parity in (0, 1):
        @pl.when(jnp.logical_and(jnp.logical_and(step > 0, step < last), step % 2 == parity))
        def _():
            project(bufs[parity])
            consume(bufs[1 - parity])

        @pl.when(jnp.logical_and(step == last, step % 2 == parity))
        def _():
            consume(bufs[1 - parity])

    @pl.when(step == pl.num_programs(2) - 1)
    def _():
        out = x_ref[0] + gt_ref[0] * jnp.transpose(acc_scr[...])
        if final_norm:
            out = _rmsnorm(out, fg_ref[...])
        o_ref[0] = out


def _peer_ffn(x, shift, scale, gate, g, wqt, k1, k2, u, vt, final_g=None):
    b, t, d = x.shape
    tt = PEER_TOKENS
    ec = PEER_ROWS * PEER_NKEYS
    n_blocks = u.shape[0] // ec
    nq = wqt.shape[0]
    final_norm = final_g is not None
    tok = pl.BlockSpec((1, tt, d), lambda bi, i, s: (bi, i, 0))
    vec = pl.BlockSpec((1, 1, d), lambda bi, i, s: (bi, 0, 0))
    const = lambda shape: pl.BlockSpec(shape, lambda bi, i, s: (0,) * len(shape))
    in_specs = [tok, vec, vec, vec, const((1, d)), const((nq, d)), const(k1.shape), const(k2.shape),
                pl.BlockSpec((ec, d), lambda bi, i, s: (jnp.minimum(s, n_blocks - 1), 0)),
                pl.BlockSpec((d, ec), lambda bi, i, s: (0, jnp.clip(s - 1, 0, n_blocks - 1)))]
    args = [x, shift, scale, gate, g, wqt, k1, k2, u, vt]
    if final_norm:
        in_specs.append(const((1, d)))
        args.append(final_g)
    table = pltpu.VMEM((PEER_HEADS, PEER_NKEYS, tt), F32)
    table16 = pltpu.VMEM((PEER_HEADS, PEER_NKEYS, tt), BF16)
    top = pltpu.VMEM((PEER_TOPK, tt), F32)
    a_buf = pltpu.VMEM((ec, tt), F32)
    scratch = [pltpu.VMEM((d, tt), BF16), pltpu.VMEM((nq, tt), F32), table16, table16, table, table,
               top, top, top, a_buf, a_buf, pltpu.VMEM((d, tt), F32)]
    est = (4 * tt * d * 4 + 2 * nq * d * 2 + 4 * ec * d * 2 + tt * d * 2 + nq * tt * 4
           + 4 * PEER_HEADS * PEER_NKEYS * tt * 4 + d * tt * 4 + 2 * ec * tt * 6 + 8 * MIB)
    return pl.pallas_call(
        functools.partial(_peer_kernel, final_norm=final_norm),
        out_shape=jax.ShapeDtypeStruct((b, t, d), F32),
        grid=(b, t // tt, n_blocks + 1),
        in_specs=in_specs,
        out_specs=tok,
        scratch_shapes=scratch,
        compiler_params=_cparams(("arbitrary", "arbitrary", "arbitrary"), est),
    )(*args)


def _prep_in_weights(w_in):
    d = w_in.shape[0]
    zeros = lambda n: jnp.zeros((d, n), w_in.dtype)
    cols = []
    for hd in range(4):
        cols += [w_in[:, hd * MLA_QK:(hd + 1) * MLA_QK], zeros(LANES - MLA_QK)]
    cols.append(w_in[:, 384:512])
    cols += [zeros(HEAD_DIM), w_in[:, 512:544], zeros(LANES - HEAD_DIM - MLA_ROPE)]
    for hd in (0, 2, 1, 3):
        cols.append(w_in[:, 544 + hd * HEAD_DIM:544 + (hd + 1) * HEAD_DIM])
    cols.append(w_in[:, 800:])
    return jnp.concatenate(cols, axis=1).astype(BF16)


def _prep_kv_weights(w_ukv):
    zeros = jnp.zeros((MLA_RANK, HEAD_DIM), w_ukv.dtype)
    wkn = jnp.concatenate([blk for hd in range(4) for blk in (w_ukv[:, hd * LANES:hd * LANES + HEAD_DIM], zeros)], axis=1)
    wv = jnp.concatenate([w_ukv[:, hd * LANES + HEAD_DIM:(hd + 1) * LANES] for hd in range(4)], axis=1)
    sel = np.zeros((LANES, 4 * LANES), np.float32)
    for hd in range(4):
        for r in range(MLA_ROPE):
            sel[HEAD_DIM + r, hd * LANES + HEAD_DIM + r] = 1.0
    return wkn.astype(BF16), jnp.asarray(sel, BF16), wv.astype(BF16)


def _rope_lane_tables(n):
    def tables(rot_dim):
        quarter = rot_dim // 4
        inv = ROPE_BASE ** (-jnp.arange(quarter, dtype=F32) / quarter)
        t = jnp.arange(n, dtype=jnp.int32)
        pos = jnp.stack([t // GRID_W, t % GRID_W], axis=-1).astype(F32)
        ang = pos[:, :, None] * inv
        lanes = lambda tb: jnp.concatenate([tb[:, 0], tb[:, 0], tb[:, 1], tb[:, 1]], axis=-1)
        return lanes(jnp.cos(ang)), lanes(jnp.sin(ang))

    c32, s32 = tables(MLA_ROPE)
    c64, s64 = tables(HEAD_DIM)
    ones = lambda w: jnp.ones((n, w), F32)
    zeros = lambda w: jnp.zeros((n, w), F32)
    return (jnp.concatenate([ones(HEAD_DIM), c32, ones(LANES - HEAD_DIM - MLA_ROPE)], axis=-1),
            jnp.concatenate([zeros(HEAD_DIM), s32, zeros(LANES - HEAD_DIM - MLA_ROPE)], axis=-1),
            jnp.concatenate([c64, c64], axis=-1), jnp.concatenate([s64, s64], axis=-1))


_SWA_SLOT_HEADS = (0, 2, 1, 3)


def kernel(x, c, ctx, c_ctx, ada_w, ada_b, norm_mix_g, w_in, mla_kv_norm_g, mla_w_ukv, swa_sink, na_rpb, ret_decay_f, ret_decay_b, ret_gn_f, ret_gn_b, mix_beta, w_out, norm_ffn_g, peer_wq, peer_k1, peer_k2, peer_u, peer_v, final_norm_g):
    b, n, d = x.shape
    n_ctx = ctx.shape[1]
    depth = ada_w.shape[0]
    rows = n // GRID_W

    pad_rows = -(b + 1) % 8
    cc = jnp.concatenate([c, c_ctx[None, :], jnp.zeros((pad_rows, d), F32)], axis=0)
    mod = _ada_modulation(cc, ada_w, ada_b)

    lat_tabs = _rope_lane_tables(n)
    ctx_tabs = (jnp.ones((n_ctx, LANES), F32), jnp.zeros((n_ctx, LANES), F32)) * 2
    slot_heads = np.asarray(_SWA_SLOT_HEADS)
    out_perm = np.arange(d)
    out_perm[256:512] = 256 + (slot_heads[:, None] * HEAD_DIM + np.arange(HEAD_DIM)[None, :]).reshape(-1)
    zero_state = jnp.zeros((b, 2, LANES, LANES), F32)
    pair_lanes = lambda p: jnp.repeat(p.astype(F32), HEAD_DIM).reshape(2, 1, LANES)

    xc = ctx
    for layer in range(depth):
        with_ctx = layer < depth - 1
        last = layer == depth - 1
        chunk = lambda k, lo, hi: mod[layer, lo:hi, k * d:(k + 1) * d][:, None, :]
        mod_l = [chunk(k, 0, b) for k in range(6)]
        mod_c = [jnp.broadcast_to(chunk(k, b, b + 1), (b, 1, d)) for k in range(6)]

        w_cols = _prep_in_weights(w_in[layer])
        wkn, emat, wv = _prep_kv_weights(mla_w_ukv[layer])
        g_mix = norm_mix_g[layer][None, :]
        kvg = mla_kv_norm_g[layer][None, :]
        p_l = _in_projection(x, mod_l[0], mod_l[1], g_mix, w_cols, kvg, wkn, emat, wv, lat_tabs)
        p_c = _in_projection(xc, mod_c[0], mod_c[1], g_mix, w_cols, kvg, wkn, emat, wv, ctx_tabs)
        mq, mk, mv, sq, sk, sv, nq, nk, nv, rq, rk, rv, gf, gb = p_l
        cmq, cmk, cmv, csq, csk, csv, cnq, cnk, cnv, crq, crk, crv, cgf, cgb = p_c

        sink_rows = jnp.broadcast_to(swa_sink[layer].astype(F32)[slot_heads][:, None], (4, LANES))
        bias = _na_bias_tables(na_rpb[layer], rows)
        dec_f, dec_b = pair_lanes(ret_decay_f[layer]), pair_lanes(ret_decay_b[layer])
        gn_f, gn_b = ret_gn_f[layer][None, :], ret_gn_b[layer][None, :]

        ya = _mla_attention(mq, cmk, cmv, mk, mv)
        yb = _swa_attention(sq, sk, sv, csk, csv, sink_rows)
        yc = _na_attention(nq, nk, nv, cnk, cnv, bias)
        cyf, cyb, s_f, s_b = _retention(crq, crk, crv, cgf, cgb, dec_f, dec_b, gn_f, gn_b, zero_state, zero_state)
        ydf, ydb, _, _ = _retention(rq, rk, rv, gf, gb, dec_f, dec_b, gn_f, gn_b, s_f, s_b)

        beta = mix_beta[layer][out_perm][None, :]
        w_o = w_out[layer][out_perm, :].astype(BF16)
        x = _out_projection(x, ya, yb, yc, ydf, ydb, beta, w_o, mod_l[2])

        g_ffn = norm_ffn_g[layer][None, :]
        wqt = peer_wq[layer].T.astype(BF16)
        k1, k2 = peer_k1[layer].astype(BF16), peer_k2[layer].astype(BF16)
        u, vt = peer_u[layer].astype(BF16), peer_v[layer].T.astype(BF16)
        x = _peer_ffn(x, mod_l[3], mod_l[4], mod_l[5], g_ffn, wqt, k1, k2, u, vt,
                      final_norm_g[None, :] if last else None)
        if with_ctx:
            cya, cyb_, cyc = _ctx_attention(cmq, cmk, cmv, csq, csk, csv, sink_rows, cnq, cnk, cnv)
            xc = _out_projection(xc, cya, cyb_, cyc, cyf, cyb, beta, w_o, mod_c[2])
            xc = _peer_ffn(xc, mod_c[3], mod_c[4], mod_c[5], g_ffn, wqt, k1, k2, u, vt)
    return x
```

```python
import functools

import numpy as np
import jax
import jax.numpy as jnp
from jax import lax
from jax.experimental import pallas as pl
from jax.experimental.pallas import tpu as pltpu

F32 = jnp.float32
BF16 = jnp.bfloat16

EPS = 1e-6
NEG_INF = -1e30
ROPE_BASE = 10000.0
GRID_W = 64
HEAD_DIM = 64
LANES = 128
V7X_VMEM_BYTES = 64 * 1024 * 1024
MIB = 1024 * 1024

MLA_ROPE = 32
MLA_QK = 96
MLA_RANK = 128
SWA_WINDOW = 128
NA_KH, NA_KW = 8, 16
NA_ROWS = 4
RET_CHUNK = 128
PEER_HEADS = 8
PEER_NKEYS = 128
PEER_TOPK = 16
PEER_TOKENS = 256
PEER_ROWS = 16

_MQ, _CKV, _KR, _SQ, _SK, _SV, _NQ, _NK, _NV, _RQ, _RK, _RV, _GF, _GB, _IN_COLS = (
    0, 512, 640, 768, 1024, 1152, 1280, 1536, 1792, 2048, 2304, 2560, 2816, 3072, 3328)


def _cparams(semantics, vmem_bytes):
    limit = int(min(vmem_bytes, V7X_VMEM_BYTES * 7 // 8))
    return pltpu.CompilerParams(dimension_semantics=semantics, vmem_limit_bytes=limit)


def _dot(a, b):
    return jnp.dot(a, b, preferred_element_type=F32)


def _dot_nt(a, b):
    return lax.dot_general(a, b, (((1,), (1,)), ((), ())), preferred_element_type=F32)


def _dot_tn(a, b):
    return lax.dot_general(a, b, (((0,), (0,)), ((), ())), preferred_element_type=F32)


def _low_half(shape):
    return lax.broadcasted_iota(jnp.int32, shape, len(shape) - 1) < HEAD_DIM


def _half_masks():
    lo = _low_half((1, LANES))
    return (jnp.where(lo, 1.0, 0.0).astype(BF16), jnp.where(lo, 0.0, 1.0).astype(BF16))


def _norm_mod(x, g, shift, scale):
    ms = jnp.mean(x * x, axis=-1, keepdims=True)
    return (x * lax.rsqrt(ms + EPS) * g) * (1.0 + scale) + shift


def _rmsnorm(x, g):
    ms = jnp.mean(x * x, axis=-1, keepdims=True)
    return x * lax.rsqrt(ms + EPS) * g


def _ada_kernel(c_ref, w_ref, b_ref, o_ref):
    c = c_ref[...]
    o_ref[0] = _dot(c * jax.nn.sigmoid(c), w_ref[0]) + b_ref[0]


def _ada_modulation(cc, ada_w, ada_b):
    depth, d, width = ada_w.shape
    rows = cc.shape[0]
    tn = 1024
    return pl.pallas_call(
        _ada_kernel,
        out_shape=jax.ShapeDtypeStruct((depth, rows, width), F32),
        grid=(depth, width // tn),
        in_specs=[pl.BlockSpec((rows, d), lambda l, j: (0, 0)),
                  pl.BlockSpec((1, d, tn), lambda l, j: (l, 0, j)),
                  pl.BlockSpec((1, 1, tn), lambda l, j: (l, 0, j))],
        out_specs=pl.BlockSpec((1, rows, tn), lambda l, j: (l, 0, j)),
        compiler_params=_cparams(("arbitrary", "arbitrary"), 2 * d * tn * 4 + 8 * MIB),
    )(cc, ada_w, ada_b.reshape(depth, 1, width))


def _rope(a, cos, sin, half):
    lane = lax.broadcasted_iota(jnp.int32, a.shape, 1)
    first = (lane % (2 * half)) < half
    rot = jnp.where(first, -pltpu.roll(a, LANES - half, 1), pltpu.roll(a, half, 1))
    return a * cos + rot * sin


def _inproj_kernel(x_ref, sh_ref, sc_ref, g_ref, w_ref, kvg_ref, wkn_ref, e_ref, wv_ref,
                   c32_ref, s32_ref, c64_ref, s64_ref,
                   mq_ref, mk_ref, mv_ref, sq_ref, sk_ref, sv_ref, nq_ref, nk_ref, nv_ref,
                   rq_ref, rk_ref, rv_ref, gf_ref, gb_ref):
    h = _norm_mod(x_ref[0], g_ref[...], sh_ref[0], sc_ref[0]).astype(BF16)

    def seg(lo, width):
        return _dot(h, w_ref[:, lo:lo + width])

    c32, s32, c64, s64 = c32_ref[...], s32_ref[...], c64_ref[...], s64_ref[...]
    mla_scale = MLA_QK ** -0.5
    head_scale = HEAD_DIM ** -0.5
    for hd in range(4):
        a = seg(_MQ + hd * LANES, LANES)
        mq_ref[0, :, hd * LANES:(hd + 1) * LANES] = (_rope(a, c32, s32, MLA_ROPE // 4) * mla_scale).astype(BF16)
    kvn = _rmsnorm(seg(_CKV, MLA_RANK), kvg_ref[...]).astype(BF16)
    kr = _rope(seg(_KR, LANES), c32, s32, MLA_ROPE // 4).astype(BF16)
    mk_ref[0] = (_dot(kvn, wkn_ref[...]) + _dot(kr, e_ref[...])).astype(BF16)
    mv_ref[0] = _dot(kvn, wv_ref[...]).astype(BF16)
    for grp in range(2):
        a = seg(_SQ + grp * LANES, LANES)
        sq_ref[0, :, grp * LANES:(grp + 1) * LANES] = (_rope(a, c64, s64, HEAD_DIM // 4) * head_scale).astype(BF16)
    sk_ref[0] = _rope(seg(_SK, LANES), c64, s64, HEAD_DIM // 4).astype(BF16)
    sv_ref[0] = seg(_SV, LANES).astype(BF16)
    nq_ref[0] = (seg(_NQ, 256) * head_scale).astype(BF16)
    nk_ref[0] = seg(_NK, 256).astype(BF16)
    nv_ref[0] = seg(_NV, 256).astype(BF16)
    rq_ref[0] = seg(_RQ, 256).astype(BF16)
    rk_ref[0] = (seg(_RK, 256) * head_scale).astype(BF16)
    rv_ref[0] = seg(_RV, 256).astype(BF16)
    gf_ref[0] = seg(_GF, 256).astype(BF16)
    gb_ref[0] = seg(_GB, 256).astype(BF16)


_INPROJ_WIDTHS = (512, 512, 256, 256, 128, 128, 256, 256, 256, 256, 256, 256, 256, 256)


def _in_projection(x, shift, scale, g, w, kvg, wkn, emat, wv, tabs):
    b, t, d = x.shape
    tm = min(512, t)
    const = lambda shape: pl.BlockSpec(shape, lambda bi, i: (0,) * len(shape))
    tab = pl.BlockSpec((tm, LANES), lambda bi, i: (i, 0))
    vec = pl.BlockSpec((1, 1, d), lambda bi, i: (bi, 0, 0))
    est = 2 * (tm * d * 4 + d * _IN_COLS * 2 + tm * sum(_INPROJ_WIDTHS) * 2 + 4 * tm * LANES * 4) + 12 * MIB
    return pl.pallas_call(
        _inproj_kernel,
        out_shape=[jax.ShapeDtypeStruct((b, t, wd), BF16) for wd in _INPROJ_WIDTHS],
        grid=(b, t // tm),
        in_specs=[pl.BlockSpec((1, tm, d), lambda bi, i: (bi, i, 0)), vec, vec, const((1, d)),
                  const((d, _IN_COLS)), const((1, MLA_RANK)), const((MLA_RANK, 512)), const((LANES, 512)),
                  const((MLA_RANK, 256)), tab, tab, tab, tab],
        out_specs=[pl.BlockSpec((1, tm, wd), lambda bi, i: (bi, i, 0)) for wd in _INPROJ_WIDTHS],
        compiler_params=_cparams(("arbitrary", "arbitrary"), est),
    )(x, shift, scale, g, w, kvg, wkn, emat, wv, *tabs)


def _attend_pair(qs, pieces, sinks):
    lo = _low_half((1, LANES))
    m0, m1 = _half_masks()
    outs = []
    for s in (0, 1):
        own, other = (m0, m1) if s == 0 else (m1, m0)
        scores = []
        for ks, _, biases in pieces:
            sc = _dot_nt(qs[s], ks[s])
            if biases[s] is not None:
                sc = sc + biases[s]
            scores.append(sc)
        m = functools.reduce(jnp.maximum, [jnp.max(sc, axis=-1, keepdims=True) for sc in scores])
        if sinks is not None:
            m = jnp.maximum(m, sinks[s])
        o = 0.0
        for sc, (_, v, _) in zip(scores, pieces):
            o = o + _dot(jnp.exp(sc - m).astype(BF16), v * own + other)
        own_lanes = lo if s == 0 else jnp.logical_not(lo)
        den = jnp.where(own_lanes, 1.0, o)
        if sinks is not None:
            den = den + jnp.exp(sinks[s] - m)
        outs.append(o * pltpu.roll(1.0 / den, HEAD_DIM, 1))
    return jnp.where(lo, outs[0], outs[1])


def _mla_kernel(q_ref, kc_ref, vc_ref, kl_ref, vl_ref, o_ref, m_ref, l_ref, acc_ref):
    j = pl.program_id(2)

    @pl.when(j == 0)
    def _():
        m_ref[...] = jnp.full(m_ref.shape, NEG_INF, F32)
        l_ref[...] = jnp.zeros(l_ref.shape, F32)
        acc_ref[...] = jnp.zeros(acc_ref.shape, F32)

    def update(k, v):
        lo = _low_half((1, LANES))
        m0, m1 = _half_masks()
        reps = k.shape[0] // LANES
        for pair in range(2):
            vp = v[:, pair * LANES:(pair + 1) * LANES]
            vals = (vp * m0 + m1, vp * m1 + m0)
            alphas, pvs = [], []
            for s in (0, 1):
                hd = 2 * pair + s
                q = q_ref[0, :, hd * LANES:(hd + 1) * LANES]
                sc = _dot_nt(q, k[:, hd * LANES:(hd + 1) * LANES])
                m_prev = m_ref[hd]
                m_new = jnp.maximum(m_prev, jnp.max(sc, axis=-1, keepdims=True))
                m_ref[hd] = m_new
                alphas.append(jnp.exp(m_prev - m_new))
                p = jnp.exp(sc - jnp.tile(m_new, (1, reps)))
                pvs.append(_dot(p.astype(BF16), vals[s]))
            acc_ref[pair] = (acc_ref[pair] * jnp.where(lo, alphas[0], alphas[1])
                             + jnp.where(lo, pvs[0], pvs[1]))
            l_ref[pair] = (l_ref[pair] * jnp.where(lo, alphas[1], alphas[0])
                           + jnp.where(lo, pvs[1], pvs[0]))

    @pl.when(j == 0)
    def _():
        update(kc_ref[0], vc_ref[0])

    @pl.when(j > 0)
    def _():
        update(kl_ref[0], vl_ref[0])

    @pl.when(j == pl.num_programs(2) - 1)
    def _():
        for pair in range(2):
            inv = pltpu.roll(1.0 / l_ref[pair], HEAD_DIM, 1)
            o_ref[0, :, pair * LANES:(pair + 1) * LANES] = (acc_ref[pair] * inv).astype(BF16)


def _mla_attention(q, kc, vc, kl, vl):
    b, n, _ = q.shape
    c = kc.shape[1]
    tq = min(1024, n)
    tk = min(1024, n)
    est = 2 * (tq * 512 * 2 + c * 768 * 2 + tk * 768 * 2 + tq * 256 * 2) + 6 * tq * LANES * 4 + 10 * tq * tk * 4 + 8 * MIB
    return pl.pallas_call(
        _mla_kernel,
        out_shape=jax.ShapeDtypeStruct((b, n, 256), BF16),
        grid=(b, n // tq, 1 + n // tk),
        in_specs=[pl.BlockSpec((1, tq, 512), lambda bi, i, j: (bi, i, 0)),
                  pl.BlockSpec((1, c, 512), lambda bi, i, j: (bi, 0, 0)),
                  pl.BlockSpec((1, c, 256), lambda bi, i, j: (bi, 0, 0)),
                  pl.BlockSpec((1, tk, 512), lambda bi, i, j: (bi, jnp.maximum(j - 1, 0), 0)),
                  pl.BlockSpec((1, tk, 256), lambda bi, i, j: (bi, jnp.maximum(j - 1, 0), 0))],
        out_specs=pl.BlockSpec((1, tq, 256), lambda bi, i, j: (bi, i, 0)),
        scratch_shapes=[pltpu.VMEM((4, tq, LANES), F32), pltpu.VMEM((2, tq, LANES), F32), pltpu.VMEM((2, tq, LANES), F32)],
        compiler_params=_cparams(("arbitrary", "arbitrary", "arbitrary"), est),
    )(q, kc, vc, kl, vl)


def _swa_kernel(q_ref, kp_ref, kcur_ref, kn_ref, vp_ref, vcur_ref, vn_ref, kctx_ref, vctx_ref, sink_ref, o_ref):
    i = pl.program_id(1)
    last = pl.num_programs(1) - 1
    tq = q_ref.shape[1]

    def band_bias(width, offset, edge_penalty):
        r = lax.broadcasted_iota(jnp.int32, (tq, width), 0)
        cidx = lax.broadcasted_iota(jnp.int32, (tq, width), 1)
        rel = cidx + offset - r
        inside = jnp.where(rel >= -SWA_WINDOW, jnp.where(rel <= SWA_WINDOW, 1, 0), 0)
        return jnp.where(inside == 1, edge_penalty, NEG_INF)

    halo = kp_ref.shape[1]
    b_prev = band_bias(halo, -halo, jnp.where(i > 0, 0.0, NEG_INF))
    b_cur = band_bias(tq, 0, 0.0)
    b_next = band_bias(halo, tq, jnp.where(i < last, 0.0, NEG_INF))
    m0, m1 = _half_masks()
    kctx, vctx = kctx_ref[0], vctx_ref[0]
    pieces = [((kp_ref[0],) * 2, vp_ref[0], (b_prev,) * 2),
              ((kcur_ref[0],) * 2, vcur_ref[0], (b_cur,) * 2),
              ((kn_ref[0],) * 2, vn_ref[0], (b_next,) * 2),
              ((kctx,) * 2, vctx, (None, None))]
    for grp in range(2):
        qg = q_ref[0, :, grp * LANES:(grp + 1) * LANES]
        sinks = [sink_ref[2 * grp + s:2 * grp + s + 1, 0:1] for s in (0, 1)]
        o_ref[0, :, grp * LANES:(grp + 1) * LANES] = _attend_pair((qg * m0, qg * m1), pieces, sinks).astype(BF16)


def _swa_attention(q, k, v, kc, vc, sink_rows):
    b, n, _ = q.shape
    c = kc.shape[1]
    tq = 256
    halo = SWA_WINDOW
    r = tq // halo
    nh = n // halo
    cur = lambda bi, i: (bi, i, 0)
    prev = lambda bi, i: (bi, jnp.maximum(i * r - 1, 0), 0)
    nxt = lambda bi, i: (bi, jnp.minimum(i * r + r, nh - 1), 0)
    ctx = lambda bi, i: (bi, 0, 0)
    est = 16 * tq * (tq + 2 * halo + c) * 4 + 8 * MIB
    return pl.pallas_call(
        _swa_kernel,
        out_shape=jax.ShapeDtypeStruct((b, n, 256), BF16),
        grid=(b, n // tq),
        in_specs=[pl.BlockSpec((1, tq, 256), cur),
                  pl.BlockSpec((1, halo, LANES), prev), pl.BlockSpec((1, tq, LANES), cur), pl.BlockSpec((1, halo, LANES), nxt),
                  pl.BlockSpec((1, halo, LANES), prev), pl.BlockSpec((1, tq, LANES), cur), pl.BlockSpec((1, halo, LANES), nxt),
                  pl.BlockSpec((1, c, LANES), ctx), pl.BlockSpec((1, c, LANES), ctx),
                  pl.BlockSpec((4, LANES), lambda bi, i: (0, 0))],
        out_specs=pl.BlockSpec((1, tq, 256), cur),
        compiler_params=_cparams(("arbitrary", "arbitrary"), est),
    )(q, k, k, k, v, v, v, kc, vc, sink_rows)


def _na_kernel(q_ref, kp_ref, kcur_ref, kn_ref, vp_ref, vcur_ref, vn_ref, kctx_ref, vctx_ref, bias_ref, o_ref):
    tq = q_ref.shape[1]
    m0, m1 = _half_masks()
    for pair in range(2):
        sl = slice(pair * LANES, (pair + 1) * LANES)
        qp = q_ref[0, :, sl]
        pieces = []
        for idx, (kr, vr) in enumerate(((kp_ref, vp_ref), (kcur_ref, vcur_ref), (kn_ref, vn_ref))):
            biases = tuple(bias_ref[0, 2 * pair + s, :, idx * tq:(idx + 1) * tq] for s in (0, 1))
            pieces.append(((kr[0, :, sl],) * 2, vr[0, :, sl], biases))
        pieces.append(((kctx_ref[0, :, sl],) * 2, vctx_ref[0, :, sl], (None, None)))
        o_ref[0, :, sl] = _attend_pair((qp * m0, qp * m1), pieces, None).astype(BF16)


def _na_bias_tables(rpb, rows):
    nb = rows // NA_ROWS
    heads = rpb.shape[0]
    width = 2 * NA_KW - 1
    span = 2 * GRID_W - 1
    lead = GRID_W - 2 - (NA_KW - 1)
    rpb = rpb.astype(F32)
    per_row = jnp.stack([rpb[:, NA_ROWS - 1 - a:NA_ROWS - 1 - a + 3 * NA_ROWS, :] for a in range(NA_ROWS)], axis=1)
    padded = jnp.pad(per_row, ((0, 0), (0, 0), (0, 0), (lead, span - width - lead)))
    tiled = jnp.broadcast_to(padded[:, :, :, None, :], (heads, NA_ROWS, 3 * NA_ROWS, GRID_W, span))
    skew = tiled.reshape(heads, NA_ROWS, 3 * NA_ROWS, GRID_W * span)[..., :GRID_W * (span - 1)]
    toep = skew.reshape(heads, NA_ROWS, 3 * NA_ROWS, GRID_W, span - 1)[..., GRID_W - 2:]
    table = jnp.transpose(toep, (0, 1, 3, 2, 4)).reshape(heads, NA_ROWS * GRID_W, 3 * NA_ROWS * GRID_W)
    a = np.arange(NA_ROWS)[:, None, None, None]
    cq = np.arange(GRID_W)[None, :, None, None]
    kr = np.arange(3 * NA_ROWS)[None, None, :, None]
    ck = np.arange(GRID_W)[None, None, None, :]
    valids = []
    for j in (0, 1, nb - 1):
        r = NA_ROWS * j + a
        rs = np.clip(r - NA_KH // 2, 0, rows - NA_KH)
        rk = NA_ROWS * (j - 1) + kr
        cs = np.clip(cq - NA_KW // 2, 0, GRID_W - NA_KW)
        valid = (rk >= rs) & (rk < rs + NA_KH) & (ck >= cs) & (ck < cs + NA_KW)
        valids.append(valid.reshape(NA_ROWS * GRID_W, 3 * NA_ROWS * GRID_W))
    valid = jnp.asarray(np.stack(valids))
    return jnp.where(valid[:, None], table[None], NEG_INF)


def _na_attention(q, k, v, kc, vc, bias):
    b, n, _ = q.shape
    c = kc.shape[1]
    tq = NA_ROWS * GRID_W
    nb = n // tq
    cur = lambda bi, i: (bi, i, 0)
    prev = lambda bi, i: (bi, jnp.maximum(i - 1, 0), 0)
    nxt = lambda bi, i: (bi, jnp.minimum(i + 1, nb - 1), 0)
    ctx = lambda bi, i: (bi, 0, 0)
    variant = lambda bi, i: (jnp.where(i == 0, 0, jnp.where(i == nb - 1, 2, 1)), 0, 0, 0)
    blk = pl.BlockSpec((1, tq, 256), cur)
    est = 2 * 4 * tq * 3 * tq * 4 + 16 * tq * (3 * tq + c) * 4 + 8 * MIB
    return pl.pallas_call(
        _na_kernel,
        out_shape=jax.ShapeDtypeStruct((b, n, 256), BF16),
        grid=(b, nb),
        in_specs=[blk, pl.BlockSpec((1, tq, 256), prev), blk, pl.BlockSpec((1, tq, 256), nxt),
                  pl.BlockSpec((1, tq, 256), prev), blk, pl.BlockSpec((1, tq, 256), nxt),
                  pl.BlockSpec((1, c, 256), ctx), pl.BlockSpec((1, c, 256), ctx),
                  pl.BlockSpec((1, 4, tq, 3 * tq), variant)],
        out_specs=blk,
        compiler_params=_cparams(("arbitrary", "arbitrary"), est),
    )(q, k, k, k, v, v, v, kc, vc, bias)


def _ctx_attn_kernel(mq_ref, mk_ref, mv_ref, sq_ref, sk_ref, sv_ref, sink_ref, nq_ref, nk_ref, nv_ref,
                     ya_ref, yb_ref, yc_ref):
    m0, m1 = _half_masks()
    none2 = (None, None)
    for pair in range(2):
        sl = slice(pair * LANES, (pair + 1) * LANES)
        h0 = slice(2 * pair * LANES, (2 * pair + 1) * LANES)
        h1 = slice((2 * pair + 1) * LANES, (2 * pair + 2) * LANES)
        ya_ref[0, :, sl] = _attend_pair((mq_ref[0, :, h0], mq_ref[0, :, h1]),
                                        [((mk_ref[0, :, h0], mk_ref[0, :, h1]), mv_ref[0, :, sl], none2)],
                                        None).astype(BF16)
        qg = sq_ref[0, :, sl]
        sinks = [sink_ref[2 * pair + s:2 * pair + s + 1, 0:1] for s in (0, 1)]
        yb_ref[0, :, sl] = _attend_pair((qg * m0, qg * m1), [((sk_ref[0],) * 2, sv_ref[0], none2)],
                                        sinks).astype(BF16)
        qn = nq_ref[0, :, sl]
        yc_ref[0, :, sl] = _attend_pair((qn * m0, qn * m1), [((nk_ref[0, :, sl],) * 2, nv_ref[0, :, sl], none2)],
                                        None).astype(BF16)


def _ctx_attention(mq, mk, mv, sq, sk, sv, sink_rows, nq, nk, nv):
    b, c, _ = mq.shape
    spec = lambda wd: pl.BlockSpec((1, c, wd), lambda bi: (bi, 0, 0))
    return pl.pallas_call(
        _ctx_attn_kernel,
        out_shape=[jax.ShapeDtypeStruct((b, c, 256), BF16)] * 3,
        grid=(b,),
        in_specs=[spec(512), spec(512), spec(256), spec(256), spec(LANES), spec(LANES),
                  pl.BlockSpec((4, LANES), lambda bi: (0, 0)), spec(256), spec(256), spec(256)],
        out_specs=[spec(256)] * 3,
        compiler_params=_cparams(("arbitrary",), 24 * MIB),
    )(mq, mk, mv, sq, sk, sv, sink_rows, nq, nk, nv)


def _ret_tables(dec_lane, reverse):
    c = RET_CHUNK
    lg = -(jnp.maximum(-dec_lane, 0.0) + jnp.log1p(jnp.exp(-jnp.abs(dec_lane))))
    ti = lax.broadcasted_iota(jnp.int32, (c, LANES), 0).astype(F32)
    if reverse:
        qpow, kpow = c - ti, ti
    else:
        qpow, kpow = ti + 1.0, c - 1.0 - ti
    ii = lax.broadcasted_iota(jnp.int32, (c, c), 0)
    jj = lax.broadcasted_iota(jnp.int32, (c, c), 1)
    dist = (jj - ii) if reverse else (ii - jj)
    distf = jnp.maximum(dist, 0).astype(F32)
    masks = [jnp.where(dist >= 0, jnp.exp(lg[:, s * HEAD_DIM:s * HEAD_DIM + 1] * distf), 0.0) for s in (0, 1)]
    return jnp.exp(lg * qpow), jnp.exp(lg * kpow), masks[0], masks[1], jnp.exp(lg * float(c))


def _ret_chunk(q, k, v, g, qdec, kdec, decays, cdec, gn_lane, s_ref):
    c = RET_CHUNK
    lo = _low_half((1, LANES))
    ii = lax.broadcasted_iota(jnp.int32, (c, c), 0)
    jj = lax.broadcasted_iota(jnp.int32, (c, c), 1)
    outs = []
    for msk, decay in zip(_half_masks(), decays):
        inner = _dot_nt(q * msk, k) * decay
        outs.append(_dot(inner.astype(BF16), v))
    state = s_ref[...]
    o = jnp.where(lo, outs[0], outs[1]) + _dot((q.astype(F32) * qdec).astype(BF16), state.astype(BF16))
    kd = (k.astype(F32) * kdec).astype(BF16)
    same_head = (ii < HEAD_DIM) == (jj < HEAD_DIM)
    s_ref[...] = state * cdec + jnp.where(same_head, _dot_tn(kd, v), 0.0)

    def head_mean(x):
        s_lo = jnp.sum(jnp.where(lo, x, 0.0), axis=-1, keepdims=True)
        s_hi = jnp.sum(jnp.where(lo, 0.0, x), axis=-1, keepdims=True)
        return jnp.where(lo, s_lo, s_hi) * (1.0 / HEAD_DIM)

    dev = o - head_mean(o)
    normed = dev * lax.rsqrt(head_mean(dev * dev) + EPS)
    gf = g.astype(F32)
    return (gf * jax.nn.sigmoid(gf)) * (normed * gn_lane)


def _ret_kernel(decf_ref, decb_ref, gnf_ref, gnb_ref, s0f_ref, s0b_ref,
                qf_ref, kf_ref, vf_ref, gf_ref, qb_ref, kb_ref, vb_ref, gb_ref,
                yf_ref, yb_ref, sf_ref, sb_ref, st_scr, tab_scr, cdec_scr):
    t = pl.program_id(1)
    directions = ((decf_ref, gnf_ref, qf_ref, kf_ref, vf_ref, gf_ref, yf_ref),
                  (decb_ref, gnb_ref, qb_ref, kb_ref, vb_ref, gb_ref, yb_ref))

    @pl.when(t == 0)
    def _():
        for pair in range(2):
            st_scr[0, pair] = s0f_ref[0, pair]
            st_scr[1, pair] = s0b_ref[0, pair]
            for d, refs in enumerate(directions):
                qdec, kdec, mask0, mask1, cdec = _ret_tables(refs[0][pair], d == 1)
                for idx, tab in enumerate((qdec, kdec, mask0, mask1)):
                    tab_scr[d, pair, idx] = tab
                cdec_scr[d, pair] = jnp.broadcast_to(cdec, (8, LANES))

    for d, (_, gn_ref, q_ref, k_ref, v_ref, g_ref, y_ref) in enumerate(directions):
        for pair in range(2):
            sl = slice(pair * LANES, (pair + 1) * LANES)
            y = _ret_chunk(q_ref[0, :, sl], k_ref[0, :, sl], v_ref[0, :, sl], g_ref[0, :, sl],
                           tab_scr[d, pair, 0], tab_scr[d, pair, 1], (tab_scr[d, pair, 2], tab_scr[d, pair, 3]),
                           cdec_scr[d, pair, 0:1, :], gn_ref[:, sl], st_scr.at[d, pair])
            y_ref[0, :, sl] = y.astype(BF16)

    @pl.when(t == pl.num_programs(1) - 1)
    def _():
        sf_ref[0] = st_scr[0]
        sb_ref[0] = st_scr[1]


def _retention(q, k, v, gf, gb, dec_f, dec_b, gn_f, gn_b, s0f, s0b):
    b, t, width = q.shape
    c = RET_CHUNK
    nt = t // c
    fwd = pl.BlockSpec((1, c, width), lambda bi, i: (bi, i, 0))
    bwd = pl.BlockSpec((1, c, width), lambda bi, i: (bi, nt - 1 - i, 0))
    dec = pl.BlockSpec((2, 1, LANES), lambda bi, i: (0, 0, 0))
    gn = pl.BlockSpec((1, width), lambda bi, i: (0, 0))
    st = pl.BlockSpec((1, 2, LANES, LANES), lambda bi, i: (bi, 0, 0, 0))
    return pl.pallas_call(
        _ret_kernel,
        out_shape=[jax.ShapeDtypeStruct((b, t, width), BF16)] * 2 + [jax.ShapeDtypeStruct((b, 2, LANES, LANES), F32)] * 2,
        grid=(b, nt),
        in_specs=[dec, dec, gn, gn, st, st, fwd, fwd, fwd, fwd, bwd, bwd, bwd, bwd],
        out_specs=[fwd, bwd, st, st],
        scratch_shapes=[pltpu.VMEM((2, 2, LANES, LANES), F32), pltpu.VMEM((2, 2, 4, c, LANES), F32),
                        pltpu.VMEM((2, 2, 8, LANES), F32)],
        compiler_params=_cparams(("arbitrary", "arbitrary"), 24 * MIB),
    )(dec_f, dec_b, gn_f, gn_b, s0f, s0b, q, k, v, gf, q, k, v, gb)


def _outproj_kernel(x_ref, ya_ref, yb_ref, yc_ref, ydf_ref, ydb_ref, beta_ref, w_ref, gate_ref, o_ref):
    ys = (ya_ref[0].astype(F32), yb_ref[0].astype(F32), yc_ref[0].astype(F32),
          ydf_ref[0].astype(F32) + ydb_ref[0].astype(F32))
    acc = 0.0
    for s, y in enumerate(ys):
        sl = slice(s * 256, (s + 1) * 256)
        acc = acc + _dot((y * beta_ref[:, sl]).astype(BF16), w_ref[sl, :])
    o_ref[0] = x_ref[0] + gate_ref[0] * acc


def _out_projection(x, ya, yb, yc, ydf, ydb, beta, w, gate):
    b, t, d = x.shape
    tm = min(512, t)
    row = lambda wd: pl.BlockSpec((1, tm, wd), lambda bi, i: (bi, i, 0))
    est = 2 * (2 * tm * d * 4 + 5 * tm * 256 * 2 + d * d * 2) + 8 * MIB
    return pl.pallas_call(
        _outproj_kernel,
        out_shape=jax.ShapeDtypeStruct((b, t, d), F32),
        grid=(b, t // tm),
        in_specs=[row(d), row(256), row(256), row(256), row(256), row(256),
                  pl.BlockSpec((1, d), lambda bi, i: (0, 0)), pl.BlockSpec((d, d), lambda bi, i: (0, 0)),
                  pl.BlockSpec((1, 1, d), lambda bi, i: (bi, 0, 0))],
        out_specs=row(d),
        compiler_params=_cparams(("arbitrary", "arbitrary"), est),
    )(x, ya, yb, yc, ydf, ydb, beta, w, gate)


def _sorting_network(n):
    size = 1
    while size < n:
        size *= 2
    pairs = []
    p = 1
    while p < size:
        k = p
        while k >= 1:
            for j in range(k % p, size - k, 2 * k):
                for i in range(min(k, size - j - k)):
                    if (i + j) // (2 * p) == (i + j + k) // (2 * p):
                        pairs.append((i + j, i + j + k))
            k //= 2
        p *= 2
    return [(i, j) for i, j in pairs if j < n]


def _top_values(scores, out_ref, count):
    groups = scores.shape[0] // 8
    lists = [scores[8 * g:8 * g + 8, :] for g in range(groups)]
    for i, j in _sorting_network(groups):
        lists[i], lists[j] = jnp.maximum(lists[i], lists[j]), jnp.minimum(lists[i], lists[j])
    for r in range(count):
        m = jnp.max(lists[0], axis=0, keepdims=True)
        out_ref[r:r + 1, :] = m
        hit = lists[0] == m
        need = count - r - 1
        for p in range(min(groups - 1, need)):
            lists[p] = jnp.where(hit, lists[p + 1], lists[p])
        if need >= groups:
            lists[groups - 1] = jnp.where(hit, -jnp.inf, lists[groups - 1])


def _staircase(w1, w2):
    row8 = lax.broadcasted_iota(jnp.int32, (8, w1.shape[1]), 0)
    cands = [w1[0:1] + w2]
    for a in range(1, 8):
        cands.append(jnp.where(row8 < PEER_TOPK // (a + 1), w1[a:a + 1] + w2[0:8], -jnp.inf))
    cands.append(w1[8:16] + w2[0:1])
    return jnp.concatenate(cands, axis=0)


def _bf16_twice(x):
    hi = pltpu.bitcast(x.astype(BF16).astype(F32), jnp.uint32)
    return hi | (hi >> 16)


def _bf16_rows(words, rows):
    return pltpu.bitcast(jnp.broadcast_to(words, (rows // 2, words.shape[1])), BF16)


def _peer_route(hd, q_scr, k1_ref, k2_ref, rk_scr, e2_scr, n_scr, e1_scr, v1_scr, v2_scr, top_scr):
    half = PEER_NKEYS
    base = pl.multiple_of(hd * 2 * half, 2 * half)
    s1 = _dot(k1_ref[...], q_scr[pl.ds(base, half), :].astype(BF16))
    s2 = _dot(k2_ref[...], q_scr[pl.ds(base + half, half), :].astype(BF16))
    _top_values(s1, v1_scr, PEER_TOPK)
    _top_values(s2, v2_scr, PEER_TOPK)
    v1, v2 = v1_scr[...], v2_scr[...]
    cand = _staircase(v1, v2)
    _top_values(cand, top_scr, PEER_TOPK)
    top = top_scr[...]
    z = jnp.sum(jnp.exp(top - top[0:1]), axis=0, keepdims=True)
    sel = jnp.where(cand >= top[PEER_TOPK - 1:PEER_TOPK], 1.0, 0.0)
    count = lambda picked: jnp.sum(picked, axis=0, keepdims=True)
    lens = [count(sel[0:16])] + [count(sel[8 + 8 * a:16 + 8 * a]) for a in range(1, 8)]
    lens += [sel[72 + a:73 + a] for a in range(8)]
    n_sel = jnp.zeros(s1.shape, F32)
    for a in reversed(range(PEER_TOPK)):
        n_sel = jnp.where(s1 >= v1[a:a + 1], lens[a], n_sel)
    rank = jnp.zeros(s2.shape, F32)
    for b in range(PEER_TOPK):
        rank = rank + jnp.where(s2 < v2[b:b + 1], 1.0, 0.0)
    rk_scr[hd] = rank.astype(BF16)
    n_scr[hd] = _bf16_twice(n_sel)
    e1_scr[hd] = _bf16_twice(jnp.exp(s1 - v1[0:1]) / z)
    e2_scr[hd] = (jnp.exp(s2 - v2[0:1]) * 0.5).astype(BF16)


def _peer_kernel(*refs, final_norm):
    if final_norm:
        (x_ref, sh_ref, sc_ref, gt_ref, g_ref, wqt_ref, k1_ref, k2_ref, u_ref, vt_ref, fg_ref, o_ref,
         ht_scr, q_scr, rk_scr, e2_scr, n_scr, e1_scr, v1_scr, v2_scr, top_scr,
         a0_scr, a1_scr, acc_scr) = refs
    else:
        (x_ref, sh_ref, sc_ref, gt_ref, g_ref, wqt_ref, k1_ref, k2_ref, u_ref, vt_ref, o_ref,
         ht_scr, q_scr, rk_scr, e2_scr, n_scr, e1_scr, v1_scr, v2_scr, top_scr,
         a0_scr, a1_scr, acc_scr) = refs
        fg_ref = None
    step = pl.program_id(2)
    tokens = ht_scr.shape[1]

    @pl.when(step == 0)
    def _():
        h = _norm_mod(x_ref[0], g_ref[...], sh_ref[0], sc_ref[0])
        ht = jnp.transpose(h).astype(BF16)
        ht_scr[...] = ht
        q_scr[...] = _dot(wqt_ref[...], ht)
        acc_scr[...] = jnp.zeros(acc_scr.shape, F32)

        def route(pair, carry):
            for sub in range(2):
                _peer_route(2 * pair + sub, q_scr, k1_ref, k2_ref, rk_scr, e2_scr, n_scr, e1_scr,
                            v1_scr.at[sub], v2_scr.at[sub], top_scr.at[sub])
            return carry

        lax.fori_loop(0, PEER_HEADS // 2, route, 0)

    def project(a_write):
        a_write[...] = _dot(u_ref[...], ht_scr[...])

    def consume(a_read):
        y = None
        piece = 2 * PEER_NKEYS
        for kt in range(PEER_ROWS // 2):
            ws = []
            for r in (2 * kt, 2 * kt + 1):
                rows = slice(r * PEER_NKEYS, (r + 1) * PEER_NKEYS)
                i = (step - 1) * PEER_ROWS + r
                gate = jnp.zeros((PEER_NKEYS, tokens), BF16)
                for hd in range(PEER_HEADS):
                    n_row = _bf16_rows(n_scr[hd, pl.ds(i, 1), :], PEER_NKEYS)
                    e1_row = _bf16_rows(e1_scr[hd, pl.ds(i, 1), :], PEER_NKEYS)
                    gate = gate + jnp.where(rk_scr[hd] < n_row, e2_scr[hd] * e1_row, jnp.zeros_like(gate))
                a = a_read[rows, :].astype(BF16)
                gelu2 = a * (1.0 + lax.erf(a * np.sqrt(0.5).astype(BF16)))
                ws.append(gelu2 * gate)
            part = _dot(vt_ref[:, kt * piece:(kt + 1) * piece], jnp.concatenate(ws, axis=0))
            y = part if y is None else y + part
        acc_scr[...] += y

    last = pl.num_programs(2) - 1
    bufs = (a0_scr, a1_scr)

    @pl.when(step == 0)
    def _():
        project(a0_scr)

    for parity in (0, 1):
        @pl.when(jnp.logical_and(jnp.logical_and(step > 0, step < last), step % 2 == parity))
        def _():
            project(bufs[parity])
            consume(bufs[1 - parity])

        @pl.when(jnp.logical_and(step == last, step % 2 == parity))
        def _():
            consume(bufs[1 - parity])

    @pl.when(step == pl.num_programs(2) - 1)
    def _():
        out = x_ref[0] + gt_ref[0] * jnp.transpose(acc_scr[...])
        if final_norm:
            out = _rmsnorm(out, fg_ref[...])
        o_ref[0] = out


def _peer_ffn(x, shift, scale, gate, g, wqt, k1, k2, u, vt, final_g=None):
    b, t, d = x.shape
    tt = PEER_TOKENS
    ec = PEER_ROWS * PEER_NKEYS
    n_blocks = u.shape[0] // ec
    nq = wqt.shape[0]
    final_norm = final_g is not None
    tok = pl.BlockSpec((1, tt, d), lambda bi, i, s: (bi, i, 0))
    vec = pl.BlockSpec((1, 1, d), lambda bi, i, s: (bi, 0, 0))
    const = lambda shape: pl.BlockSpec(shape, lambda bi, i, s: (0,) * len(shape))
    in_specs = [tok, vec, vec, vec, const((1, d)), const((nq, d)), const(k1.shape), const(k2.shape),
                pl.BlockSpec((ec, d), lambda bi, i, s: (jnp.minimum(s, n_blocks - 1), 0)),
                pl.BlockSpec((d, ec), lambda bi, i, s: (0, jnp.clip(s - 1, 0, n_blocks - 1)))]
    args = [x, shift, scale, gate, g, wqt, k1, k2, u, vt]
    if final_norm:
        in_specs.append(const((1, d)))
        args.append(final_g)
    table = pltpu.VMEM((PEER_HEADS, PEER_NKEYS, tt), jnp.uint32)
    table16 = pltpu.VMEM((PEER_HEADS, PEER_NKEYS, tt), BF16)
    top = pltpu.VMEM((2, PEER_TOPK, tt), F32)
    a_buf = pltpu.VMEM((ec, tt), F32)
    scratch = [pltpu.VMEM((d, tt), BF16), pltpu.VMEM((nq, tt), F32), table16, table16, table, table,
               top, top, top, a_buf, a_buf, pltpu.VMEM((d, tt), F32)]
    est = (4 * tt * d * 4 + 2 * nq * d * 2 + 4 * ec * d * 2 + tt * d * 2 + nq * tt * 4
           + 4 * PEER_HEADS * PEER_NKEYS * tt * 4 + d * tt * 4 + 2 * ec * tt * 6 + 8 * MIB)
    return pl.pallas_call(
        functools.partial(_peer_kernel, final_norm=final_norm),
        out_shape=jax.ShapeDtypeStruct((b, t, d), F32),
        grid=(b, t // tt, n_blocks + 1),
        in_specs=in_specs,
        out_specs=tok,
        scratch_shapes=scratch,
        compiler_params=_cparams(("arbitrary", "arbitrary", "arbitrary"), est),
    )(*args)


def _prep_in_weights(w_in):
    d = w_in.shape[0]
    zeros = lambda n: jnp.zeros((d, n), w_in.dtype)
    cols = []
    for hd in range(4):
        cols += [w_in[:, hd * MLA_QK:(hd + 1) * MLA_QK], zeros(LANES - MLA_QK)]
    cols.append(w_in[:, 384:512])
    cols += [zeros(HEAD_DIM), w_in[:, 512:544], zeros(LANES - HEAD_DIM - MLA_ROPE)]
    for hd in (0, 2, 1, 3):
        cols.append(w_in[:, 544 + hd * HEAD_DIM:544 + (hd + 1) * HEAD_DIM])
    cols.append(w_in[:, 800:])
    return jnp.concatenate(cols, axis=1).astype(BF16)


def _prep_kv_weights(w_ukv):
    zeros = jnp.zeros((MLA_RANK, HEAD_DIM), w_ukv.dtype)
    wkn = jnp.concatenate([blk for hd in range(4) for blk in (w_ukv[:, hd * LANES:hd * LANES + HEAD_DIM], zeros)], axis=1)
    wv = jnp.concatenate([w_ukv[:, hd * LANES + HEAD_DIM:(hd + 1) * LANES] for hd in range(4)], axis=1)
    sel = np.zeros((LANES, 4 * LANES), np.float32)
    for hd in range(4):
        for r in range(MLA_ROPE):
            sel[HEAD_DIM + r, hd * LANES + HEAD_DIM + r] = 1.0
    return wkn.astype(BF16), jnp.asarray(sel, BF16), wv.astype(BF16)


def _rope_lane_tables(n):
    def tables(rot_dim):
        quarter = rot_dim // 4
        inv = ROPE_BASE ** (-jnp.arange(quarter, dtype=F32) / quarter)
        t = jnp.arange(n, dtype=jnp.int32)
        pos = jnp.stack([t // GRID_W, t % GRID_W], axis=-1).astype(F32)
        ang = pos[:, :, None] * inv
        lanes = lambda tb: jnp.concatenate([tb[:, 0], tb[:, 0], tb[:, 1], tb[:, 1]], axis=-1)
        return lanes(jnp.cos(ang)), lanes(jnp.sin(ang))

    c32, s32 = tables(MLA_ROPE)
    c64, s64 = tables(HEAD_DIM)
    ones = lambda w: jnp.ones((n, w), F32)
    zeros = lambda w: jnp.zeros((n, w), F32)
    return (jnp.concatenate([ones(HEAD_DIM), c32, ones(LANES - HEAD_DIM - MLA_ROPE)], axis=-1),
            jnp.concatenate([zeros(HEAD_DIM), s32, zeros(LANES - HEAD_DIM - MLA_ROPE)], axis=-1),
            jnp.concatenate([c64, c64], axis=-1), jnp.concatenate([s64, s64], axis=-1))


_SWA_SLOT_HEADS = (0, 2, 1, 3)


def kernel(x, c, ctx, c_ctx, ada_w, ada_b, norm_mix_g, w_in, mla_kv_norm_g, mla_w_ukv, swa_sink, na_rpb, ret_decay_f, ret_decay_b, ret_gn_f, ret_gn_b, mix_beta, w_out, norm_ffn_g, peer_wq, peer_k1, peer_k2, peer_u, peer_v, final_norm_g):
    b, n, d = x.shape
    n_ctx = ctx.shape[1]
    depth = ada_w.shape[0]
    rows = n // GRID_W

    pad_rows = -(b + 1) % 8
    cc = jnp.concatenate([c, c_ctx[None, :], jnp.zeros((pad_rows, d), F32)], axis=0)
    mod = _ada_modulation(cc, ada_w, ada_b)

    lat_tabs = _rope_lane_tables(n)
    ctx_tabs = (jnp.ones((n_ctx, LANES), F32), jnp.zeros((n_ctx, LANES), F32)) * 2
    slot_heads = np.asarray(_SWA_SLOT_HEADS)
    out_perm = np.arange(d)
    out_perm[256:512] = 256 + (slot_heads[:, None] * HEAD_DIM + np.arange(HEAD_DIM)[None, :]).reshape(-1)
    zero_state = jnp.zeros((b, 2, LANES, LANES), F32)
    pair_lanes = lambda p: jnp.repeat(p.astype(F32), HEAD_DIM).reshape(2, 1, LANES)

    xc = ctx
    for layer in range(depth):
        with_ctx = layer < depth - 1
        last = layer == depth - 1
        chunk = lambda k, lo, hi: mod[layer, lo:hi, k * d:(k + 1) * d][:, None, :]
        mod_l = [chunk(k, 0, b) for k in range(6)]
        mod_c = [jnp.broadcast_to(chunk(k, b, b + 1), (b, 1, d)) for k in range(6)]

        w_cols = _prep_in_weights(w_in[layer])
        wkn, emat, wv = _prep_kv_weights(mla_w_ukv[layer])
        g_mix = norm_mix_g[layer][None, :]
        kvg = mla_kv_norm_g[layer][None, :]
        p_l = _in_projection(x, mod_l[0], mod_l[1], g_mix, w_cols, kvg, wkn, emat, wv, lat_tabs)
        p_c = _in_projection(xc, mod_c[0], mod_c[1], g_mix, w_cols, kvg, wkn, emat, wv, ctx_tabs)
        mq, mk, mv, sq, sk, sv, nq, nk, nv, rq, rk, rv, gf, gb = p_l
        cmq, cmk, cmv, csq, csk, csv, cnq, cnk, cnv, crq, crk, crv, cgf, cgb = p_c

        sink_rows = jnp.broadcast_to(swa_sink[layer].astype(F32)[slot_heads][:, None], (4, LANES))
        bias = _na_bias_tables(na_rpb[layer], rows)
        dec_f, dec_b = pair_lanes(ret_decay_f[layer]), pair_lanes(ret_decay_b[layer])
        gn_f, gn_b = ret_gn_f[layer][None, :], ret_gn_b[layer][None, :]

        ya = _mla_attention(mq, cmk, cmv, mk, mv)
        yb = _swa_attention(sq, sk, sv, csk, csv, sink_rows)
        yc = _na_attention(nq, nk, nv, cnk, cnv, bias)
        cyf, cyb, s_f, s_b = _retention(crq, crk, crv, cgf, cgb, dec_f, dec_b, gn_f, gn_b, zero_state, zero_state)
        ydf, ydb, _, _ = _retention(rq, rk, rv, gf, gb, dec_f, dec_b, gn_f, gn_b, s_f, s_b)

        beta = mix_beta[layer][out_perm][None, :]
        w_o = w_out[layer][out_perm, :].astype(BF16)
        x = _out_projection(x, ya, yb, yc, ydf, ydb, beta, w_o, mod_l[2])

        g_ffn = norm_ffn_g[layer][None, :]
        wqt = peer_wq[layer].T.astype(BF16)
        k1, k2 = peer_k1[layer].astype(BF16), peer_k2[layer].astype(BF16)
        u, vt = peer_u[layer].astype(BF16), peer_v[layer].T.astype(BF16)
        x = _peer_ffn(x, mod_l[3], mod_l[4], mod_l[5], g_ffn, wqt, k1, k2, u, vt,
                      final_norm_g[None, :] if last else None)
        if with_ctx:
            cya, cyb_, cyc = _ctx_attention(cmq, cmk, cmv, csq, csk, csv, sink_rows, cnq, cnk, cnv)
            xc = _out_projection(xc, cya, cyb_, cyc, cyf, cyb, beta, w_o, mod_c[2])
            xc = _peer_ffn(xc, mod_c[3], mod_c[4], mod_c[5], g_ffn, wqt, k1, k2, u, vt)
    return x
```

```python
import functools

import numpy as np
import jax
import jax.numpy as jnp
from jax import lax
from jax.experimental import pallas as pl
from jax.experimental.pallas import tpu as pltpu

F32 = jnp.float32
BF16 = jnp.bfloat16

EPS = 1e-6
NEG_INF = -1e30
ROPE_BASE = 10000.0
GRID_W = 64
HEAD_DIM = 64
LANES = 128
V7X_VMEM_BYTES = 64 * 1024 * 1024
MIB = 1024 * 1024

MLA_ROPE = 32
MLA_QK = 96
MLA_RANK = 128
SWA_WINDOW = 128
NA_KH, NA_KW = 8, 16
NA_ROWS = 4
RET_CHUNK = 128
PEER_HEADS = 8
PEER_NKEYS = 128
PEER_TOPK = 16
PEER_TOKENS = 256
PEER_ROWS = 16
PEER_SHARE = 2

_MQ, _CKV, _KR, _SQ, _SK, _SV, _NQ, _NK, _NV, _RQ, _RK, _RV, _GF, _GB, _IN_COLS = (
    0, 512, 640, 768, 1024, 1152, 1280, 1536, 1792, 2048, 2304, 2560, 2816, 3072, 3328)


def _cparams(semantics, vmem_bytes):
    limit = int(min(vmem_bytes, V7X_VMEM_BYTES * 7 // 8))
    return pltpu.CompilerParams(dimension_semantics=semantics, vmem_limit_bytes=limit)


def _dot(a, b):
    return jnp.dot(a, b, preferred_element_type=F32)


def _dot_nt(a, b):
    return lax.dot_general(a, b, (((1,), (1,)), ((), ())), preferred_element_type=F32)


def _dot_tn(a, b):
    return lax.dot_general(a, b, (((0,), (0,)), ((), ())), preferred_element_type=F32)


def _low_half(shape):
    return lax.broadcasted_iota(jnp.int32, shape, len(shape) - 1) < HEAD_DIM


def _half_masks():
    lo = _low_half((1, LANES))
    return (jnp.where(lo, 1.0, 0.0).astype(BF16), jnp.where(lo, 0.0, 1.0).astype(BF16))


def _norm_mod(x, g, shift, scale):
    ms = jnp.mean(x * x, axis=-1, keepdims=True)
    return (x * lax.rsqrt(ms + EPS) * g) * (1.0 + scale) + shift


def _rmsnorm(x, g):
    ms = jnp.mean(x * x, axis=-1, keepdims=True)
    return x * lax.rsqrt(ms + EPS) * g


def _ada_kernel(c_ref, w_ref, b_ref, o_ref):
    c = c_ref[...]
    o_ref[0] = _dot(c * jax.nn.sigmoid(c), w_ref[0]) + b_ref[0]


def _ada_modulation(cc, ada_w, ada_b):
    depth, d, width = ada_w.shape
    rows = cc.shape[0]
    tn = 1024
    return pl.pallas_call(
        _ada_kernel,
        out_shape=jax.ShapeDtypeStruct((depth, rows, width), F32),
        grid=(depth, width // tn),
        in_specs=[pl.BlockSpec((rows, d), lambda l, j: (0, 0)),
                  pl.BlockSpec((1, d, tn), lambda l, j: (l, 0, j)),
                  pl.BlockSpec((1, 1, tn), lambda l, j: (l, 0, j))],
        out_specs=pl.BlockSpec((1, rows, tn), lambda l, j: (l, 0, j)),
        compiler_params=_cparams(("arbitrary", "arbitrary"), 2 * d * tn * 4 + 8 * MIB),
    )(cc, ada_w, ada_b.reshape(depth, 1, width))


def _rope(a, cos, sin, half):
    lane = lax.broadcasted_iota(jnp.int32, a.shape, 1)
    first = (lane % (2 * half)) < half
    rot = jnp.where(first, -pltpu.roll(a, LANES - half, 1), pltpu.roll(a, half, 1))
    return a * cos + rot * sin


def _inproj_kernel(x_ref, sh_ref, sc_ref, g_ref, w_ref, kvg_ref, wkn_ref, e_ref, wv_ref,
                   c32_ref, s32_ref, c64_ref, s64_ref,
                   mq_ref, mk_ref, mv_ref, sq_ref, sk_ref, sv_ref, nq_ref, nk_ref, nv_ref,
                   rq_ref, rk_ref, rv_ref, gf_ref, gb_ref):
    h = _norm_mod(x_ref[0], g_ref[...], sh_ref[0], sc_ref[0]).astype(BF16)

    def seg(lo, width):
        return _dot(h, w_ref[:, lo:lo + width])

    c32, s32, c64, s64 = c32_ref[...], s32_ref[...], c64_ref[...], s64_ref[...]
    mla_scale = MLA_QK ** -0.5
    head_scale = HEAD_DIM ** -0.5
    for hd in range(4):
        a = seg(_MQ + hd * LANES, LANES)
        mq_ref[0, :, hd * LANES:(hd + 1) * LANES] = (_rope(a, c32, s32, MLA_ROPE // 4) * mla_scale).astype(BF16)
    kvn = _rmsnorm(seg(_CKV, MLA_RANK), kvg_ref[...]).astype(BF16)
    kr = _rope(seg(_KR, LANES), c32, s32, MLA_ROPE // 4).astype(BF16)
    mk_ref[0] = (_dot(kvn, wkn_ref[...]) + _dot(kr, e_ref[...])).astype(BF16)
    mv_ref[0] = _dot(kvn, wv_ref[...]).astype(BF16)
    for grp in range(2):
        a = seg(_SQ + grp * LANES, LANES)
        sq_ref[0, :, grp * LANES:(grp + 1) * LANES] = (_rope(a, c64, s64, HEAD_DIM // 4) * head_scale).astype(BF16)
    sk_ref[0] = _rope(seg(_SK, LANES), c64, s64, HEAD_DIM // 4).astype(BF16)
    sv_ref[0] = seg(_SV, LANES).astype(BF16)
    nq_ref[0] = (seg(_NQ, 256) * head_scale).astype(BF16)
    nk_ref[0] = seg(_NK, 256).astype(BF16)
    nv_ref[0] = seg(_NV, 256).astype(BF16)
    rq_ref[0] = seg(_RQ, 256).astype(BF16)
    rk_ref[0] = (seg(_RK, 256) * head_scale).astype(BF16)
    rv_ref[0] = seg(_RV, 256).astype(BF16)
    gf_ref[0] = seg(_GF, 256).astype(BF16)
    gb_ref[0] = seg(_GB, 256).astype(BF16)


_INPROJ_WIDTHS = (512, 512, 256, 256, 128, 128, 256, 256, 256, 256, 256, 256, 256, 256)


def _in_projection(x, shift, scale, g, w, kvg, wkn, emat, wv, tabs):
    b, t, d = x.shape
    tm = min(512, t)
    const = lambda shape: pl.BlockSpec(shape, lambda bi, i: (0,) * len(shape))
    tab = pl.BlockSpec((tm, LANES), lambda bi, i: (i, 0))
    vec = pl.BlockSpec((1, 1, d), lambda bi, i: (bi, 0, 0))
    est = 2 * (tm * d * 4 + d * _IN_COLS * 2 + tm * sum(_INPROJ_WIDTHS) * 2 + 4 * tm * LANES * 4) + 12 * MIB
    return pl.pallas_call(
        _inproj_kernel,
        out_shape=[jax.ShapeDtypeStruct((b, t, wd), BF16) for wd in _INPROJ_WIDTHS],
        grid=(b, t // tm),
        in_specs=[pl.BlockSpec((1, tm, d), lambda bi, i: (bi, i, 0)), vec, vec, const((1, d)),
                  const((d, _IN_COLS)), const((1, MLA_RANK)), const((MLA_RANK, 512)), const((LANES, 512)),
                  const((MLA_RANK, 256)), tab, tab, tab, tab],
        out_specs=[pl.BlockSpec((1, tm, wd), lambda bi, i: (bi, i, 0)) for wd in _INPROJ_WIDTHS],
        compiler_params=_cparams(("arbitrary", "arbitrary"), est),
    )(x, shift, scale, g, w, kvg, wkn, emat, wv, *tabs)


def _attend_pair(qs, pieces, sinks):
    lo = _low_half((1, LANES))
    m0, m1 = _half_masks()
    outs = []
    for s in (0, 1):
        own, other = (m0, m1) if s == 0 else (m1, m0)
        scores = []
        for ks, _, biases in pieces:
            sc = _dot_nt(qs[s], ks[s])
            if biases[s] is not None:
                sc = sc + biases[s]
            scores.append(sc)
        m = functools.reduce(jnp.maximum, [jnp.max(sc, axis=-1, keepdims=True) for sc in scores])
        if sinks is not None:
            m = jnp.maximum(m, sinks[s])
        o = 0.0
        for sc, (_, v, _) in zip(scores, pieces):
            o = o + _dot(jnp.exp(sc - m).astype(BF16), v * own + other)
        own_lanes = lo if s == 0 else jnp.logical_not(lo)
        den = jnp.where(own_lanes, 1.0, o)
        if sinks is not None:
            den = den + jnp.exp(sinks[s] - m)
        outs.append(o * pltpu.roll(1.0 / den, HEAD_DIM, 1))
    return jnp.where(lo, outs[0], outs[1])


def _mla_kernel(q_ref, kc_ref, vc_ref, kl_ref, vl_ref, o_ref, m_ref, l_ref, acc_ref):
    j = pl.program_id(2)

    @pl.when(j == 0)
    def _():
        m_ref[...] = jnp.full(m_ref.shape, NEG_INF, F32)
        l_ref[...] = jnp.zeros(l_ref.shape, F32)
        acc_ref[...] = jnp.zeros(acc_ref.shape, F32)

    def update(k, v):
        lo = _low_half((1, LANES))
        m0, m1 = _half_masks()
        reps = k.shape[0] // LANES
        for pair in range(2):
            vp = v[:, pair * LANES:(pair + 1) * LANES]
            vals = (vp * m0 + m1, vp * m1 + m0)
            alphas, pvs = [], []
            for s in (0, 1):
                hd = 2 * pair + s
                q = q_ref[0, :, hd * LANES:(hd + 1) * LANES]
                sc = _dot_nt(q, k[:, hd * LANES:(hd + 1) * LANES])
                m_prev = m_ref[hd]
                m_new = jnp.maximum(m_prev, jnp.max(sc, axis=-1, keepdims=True))
                m_ref[hd] = m_new
                alphas.append(jnp.exp(m_prev - m_new))
                p = jnp.exp(sc - jnp.tile(m_new, (1, reps)))
                pvs.append(_dot(p.astype(BF16), vals[s]))
            acc_ref[pair] = (acc_ref[pair] * jnp.where(lo, alphas[0], alphas[1])
                             + jnp.where(lo, pvs[0], pvs[1]))
            l_ref[pair] = (l_ref[pair] * jnp.where(lo, alphas[1], alphas[0])
                           + jnp.where(lo, pvs[1], pvs[0]))

    @pl.when(j == 0)
    def _():
        update(kc_ref[0], vc_ref[0])

    @pl.when(j > 0)
    def _():
        update(kl_ref[0], vl_ref[0])

    @pl.when(j == pl.num_programs(2) - 1)
    def _():
        for pair in range(2):
            inv = pltpu.roll(1.0 / l_ref[pair], HEAD_DIM, 1)
            o_ref[0, :, pair * LANES:(pair + 1) * LANES] = (acc_ref[pair] * inv).astype(BF16)


def _mla_attention(q, kc, vc, kl, vl):
    b, n, _ = q.shape
    c = kc.shape[1]
    tq = min(1024, n)
    tk = min(1024, n)
    est = 2 * (tq * 512 * 2 + c * 768 * 2 + tk * 768 * 2 + tq * 256 * 2) + 6 * tq * LANES * 4 + 10 * tq * tk * 4 + 8 * MIB
    return pl.pallas_call(
        _mla_kernel,
        out_shape=jax.ShapeDtypeStruct((b, n, 256), BF16),
        grid=(b, n // tq, 1 + n // tk),
        in_specs=[pl.BlockSpec((1, tq, 512), lambda bi, i, j: (bi, i, 0)),
                  pl.BlockSpec((1, c, 512), lambda bi, i, j: (bi, 0, 0)),
                  pl.BlockSpec((1, c, 256), lambda bi, i, j: (bi, 0, 0)),
                  pl.BlockSpec((1, tk, 512), lambda bi, i, j: (bi, jnp.maximum(j - 1, 0), 0)),
                  pl.BlockSpec((1, tk, 256), lambda bi, i, j: (bi, jnp.maximum(j - 1, 0), 0))],
        out_specs=pl.BlockSpec((1, tq, 256), lambda bi, i, j: (bi, i, 0)),
        scratch_shapes=[pltpu.VMEM((4, tq, LANES), F32), pltpu.VMEM((2, tq, LANES), F32), pltpu.VMEM((2, tq, LANES), F32)],
        compiler_params=_cparams(("arbitrary", "arbitrary", "arbitrary"), est),
    )(q, kc, vc, kl, vl)


def _swa_kernel(q_ref, kp_ref, kcur_ref, kn_ref, vp_ref, vcur_ref, vn_ref, kctx_ref, vctx_ref, sink_ref, o_ref):
    i = pl.program_id(1)
    last = pl.num_programs(1) - 1
    tq = q_ref.shape[1]

    def band_bias(width, offset, edge_penalty):
        r = lax.broadcasted_iota(jnp.int32, (tq, width), 0)
        cidx = lax.broadcasted_iota(jnp.int32, (tq, width), 1)
        rel = cidx + offset - r
        inside = jnp.where(rel >= -SWA_WINDOW, jnp.where(rel <= SWA_WINDOW, 1, 0), 0)
        return jnp.where(inside == 1, edge_penalty, NEG_INF)

    halo = kp_ref.shape[1]
    b_prev = band_bias(halo, -halo, jnp.where(i > 0, 0.0, NEG_INF))
    b_cur = band_bias(tq, 0, 0.0)
    b_next = band_bias(halo, tq, jnp.where(i < last, 0.0, NEG_INF))
    m0, m1 = _half_masks()
    kctx, vctx = kctx_ref[0], vctx_ref[0]
    pieces = [((kp_ref[0],) * 2, vp_ref[0], (b_prev,) * 2),
              ((kcur_ref[0],) * 2, vcur_ref[0], (b_cur,) * 2),
              ((kn_ref[0],) * 2, vn_ref[0], (b_next,) * 2),
              ((kctx,) * 2, vctx, (None, None))]
    for grp in range(2):
        qg = q_ref[0, :, grp * LANES:(grp + 1) * LANES]
        sinks = [sink_ref[2 * grp + s:2 * grp + s + 1, 0:1] for s in (0, 1)]
        o_ref[0, :, grp * LANES:(grp + 1) * LANES] = _attend_pair((qg * m0, qg * m1), pieces, sinks).astype(BF16)


def _swa_attention(q, k, v, kc, vc, sink_rows):
    b, n, _ = q.shape
    c = kc.shape[1]
    tq = 256
    halo = SWA_WINDOW
    r = tq // halo
    nh = n // halo
    cur = lambda bi, i: (bi, i, 0)
    prev = lambda bi, i: (bi, jnp.maximum(i * r - 1, 0), 0)
    nxt = lambda bi, i: (bi, jnp.minimum(i * r + r, nh - 1), 0)
    ctx = lambda bi, i: (bi, 0, 0)
    est = 16 * tq * (tq + 2 * halo + c) * 4 + 8 * MIB
    return pl.pallas_call(
        _swa_kernel,
        out_shape=jax.ShapeDtypeStruct((b, n, 256), BF16),
        grid=(b, n // tq),
        in_specs=[pl.BlockSpec((1, tq, 256), cur),
                  pl.BlockSpec((1, halo, LANES), prev), pl.BlockSpec((1, tq, LANES), cur), pl.BlockSpec((1, halo, LANES), nxt),
                  pl.BlockSpec((1, halo, LANES), prev), pl.BlockSpec((1, tq, LANES), cur), pl.BlockSpec((1, halo, LANES), nxt),
                  pl.BlockSpec((1, c, LANES), ctx), pl.BlockSpec((1, c, LANES), ctx),
                  pl.BlockSpec((4, LANES), lambda bi, i: (0, 0))],
        out_specs=pl.BlockSpec((1, tq, 256), cur),
        compiler_params=_cparams(("arbitrary", "arbitrary"), est),
    )(q, k, k, k, v, v, v, kc, vc, sink_rows)


def _na_kernel(q_ref, kp_ref, kcur_ref, kn_ref, vp_ref, vcur_ref, vn_ref, kctx_ref, vctx_ref, bias_ref, o_ref):
    tq = q_ref.shape[1]
    m0, m1 = _half_masks()
    for pair in range(2):
        sl = slice(pair * LANES, (pair + 1) * LANES)
        qp = q_ref[0, :, sl]
        pieces = []
        for idx, (kr, vr) in enumerate(((kp_ref, vp_ref), (kcur_ref, vcur_ref), (kn_ref, vn_ref))):
            biases = tuple(bias_ref[0, 2 * pair + s, :, idx * tq:(idx + 1) * tq] for s in (0, 1))
            pieces.append(((kr[0, :, sl],) * 2, vr[0, :, sl], biases))
        pieces.append(((kctx_ref[0, :, sl],) * 2, vctx_ref[0, :, sl], (None, None)))
        o_ref[0, :, sl] = _attend_pair((qp * m0, qp * m1), pieces, None).astype(BF16)


def _na_bias_tables(rpb, rows):
    nb = rows // NA_ROWS
    heads = rpb.shape[0]
    width = 2 * NA_KW - 1
    span = 2 * GRID_W - 1
    lead = GRID_W - 2 - (NA_KW - 1)
    rpb = rpb.astype(F32)
    per_row = jnp.stack([rpb[:, NA_ROWS - 1 - a:NA_ROWS - 1 - a + 3 * NA_ROWS, :] for a in range(NA_ROWS)], axis=1)
    padded = jnp.pad(per_row, ((0, 0), (0, 0), (0, 0), (lead, span - width - lead)))
    tiled = jnp.broadcast_to(padded[:, :, :, None, :], (heads, NA_ROWS, 3 * NA_ROWS, GRID_W, span))
    skew = tiled.reshape(heads, NA_ROWS, 3 * NA_ROWS, GRID_W * span)[..., :GRID_W * (span - 1)]
    toep = skew.reshape(heads, NA_ROWS, 3 * NA_ROWS, GRID_W, span - 1)[..., GRID_W - 2:]
    table = jnp.transpose(toep, (0, 1, 3, 2, 4)).reshape(heads, NA_ROWS * GRID_W, 3 * NA_ROWS * GRID_W)
    a = np.arange(NA_ROWS)[:, None, None, None]
    cq = np.arange(GRID_W)[None, :, None, None]
    kr = np.arange(3 * NA_ROWS)[None, None, :, None]
    ck = np.arange(GRID_W)[None, None, None, :]
    valids = []
    for j in (0, 1, nb - 1):
        r = NA_ROWS * j + a
        rs = np.clip(r - NA_KH // 2, 0, rows - NA_KH)
        rk = NA_ROWS * (j - 1) + kr
        cs = np.clip(cq - NA_KW // 2, 0, GRID_W - NA_KW)
        valid = (rk >= rs) & (rk < rs + NA_KH) & (ck >= cs) & (ck < cs + NA_KW)
        valids.append(valid.reshape(NA_ROWS * GRID_W, 3 * NA_ROWS * GRID_W))
    valid = jnp.asarray(np.stack(valids))
    return jnp.where(valid[:, None], table[None], NEG_INF)


def _na_attention(q, k, v, kc, vc, bias):
    b, n, _ = q.shape
    c = kc.shape[1]
    tq = NA_ROWS * GRID_W
    nb = n // tq
    cur = lambda bi, i: (bi, i, 0)
    prev = lambda bi, i: (bi, jnp.maximum(i - 1, 0), 0)
    nxt = lambda bi, i: (bi, jnp.minimum(i + 1, nb - 1), 0)
    ctx = lambda bi, i: (bi, 0, 0)
    variant = lambda bi, i: (jnp.where(i == 0, 0, jnp.where(i == nb - 1, 2, 1)), 0, 0, 0)
    blk = pl.BlockSpec((1, tq, 256), cur)
    est = 2 * 4 * tq * 3 * tq * 4 + 16 * tq * (3 * tq + c) * 4 + 8 * MIB
    return pl.pallas_call(
        _na_kernel,
        out_shape=jax.ShapeDtypeStruct((b, n, 256), BF16),
        grid=(b, nb),
        in_specs=[blk, pl.BlockSpec((1, tq, 256), prev), blk, pl.BlockSpec((1, tq, 256), nxt),
                  pl.BlockSpec((1, tq, 256), prev), blk, pl.BlockSpec((1, tq, 256), nxt),
                  pl.BlockSpec((1, c, 256), ctx), pl.BlockSpec((1, c, 256), ctx),
                  pl.BlockSpec((1, 4, tq, 3 * tq), variant)],
        out_specs=blk,
        compiler_params=_cparams(("arbitrary", "arbitrary"), est),
    )(q, k, k, k, v, v, v, kc, vc, bias)


def _ctx_attn_kernel(mq_ref, mk_ref, mv_ref, sq_ref, sk_ref, sv_ref, sink_ref, nq_ref, nk_ref, nv_ref,
                     ya_ref, yb_ref, yc_ref):
    m0, m1 = _half_masks()
    none2 = (None, None)
    for pair in range(2):
        sl = slice(pair * LANES, (pair + 1) * LANES)
        h0 = slice(2 * pair * LANES, (2 * pair + 1) * LANES)
        h1 = slice((2 * pair + 1) * LANES, (2 * pair + 2) * LANES)
        ya_ref[0, :, sl] = _attend_pair((mq_ref[0, :, h0], mq_ref[0, :, h1]),
                                        [((mk_ref[0, :, h0], mk_ref[0, :, h1]), mv_ref[0, :, sl], none2)],
                                        None).astype(BF16)
        qg = sq_ref[0, :, sl]
        sinks = [sink_ref[2 * pair + s:2 * pair + s + 1, 0:1] for s in (0, 1)]
        yb_ref[0, :, sl] = _attend_pair((qg * m0, qg * m1), [((sk_ref[0],) * 2, sv_ref[0], none2)],
                                        sinks).astype(BF16)
        qn = nq_ref[0, :, sl]
        yc_ref[0, :, sl] = _attend_pair((qn * m0, qn * m1), [((nk_ref[0, :, sl],) * 2, nv_ref[0, :, sl], none2)],
                                        None).astype(BF16)


def _ctx_attention(mq, mk, mv, sq, sk, sv, sink_rows, nq, nk, nv):
    b, c, _ = mq.shape
    spec = lambda wd: pl.BlockSpec((1, c, wd), lambda bi: (bi, 0, 0))
    return pl.pallas_call(
        _ctx_attn_kernel,
        out_shape=[jax.ShapeDtypeStruct((b, c, 256), BF16)] * 3,
        grid=(b,),
        in_specs=[spec(512), spec(512), spec(256), spec(256), spec(LANES), spec(LANES),
                  pl.BlockSpec((4, LANES), lambda bi: (0, 0)), spec(256), spec(256), spec(256)],
        out_specs=[spec(256)] * 3,
        compiler_params=_cparams(("arbitrary",), 24 * MIB),
    )(mq, mk, mv, sq, sk, sv, sink_rows, nq, nk, nv)


def _ret_tables(dec_lane, reverse):
    c = RET_CHUNK
    lg = -(jnp.maximum(-dec_lane, 0.0) + jnp.log1p(jnp.exp(-jnp.abs(dec_lane))))
    ti = lax.broadcasted_iota(jnp.int32, (c, LANES), 0).astype(F32)
    if reverse:
        qpow, kpow = c - ti, ti
    else:
        qpow, kpow = ti + 1.0, c - 1.0 - ti
    ii = lax.broadcasted_iota(jnp.int32, (c, c), 0)
    jj = lax.broadcasted_iota(jnp.int32, (c, c), 1)
    dist = (jj - ii) if reverse else (ii - jj)
    distf = jnp.maximum(dist, 0).astype(F32)
    masks = [jnp.where(dist >= 0, jnp.exp(lg[:, s * HEAD_DIM:s * HEAD_DIM + 1] * distf), 0.0) for s in (0, 1)]
    return jnp.exp(lg * qpow), jnp.exp(lg * kpow), masks[0], masks[1], jnp.exp(lg * float(c))


def _ret_chunk(q, k, v, g, qdec, kdec, decays, cdec, gn_lane, s_ref):
    c = RET_CHUNK
    lo = _low_half((1, LANES))
    ii = lax.broadcasted_iota(jnp.int32, (c, c), 0)
    jj = lax.broadcasted_iota(jnp.int32, (c, c), 1)
    outs = []
    for msk, decay in zip(_half_masks(), decays):
        inner = _dot_nt(q * msk, k) * decay
        outs.append(_dot(inner.astype(BF16), v))
    state = s_ref[...]
    o = jnp.where(lo, outs[0], outs[1]) + _dot((q.astype(F32) * qdec).astype(BF16), state.astype(BF16))
    kd = (k.astype(F32) * kdec).astype(BF16)
    same_head = (ii < HEAD_DIM) == (jj < HEAD_DIM)
    s_ref[...] = state * cdec + jnp.where(same_head, _dot_tn(kd, v), 0.0)

    def head_mean(x):
        s_lo = jnp.sum(jnp.where(lo, x, 0.0), axis=-1, keepdims=True)
        s_hi = jnp.sum(jnp.where(lo, 0.0, x), axis=-1, keepdims=True)
        return jnp.where(lo, s_lo, s_hi) * (1.0 / HEAD_DIM)

    dev = o - head_mean(o)
    normed = dev * lax.rsqrt(head_mean(dev * dev) + EPS)
    gf = g.astype(F32)
    return (gf * jax.nn.sigmoid(gf)) * (normed * gn_lane)


def _ret_kernel(decf_ref, decb_ref, gnf_ref, gnb_ref, s0f_ref, s0b_ref,
                qf_ref, kf_ref, vf_ref, gf_ref, qb_ref, kb_ref, vb_ref, gb_ref,
                yf_ref, yb_ref, sf_ref, sb_ref, st_scr, tab_scr, cdec_scr):
    t = pl.program_id(1)
    directions = ((decf_ref, gnf_ref, qf_ref, kf_ref, vf_ref, gf_ref, yf_ref),
                  (decb_ref, gnb_ref, qb_ref, kb_ref, vb_ref, gb_ref, yb_ref))

    @pl.when(t == 0)
    def _():
        for pair in range(2):
            st_scr[0, pair] = s0f_ref[0, pair]
            st_scr[1, pair] = s0b_ref[0, pair]
            for d, refs in enumerate(directions):
                qdec, kdec, mask0, mask1, cdec = _ret_tables(refs[0][pair], d == 1)
                for idx, tab in enumerate((qdec, kdec, mask0, mask1)):
                    tab_scr[d, pair, idx] = tab
                cdec_scr[d, pair] = jnp.broadcast_to(cdec, (8, LANES))

    for d, (_, gn_ref, q_ref, k_ref, v_ref, g_ref, y_ref) in enumerate(directions):
        for pair in range(2):
            sl = slice(pair * LANES, (pair + 1) * LANES)
            y = _ret_chunk(q_ref[0, :, sl], k_ref[0, :, sl], v_ref[0, :, sl], g_ref[0, :, sl],
                           tab_scr[d, pair, 0], tab_scr[d, pair, 1], (tab_scr[d, pair, 2], tab_scr[d, pair, 3]),
                           cdec_scr[d, pair, 0:1, :], gn_ref[:, sl], st_scr.at[d, pair])
            y_ref[0, :, sl] = y.astype(BF16)

    @pl.when(t == pl.num_programs(1) - 1)
    def _():
        sf_ref[0] = st_scr[0]
        sb_ref[0] = st_scr[1]


def _retention(q, k, v, gf, gb, dec_f, dec_b, gn_f, gn_b, s0f, s0b):
    b, t, width = q.shape
    c = RET_CHUNK
    nt = t // c
    fwd = pl.BlockSpec((1, c, width), lambda bi, i: (bi, i, 0))
    bwd = pl.BlockSpec((1, c, width), lambda bi, i: (bi, nt - 1 - i, 0))
    dec = pl.BlockSpec((2, 1, LANES), lambda bi, i: (0, 0, 0))
    gn = pl.BlockSpec((1, width), lambda bi, i: (0, 0))
    st = pl.BlockSpec((1, 2, LANES, LANES), lambda bi, i: (bi, 0, 0, 0))
    return pl.pallas_call(
        _ret_kernel,
        out_shape=[jax.ShapeDtypeStruct((b, t, width), BF16)] * 2 + [jax.ShapeDtypeStruct((b, 2, LANES, LANES), F32)] * 2,
        grid=(b, nt),
        in_specs=[dec, dec, gn, gn, st, st, fwd, fwd, fwd, fwd, bwd, bwd, bwd, bwd],
        out_specs=[fwd, bwd, st, st],
        scratch_shapes=[pltpu.VMEM((2, 2, LANES, LANES), F32), pltpu.VMEM((2, 2, 4, c, LANES), F32),
                        pltpu.VMEM((2, 2, 8, LANES), F32)],
        compiler_params=_cparams(("arbitrary", "arbitrary"), 24 * MIB),
    )(dec_f, dec_b, gn_f, gn_b, s0f, s0b, q, k, v, gf, q, k, v, gb)


def _outproj_kernel(x_ref, ya_ref, yb_ref, yc_ref, ydf_ref, ydb_ref, beta_ref, w_ref, gate_ref, o_ref):
    ys = (ya_ref[0].astype(F32), yb_ref[0].astype(F32), yc_ref[0].astype(F32),
          ydf_ref[0].astype(F32) + ydb_ref[0].astype(F32))
    acc = 0.0
    for s, y in enumerate(ys):
        sl = slice(s * 256, (s + 1) * 256)
        acc = acc + _dot((y * beta_ref[:, sl]).astype(BF16), w_ref[sl, :])
    o_ref[0] = x_ref[0] + gate_ref[0] * acc


def _out_projection(x, ya, yb, yc, ydf, ydb, beta, w, gate):
    b, t, d = x.shape
    tm = min(512, t)
    row = lambda wd: pl.BlockSpec((1, tm, wd), lambda bi, i: (bi, i, 0))
    est = 2 * (2 * tm * d * 4 + 5 * tm * 256 * 2 + d * d * 2) + 8 * MIB
    return pl.pallas_call(
        _outproj_kernel,
        out_shape=jax.ShapeDtypeStruct((b, t, d), F32),
        grid=(b, t // tm),
        in_specs=[row(d), row(256), row(256), row(256), row(256), row(256),
                  pl.BlockSpec((1, d), lambda bi, i: (0, 0)), pl.BlockSpec((d, d), lambda bi, i: (0, 0)),
                  pl.BlockSpec((1, 1, d), lambda bi, i: (bi, 0, 0))],
        out_specs=row(d),
        compiler_params=_cparams(("arbitrary", "arbitrary"), est),
    )(x, ya, yb, yc, ydf, ydb, beta, w, gate)


def _sorting_network(n):
    size = 1
    while size < n:
        size *= 2
    pairs = []
    p = 1
    while p < size:
        k = p
        while k >= 1:
            for j in range(k % p, size - k, 2 * k):
                for i in range(min(k, size - j - k)):
                    if (i + j) // (2 * p) == (i + j + k) // (2 * p):
                        pairs.append((i + j, i + j + k))
            k //= 2
        p *= 2
    return [(i, j) for i, j in pairs if j < n]


def _top_values(scores, out_ref, count):
    groups = scores.shape[0] // 8
    lists = [scores[8 * g:8 * g + 8, :] for g in range(groups)]
    for i, j in _sorting_network(groups):
        lists[i], lists[j] = jnp.maximum(lists[i], lists[j]), jnp.minimum(lists[i], lists[j])
    for r in range(count):
        m = jnp.max(lists[0], axis=0, keepdims=True)
        out_ref[r:r + 1, :] = m
        hit = lists[0] == m
        need = count - r - 1
        for p in range(min(groups - 1, need)):
            lists[p] = jnp.where(hit, lists[p + 1], lists[p])
        if need >= groups:
            lists[groups - 1] = jnp.where(hit, -jnp.inf, lists[groups - 1])


def _staircase(w1, w2):
    row8 = lax.broadcasted_iota(jnp.int32, (8, w1.shape[1]), 0)
    cands = [w1[0:1] + w2]
    for a in range(1, 8):
        cands.append(jnp.where(row8 < PEER_TOPK // (a + 1), w1[a:a + 1] + w2[0:8], -jnp.inf))
    cands.append(w1[8:16] + w2[0:1])
    return jnp.concatenate(cands, axis=0)


def _bf16_twice(x):
    hi = pltpu.bitcast(x.astype(BF16).astype(F32), jnp.uint32)
    return hi | (hi >> 16)


def _bf16_rows(words, rows):
    return pltpu.bitcast(jnp.broadcast_to(words, (rows // 2, words.shape[1])), BF16)


def _peer_route(hd, q_scr, k1_ref, k2_ref, rk_scr, e2_scr, n_scr, e1_scr, v1_scr, v2_scr, top_scr):
    half = PEER_NKEYS
    base = pl.multiple_of(hd * 2 * half, 2 * half)
    s1 = _dot(k1_ref[...], q_scr[pl.ds(base, half), :])
    s2 = _dot(k2_ref[...], q_scr[pl.ds(base + half, half), :])
    _top_values(s1, v1_scr, PEER_TOPK)
    _top_values(s2, v2_scr, PEER_TOPK)
    v1, v2 = v1_scr[...], v2_scr[...]
    cand = _staircase(v1, v2)
    _top_values(cand, top_scr, PEER_TOPK)
    top = top_scr[...]
    z = jnp.sum(jnp.exp(top - top[0:1]), axis=0, keepdims=True)
    sel = jnp.where(cand >= top[PEER_TOPK - 1:PEER_TOPK], 1.0, 0.0)
    count = lambda picked: jnp.sum(picked, axis=0, keepdims=True)
    lens = [count(sel[0:16])] + [count(sel[8 + 8 * a:16 + 8 * a]) for a in range(1, 8)]
    lens += [sel[72 + a:73 + a] for a in range(8)]
    n_sel = jnp.zeros(s1.shape, F32)
    for a in reversed(range(PEER_TOPK)):
        n_sel = jnp.where(s1 >= v1[a:a + 1], lens[a], n_sel)
    rank = jnp.zeros(s2.shape, F32)
    for b in range(PEER_TOPK):
        rank = rank + jnp.where(s2 < v2[b:b + 1], 1.0, 0.0)
    rk_scr[hd] = rank.astype(BF16)
    n_scr[hd] = _bf16_twice(n_sel)
    e1_scr[hd] = _bf16_twice(jnp.exp(s1 - v1[0:1]) / z)
    e2_scr[hd] = (jnp.exp(s2 - v2[0:1]) * 0.5).astype(BF16)


def _peer_kernel(*refs, final_norm):
    if final_norm:
        (x_ref, sh_ref, sc_ref, gt_ref, g_ref, wqt_ref, k1_ref, k2_ref, u_ref, vt_ref, fg_ref, o_ref,
         ht_all, q_scr, rk_all, e2_all, n_all, e1_all, v1_scr, v2_scr, top_scr, a0_all, a1_all, acc_all) = refs
    else:
        (x_ref, sh_ref, sc_ref, gt_ref, g_ref, wqt_ref, k1_ref, k2_ref, u_ref, vt_ref, o_ref,
         ht_all, q_scr, rk_all, e2_all, n_all, e1_all, v1_scr, v2_scr, top_scr, a0_all, a1_all, acc_all) = refs
        fg_ref = None
    step = pl.program_id(2)
    sub = pl.program_id(3)
    ht_scr, rk_scr, e2_scr, n_scr, e1_scr, acc_scr = (r.at[sub] for r in (ht_all, rk_all, e2_all, n_all, e1_all, acc_all))
    tokens = ht_all.shape[2]
    tok_rows = pl.ds(pl.multiple_of(sub * tokens, tokens), tokens)

    @pl.when(step == 0)
    def _():
        h = _norm_mod(x_ref[0, tok_rows, :], g_ref[...], sh_ref[0], sc_ref[0])
        ht = jnp.transpose(h).astype(BF16)
        ht_scr[...] = ht
        q_scr[...] = _dot(wqt_ref[...], ht).astype(BF16)
        acc_scr[...] = jnp.zeros(acc_scr.shape, F32)

        def route(pair, carry):
            for k in range(2):
                _peer_route(2 * pair + k, q_scr, k1_ref, k2_ref, rk_scr, e2_scr, n_scr, e1_scr,
                            v1_scr.at[k], v2_scr.at[k], top_scr.at[k])
            return carry

        lax.fori_loop(0, PEER_HEADS // 2, route, 0)

    def project(a_write):
        a_write[...] = _dot(u_ref[...], ht_scr[...])

    def consume(a_read):
        y = None
        piece = 2 * PEER_NKEYS
        for kt in range(PEER_ROWS // 2):
            ws = []
            for r in (2 * kt, 2 * kt + 1):
                rows = slice(r * PEER_NKEYS, (r + 1) * PEER_NKEYS)
                i = (step - 1) * PEER_ROWS + r
                gate = jnp.zeros((PEER_NKEYS, tokens), BF16)
                for hd in range(PEER_HEADS):
                    n_row = _bf16_rows(n_scr[hd, pl.ds(i, 1), :], PEER_NKEYS)
                    e1_row = _bf16_rows(e1_scr[hd, pl.ds(i, 1), :], PEER_NKEYS)
                    gate = gate + jnp.where(rk_scr[hd] < n_row, e2_scr[hd] * e1_row, jnp.zeros_like(gate))
                a = a_read[rows, :].astype(BF16)
                gelu2 = a * (1.0 + lax.erf(a * np.sqrt(0.5).astype(BF16)))
                ws.append(gelu2 * gate)
            part = _dot(vt_ref[:, kt * piece:(kt + 1) * piece], jnp.concatenate(ws, axis=0))
            y = part if y is None else y + part
        acc_scr[...] += y

    last = pl.num_programs(2) - 1
    bufs = (a0_all.at[sub], a1_all.at[sub])

    @pl.when(step == 0)
    def _():
        project(bufs[0])

    for parity in (0, 1):
        @pl.when(jnp.logical_and(jnp.logical_and(step > 0, step < last), step % 2 == parity))
        def _():
            project(bufs[parity])
            consume(bufs[1 - parity])

        @pl.when(jnp.logical_and(step == last, step % 2 == parity))
        def _():
            consume(bufs[1 - parity])

    @pl.when(step == last)
    def _():
        out = x_ref[0, tok_rows, :] + gt_ref[0] * jnp.transpose(acc_scr[...])
        if final_norm:
            out = _rmsnorm(out, fg_ref[...])
        o_ref[0, tok_rows, :] = out


def _peer_ffn(x, shift, scale, gate, g, wqt, k1, k2, u, vt, final_g=None):
    b, t, d = x.shape
    tt = PEER_TOKENS
    share = PEER_SHARE if t % (PEER_SHARE * tt) == 0 else 1
    ec = PEER_ROWS * PEER_NKEYS
    n_blocks = u.shape[0] // ec
    nq = wqt.shape[0]
    final_norm = final_g is not None
    tok = pl.BlockSpec((1, share * tt, d), lambda bi, i, s, k: (bi, i, 0))
    vec = pl.BlockSpec((1, 1, d), lambda bi, i, s, k: (bi, 0, 0))
    const = lambda shape: pl.BlockSpec(shape, lambda bi, i, s, k: (0,) * len(shape))
    in_specs = [tok, vec, vec, vec, const((1, d)), const((nq, d)), const(k1.shape), const(k2.shape),
                pl.BlockSpec((ec, d), lambda bi, i, s, k: (jnp.minimum(s, n_blocks - 1), 0)),
                pl.BlockSpec((d, ec), lambda bi, i, s, k: (0, jnp.clip(s - 1, 0, n_blocks - 1)))]
    args = [x, shift, scale, gate, g, wqt, k1, k2, u, vt]
    if final_norm:
        in_specs.append(const((1, d)))
        args.append(final_g)
    table = pltpu.VMEM((share, PEER_HEADS, PEER_NKEYS, tt), jnp.uint32)
    table16 = pltpu.VMEM((share, PEER_HEADS, PEER_NKEYS, tt), BF16)
    top = pltpu.VMEM((2, PEER_TOPK, tt), F32)
    a_buf = pltpu.VMEM((share, ec, tt), F32)
    scratch = [pltpu.VMEM((share, d, tt), BF16), pltpu.VMEM((nq, tt), BF16), table16, table16, table, table,
               top, top, top, a_buf, a_buf, pltpu.VMEM((share, d, tt), F32)]
    est = (4 * share * tt * d * 4 + 2 * nq * d * 2 + 4 * ec * d * 2 + nq * tt * 2
           + share * (tt * d * 2 + 12 * PEER_HEADS * PEER_NKEYS * tt + d * tt * 4 + 2 * ec * tt * 4) + 8 * MIB)
    return pl.pallas_call(
        functools.partial(_peer_kernel, final_norm=final_norm),
        out_shape=jax.ShapeDtypeStruct((b, t, d), F32),
        grid=(b, t // (share * tt), n_blocks + 1, share),
        in_specs=in_specs,
        out_specs=tok,
        scratch_shapes=scratch,
        compiler_params=_cparams(("arbitrary",) * 4, est),
    )(*args)


def _prep_in_weights(w_in):
    d = w_in.shape[0]
    zeros = lambda n: jnp.zeros((d, n), w_in.dtype)
    cols = []
    for hd in range(4):
        cols += [w_in[:, hd * MLA_QK:(hd + 1) * MLA_QK], zeros(LANES - MLA_QK)]
    cols.append(w_in[:, 384:512])
    cols += [zeros(HEAD_DIM), w_in[:, 512:544], zeros(LANES - HEAD_DIM - MLA_ROPE)]
    for hd in (0, 2, 1, 3):
        cols.append(w_in[:, 544 + hd * HEAD_DIM:544 + (hd + 1) * HEAD_DIM])
    cols.append(w_in[:, 800:])
    return jnp.concatenate(cols, axis=1).astype(BF16)


def _prep_kv_weights(w_ukv):
    zeros = jnp.zeros((MLA_RANK, HEAD_DIM), w_ukv.dtype)
    wkn = jnp.concatenate([blk for hd in range(4) for blk in (w_ukv[:, hd * LANES:hd * LANES + HEAD_DIM], zeros)], axis=1)
    wv = jnp.concatenate([w_ukv[:, hd * LANES + HEAD_DIM:(hd + 1) * LANES] for hd in range(4)], axis=1)
    sel = np.zeros((LANES, 4 * LANES), np.float32)
    for hd in range(4):
        for r in range(MLA_ROPE):
            sel[HEAD_DIM + r, hd * LANES + HEAD_DIM + r] = 1.0
    return wkn.astype(BF16), jnp.asarray(sel, BF16), wv.astype(BF16)


def _rope_lane_tables(n):
    def tables(rot_dim):
        quarter = rot_dim // 4
        inv = ROPE_BASE ** (-jnp.arange(quarter, dtype=F32) / quarter)
        t = jnp.arange(n, dtype=jnp.int32)
        pos = jnp.stack([t // GRID_W, t % GRID_W], axis=-1).astype(F32)
        ang = pos[:, :, None] * inv
        lanes = lambda tb: jnp.concatenate([tb[:, 0], tb[:, 0], tb[:, 1], tb[:, 1]], axis=-1)
        return lanes(jnp.cos(ang)), lanes(jnp.sin(ang))

    c32, s32 = tables(MLA_ROPE)
    c64, s64 = tables(HEAD_DIM)
    ones = lambda w: jnp.ones((n, w), F32)
    zeros = lambda w: jnp.zeros((n, w), F32)
    return (jnp.concatenate([ones(HEAD_DIM), c32, ones(LANES - HEAD_DIM - MLA_ROPE)], axis=-1),
            jnp.concatenate([zeros(HEAD_DIM), s32, zeros(LANES - HEAD_DIM - MLA_ROPE)], axis=-1),
            jnp.concatenate([c64, c64], axis=-1), jnp.concatenate([s64, s64], axis=-1))


_SWA_SLOT_HEADS = (0, 2, 1, 3)


def kernel(x, c, ctx, c_ctx, ada_w, ada_b, norm_mix_g, w_in, mla_kv_norm_g, mla_w_ukv, swa_sink, na_rpb, ret_decay_f, ret_decay_b, ret_gn_f, ret_gn_b, mix_beta, w_out, norm_ffn_g, peer_wq, peer_k1, peer_k2, peer_u, peer_v, final_norm_g):
    b, n, d = x.shape
    n_ctx = ctx.shape[1]
    depth = ada_w.shape[0]
    rows = n // GRID_W

    pad_rows = -(b + 1) % 8
    cc = jnp.concatenate([c, c_ctx[None, :], jnp.zeros((pad_rows, d), F32)], axis=0)
    mod = _ada_modulation(cc, ada_w, ada_b)

    lat_tabs = _rope_lane_tables(n)
    ctx_tabs = (jnp.ones((n_ctx, LANES), F32), jnp.zeros((n_ctx, LANES), F32)) * 2
    slot_heads = np.asarray(_SWA_SLOT_HEADS)
    out_perm = np.arange(d)
    out_perm[256:512] = 256 + (slot_heads[:, None] * HEAD_DIM + np.arange(HEAD_DIM)[None, :]).reshape(-1)
    zero_state = jnp.zeros((b, 2, LANES, LANES), F32)
    pair_lanes = lambda p: jnp.repeat(p.astype(F32), HEAD_DIM).reshape(2, 1, LANES)

    xc = ctx
    for layer in range(depth):
        with_ctx = layer < depth - 1
        last = layer == depth - 1
        chunk = lambda k, lo, hi: mod[layer, lo:hi, k * d:(k + 1) * d][:, None, :]
        mod_l = [chunk(k, 0, b) for k in range(6)]
        mod_c = [jnp.broadcast_to(chunk(k, b, b + 1), (b, 1, d)) for k in range(6)]

        w_cols = _prep_in_weights(w_in[layer])
        wkn, emat, wv = _prep_kv_weights(mla_w_ukv[layer])
        g_mix = norm_mix_g[layer][None, :]
        kvg = mla_kv_norm_g[layer][None, :]
        p_l = _in_projection(x, mod_l[0], mod_l[1], g_mix, w_cols, kvg, wkn, emat, wv, lat_tabs)
        p_c = _in_projection(xc, mod_c[0], mod_c[1], g_mix, w_cols, kvg, wkn, emat, wv, ctx_tabs)
        mq, mk, mv, sq, sk, sv, nq, nk, nv, rq, rk, rv, gf, gb = p_l
        cmq, cmk, cmv, csq, csk, csv, cnq, cnk, cnv, crq, crk, crv, cgf, cgb = p_c

        sink_rows = jnp.broadcast_to(swa_sink[layer].astype(F32)[slot_heads][:, None], (4, LANES))
        bias = _na_bias_tables(na_rpb[layer], rows)
        dec_f, dec_b = pair_lanes(ret_decay_f[layer]), pair_lanes(ret_decay_b[layer])
        gn_f, gn_b = ret_gn_f[layer][None, :], ret_gn_b[layer][None, :]

        ya = _mla_attention(mq, cmk, cmv, mk, mv)
        yb = _swa_attention(sq, sk, sv, csk, csv, sink_rows)
        yc = _na_attention(nq, nk, nv, cnk, cnv, bias)
        cyf, cyb, s_f, s_b = _retention(crq, crk, crv, cgf, cgb, dec_f, dec_b, gn_f, gn_b, zero_state, zero_state)
        ydf, ydb, _, _ = _retention(rq, rk, rv, gf, gb, dec_f, dec_b, gn_f, gn_b, s_f, s_b)

        beta = mix_beta[layer][out_perm][None, :]
        w_o = w_out[layer][out_perm, :].astype(BF16)
        x = _out_projection(x, ya, yb, yc, ydf, ydb, beta, w_o, mod_l[2])

        g_ffn = norm_ffn_g[layer][None, :]
        wqt = peer_wq[layer].T.astype(BF16)
        k1, k2 = peer_k1[layer].astype(BF16), peer_k2[layer].astype(BF16)
        u, vt = peer_u[layer].astype(BF16), peer_v[layer].T.astype(BF16)
        x = _peer_ffn(x, mod_l[3], mod_l[4], mod_l[5], g_ffn, wqt, k1, k2, u, vt,
                      final_norm_g[None, :] if last else None)
        if with_ctx:
            cya, cyb_, cyc = _ctx_attention(cmq, cmk, cmv, csq, csk, csv, sink_rows, cnq, cnk, cnv)
            xc = _out_projection(xc, cya, cyb_, cyc, cyf, cyb, beta, w_o, mod_c[2])
            xc = _peer_ffn(xc, mod_c[3], mod_c[4], mod_c[5], g_ffn, wqt, k1, k2, u, vt)
    return x
```

```python
import functools

import numpy as np
import jax
import jax.numpy as jnp
from jax import lax
from jax.experimental import pallas as pl
from jax.experimental.pallas import tpu as pltpu

F32 = jnp.float32
BF16 = jnp.bfloat16

EPS = 1e-6
NEG_INF = -1e30
ROPE_BASE = 10000.0
GRID_W = 64
HEAD_DIM = 64
LANES = 128
V7X_VMEM_BYTES = 64 * 1024 * 1024
MIB = 1024 * 1024

MLA_ROPE = 32
MLA_QK = 96
MLA_RANK = 128
SWA_WINDOW = 128
NA_KH, NA_KW = 8, 16
NA_ROWS = 4
RET_CHUNK = 256
RET_BATCH = 2
PEER_HEADS = 8
PEER_NKEYS = 128
PEER_TOPK = 16
PEER_TOKENS = 256
PEER_ROWS = 16
PEER_SHARE = 2

_MQ, _CKV, _KR, _SQ, _SK, _SV, _NQ, _NK, _NV, _RQ, _RK, _RV, _GF, _GB, _IN_COLS = (
    0, 512, 640, 768, 1024, 1152, 1280, 1536, 1792, 2048, 2304, 2560, 2816, 3072, 3328)


def _cparams(semantics, vmem_bytes):
    limit = int(min(vmem_bytes, V7X_VMEM_BYTES * 7 // 8))
    return pltpu.CompilerParams(dimension_semantics=semantics, vmem_limit_bytes=limit)


def _dot(a, b):
    return jnp.dot(a, b, preferred_element_type=F32)


def _dot_nt(a, b):
    return lax.dot_general(a, b, (((1,), (1,)), ((), ())), preferred_element_type=F32)


def _dot_tn(a, b):
    return lax.dot_general(a, b, (((0,), (0,)), ((), ())), preferred_element_type=F32)


def _low_half(shape):
    return lax.broadcasted_iota(jnp.int32, shape, len(shape) - 1) < HEAD_DIM


def _half_masks():
    lo = _low_half((1, LANES))
    return (jnp.where(lo, 1.0, 0.0).astype(BF16), jnp.where(lo, 0.0, 1.0).astype(BF16))


def _norm_mod(x, g, shift, scale):
    ms = jnp.mean(x * x, axis=-1, keepdims=True)
    return (x * lax.rsqrt(ms + EPS) * g) * (1.0 + scale) + shift


def _rmsnorm(x, g):
    ms = jnp.mean(x * x, axis=-1, keepdims=True)
    return x * lax.rsqrt(ms + EPS) * g


def _ada_kernel(c_ref, w_ref, b_ref, o_ref):
    c = c_ref[...]
    o_ref[0] = _dot(c * jax.nn.sigmoid(c), w_ref[0]) + b_ref[0]


def _ada_modulation(cc, ada_w, ada_b):
    depth, d, width = ada_w.shape
    rows = cc.shape[0]
    tn = 1024
    return pl.pallas_call(
        _ada_kernel,
        out_shape=jax.ShapeDtypeStruct((depth, rows, width), F32),
        grid=(depth, width // tn),
        in_specs=[pl.BlockSpec((rows, d), lambda l, j: (0, 0)),
                  pl.BlockSpec((1, d, tn), lambda l, j: (l, 0, j)),
                  pl.BlockSpec((1, 1, tn), lambda l, j: (l, 0, j))],
        out_specs=pl.BlockSpec((1, rows, tn), lambda l, j: (l, 0, j)),
        compiler_params=_cparams(("arbitrary", "arbitrary"), 2 * d * tn * 4 + 8 * MIB),
    )(cc, ada_w, ada_b.reshape(depth, 1, width))


def _rope(a, cos, sin, half):
    lane = lax.broadcasted_iota(jnp.int32, a.shape, 1)
    first = (lane % (2 * half)) < half
    rot = jnp.where(first, -pltpu.roll(a, LANES - half, 1), pltpu.roll(a, half, 1))
    return a * cos + rot * sin


def _inproj_kernel(x_ref, sh_ref, sc_ref, g_ref, w_ref, kvg_ref, wkn_ref, e_ref, wv_ref,
                   c32_ref, s32_ref, c64_ref, s64_ref,
                   mq_ref, mk_ref, mv_ref, sq_ref, sk_ref, sv_ref, nq_ref, nk_ref, nv_ref,
                   rq_ref, rk_ref, rv_ref, gf_ref, gb_ref):
    h = _norm_mod(x_ref[0], g_ref[...], sh_ref[0], sc_ref[0]).astype(BF16)

    def seg(lo, width):
        return _dot(h, w_ref[:, lo:lo + width])

    c32, s32, c64, s64 = c32_ref[...], s32_ref[...], c64_ref[...], s64_ref[...]
    mla_scale = MLA_QK ** -0.5
    head_scale = HEAD_DIM ** -0.5
    for hd in range(4):
        a = seg(_MQ + hd * LANES, LANES)
        mq_ref[0, :, hd * LANES:(hd + 1) * LANES] = (_rope(a, c32, s32, MLA_ROPE // 4) * mla_scale).astype(BF16)
    kvn = _rmsnorm(seg(_CKV, MLA_RANK), kvg_ref[...]).astype(BF16)
    kr = _rope(seg(_KR, LANES), c32, s32, MLA_ROPE // 4).astype(BF16)
    mk_ref[0] = (_dot(kvn, wkn_ref[...]) + _dot(kr, e_ref[...])).astype(BF16)
    mv_ref[0] = _dot(kvn, wv_ref[...]).astype(BF16)
    for grp in range(2):
        a = seg(_SQ + grp * LANES, LANES)
        sq_ref[0, :, grp * LANES:(grp + 1) * LANES] = (_rope(a, c64, s64, HEAD_DIM // 4) * head_scale).astype(BF16)
    sk_ref[0] = _rope(seg(_SK, LANES), c64, s64, HEAD_DIM // 4).astype(BF16)
    sv_ref[0] = seg(_SV, LANES).astype(BF16)
    nq_ref[0] = (seg(_NQ, 256) * head_scale).astype(BF16)
    nk_ref[0] = seg(_NK, 256).astype(BF16)
    nv_ref[0] = seg(_NV, 256).astype(BF16)
    rq_ref[0] = seg(_RQ, 256).astype(BF16)
    rk_ref[0] = (seg(_RK, 256) * head_scale).astype(BF16)
    rv_ref[0] = seg(_RV, 256).astype(BF16)
    gf_ref[0] = seg(_GF, 256).astype(BF16)
    gb_ref[0] = seg(_GB, 256).astype(BF16)


_INPROJ_WIDTHS = (512, 512, 256, 256, 128, 128, 256, 256, 256, 256, 256, 256, 256, 256)


def _in_projection(x, shift, scale, g, w, kvg, wkn, emat, wv, tabs):
    b, t, d = x.shape
    tm = min(512, t)
    const = lambda shape: pl.BlockSpec(shape, lambda bi, i: (0,) * len(shape))
    tab = pl.BlockSpec((tm, LANES), lambda bi, i: (i, 0))
    vec = pl.BlockSpec((1, 1, d), lambda bi, i: (bi, 0, 0))
    est = 2 * (tm * d * 4 + d * _IN_COLS * 2 + tm * sum(_INPROJ_WIDTHS) * 2 + 4 * tm * LANES * 4) + 12 * MIB
    return pl.pallas_call(
        _inproj_kernel,
        out_shape=[jax.ShapeDtypeStruct((b, t, wd), BF16) for wd in _INPROJ_WIDTHS],
        grid=(b, t // tm),
        in_specs=[pl.BlockSpec((1, tm, d), lambda bi, i: (bi, i, 0)), vec, vec, const((1, d)),
                  const((d, _IN_COLS)), const((1, MLA_RANK)), const((MLA_RANK, 512)), const((LANES, 512)),
                  const((MLA_RANK, 256)), tab, tab, tab, tab],
        out_specs=[pl.BlockSpec((1, tm, wd), lambda bi, i: (bi, i, 0)) for wd in _INPROJ_WIDTHS],
        compiler_params=_cparams(("arbitrary", "arbitrary"), est),
    )(x, shift, scale, g, w, kvg, wkn, emat, wv, *tabs)


def _attend_pair(qs, pieces, sinks):
    lo = _low_half((1, LANES))
    m0, m1 = _half_masks()
    outs = []
    for s in (0, 1):
        own, other = (m0, m1) if s == 0 else (m1, m0)
        scores = []
        for ks, _, biases in pieces:
            sc = _dot_nt(qs[s], ks[s])
            if biases[s] is not None:
                sc = sc + biases[s]
            scores.append(sc)
        m = functools.reduce(jnp.maximum, [jnp.max(sc, axis=-1, keepdims=True) for sc in scores])
        if sinks is not None:
            m = jnp.maximum(m, sinks[s])
        o = 0.0
        for sc, (_, v, _) in zip(scores, pieces):
            o = o + _dot(jnp.exp(sc - m).astype(BF16), v * own + other)
        own_lanes = lo if s == 0 else jnp.logical_not(lo)
        den = jnp.where(own_lanes, 1.0, o)
        if sinks is not None:
            den = den + jnp.exp(sinks[s] - m)
        outs.append(o * pltpu.roll(1.0 / den, HEAD_DIM, 1))
    return jnp.where(lo, outs[0], outs[1])


def _mla_kernel(q_ref, kc_ref, vc_ref, kl_ref, vl_ref, o_ref, m_ref, l_ref, acc_ref):
    j = pl.program_id(2)

    @pl.when(j == 0)
    def _():
        m_ref[...] = jnp.full(m_ref.shape, NEG_INF, F32)
        l_ref[...] = jnp.zeros(l_ref.shape, F32)
        acc_ref[...] = jnp.zeros(acc_ref.shape, F32)

    def update(k, v):
        lo = _low_half((1, LANES))
        m0, m1 = _half_masks()
        reps = k.shape[0] // LANES
        for pair in range(2):
            vp = v[:, pair * LANES:(pair + 1) * LANES]
            vals = (vp * m0 + m1, vp * m1 + m0)
            alphas, pvs = [], []
            for s in (0, 1):
                hd = 2 * pair + s
                q = q_ref[0, :, hd * LANES:(hd + 1) * LANES]
                sc = _dot_nt(q, k[:, hd * LANES:(hd + 1) * LANES])
                m_prev = m_ref[hd]
                m_new = jnp.maximum(m_prev, jnp.max(sc, axis=-1, keepdims=True))
                m_ref[hd] = m_new
                alphas.append(jnp.exp(m_prev - m_new))
                p = jnp.exp(sc - jnp.tile(m_new, (1, reps)))
                pvs.append(_dot(p.astype(BF16), vals[s]))
            acc_ref[pair] = (acc_ref[pair] * jnp.where(lo, alphas[0], alphas[1])
                             + jnp.where(lo, pvs[0], pvs[1]))
            l_ref[pair] = (l_ref[pair] * jnp.where(lo, alphas[1], alphas[0])
                           + jnp.where(lo, pvs[1], pvs[0]))

    @pl.when(j == 0)
    def _():
        update(kc_ref[0], vc_ref[0])

    @pl.when(j > 0)
    def _():
        update(kl_ref[0], vl_ref[0])

    @pl.when(j == pl.num_programs(2) - 1)
    def _():
        for pair in range(2):
            inv = pltpu.roll(1.0 / l_ref[pair], HEAD_DIM, 1)
            o_ref[0, :, pair * LANES:(pair + 1) * LANES] = (acc_ref[pair] * inv).astype(BF16)


def _mla_attention(q, kc, vc, kl, vl):
    b, n, _ = q.shape
    c = kc.shape[1]
    tq = min(1024, n)
    tk = min(1024, n)
    est = 2 * (tq * 512 * 2 + c * 768 * 2 + tk * 768 * 2 + tq * 256 * 2) + 6 * tq * LANES * 4 + 10 * tq * tk * 4 + 8 * MIB
    return pl.pallas_call(
        _mla_kernel,
        out_shape=jax.ShapeDtypeStruct((b, n, 256), BF16),
        grid=(b, n // tq, 1 + n // tk),
        in_specs=[pl.BlockSpec((1, tq, 512), lambda bi, i, j: (bi, i, 0)),
                  pl.BlockSpec((1, c, 512), lambda bi, i, j: (bi, 0, 0)),
                  pl.BlockSpec((1, c, 256), lambda bi, i, j: (bi, 0, 0)),
                  pl.BlockSpec((1, tk, 512), lambda bi, i, j: (bi, jnp.maximum(j - 1, 0), 0)),
                  pl.BlockSpec((1, tk, 256), lambda bi, i, j: (bi, jnp.maximum(j - 1, 0), 0))],
        out_specs=pl.BlockSpec((1, tq, 256), lambda bi, i, j: (bi, i, 0)),
        scratch_shapes=[pltpu.VMEM((4, tq, LANES), F32), pltpu.VMEM((2, tq, LANES), F32), pltpu.VMEM((2, tq, LANES), F32)],
        compiler_params=_cparams(("arbitrary", "arbitrary", "arbitrary"), est),
    )(q, kc, vc, kl, vl)


def _swa_kernel(q_ref, kp_ref, kcur_ref, kn_ref, vp_ref, vcur_ref, vn_ref, kctx_ref, vctx_ref, sink_ref, o_ref):
    i = pl.program_id(1)
    last = pl.num_programs(1) - 1
    tq = q_ref.shape[1]

    def band_bias(width, offset, edge_penalty):
        r = lax.broadcasted_iota(jnp.int32, (tq, width), 0)
        cidx = lax.broadcasted_iota(jnp.int32, (tq, width), 1)
        rel = cidx + offset - r
        inside = jnp.where(rel >= -SWA_WINDOW, jnp.where(rel <= SWA_WINDOW, 1, 0), 0)
        return jnp.where(inside == 1, edge_penalty, NEG_INF)

    halo = kp_ref.shape[1]
    b_prev = band_bias(halo, -halo, jnp.where(i > 0, 0.0, NEG_INF))
    b_cur = band_bias(tq, 0, 0.0)
    b_next = band_bias(halo, tq, jnp.where(i < last, 0.0, NEG_INF))
    m0, m1 = _half_masks()
    kctx, vctx = kctx_ref[0], vctx_ref[0]
    pieces = [((kp_ref[0],) * 2, vp_ref[0], (b_prev,) * 2),
              ((kcur_ref[0],) * 2, vcur_ref[0], (b_cur,) * 2),
              ((kn_ref[0],) * 2, vn_ref[0], (b_next,) * 2),
              ((kctx,) * 2, vctx, (None, None))]
    for grp in range(2):
        qg = q_ref[0, :, grp * LANES:(grp + 1) * LANES]
        sinks = [sink_ref[2 * grp + s:2 * grp + s + 1, 0:1] for s in (0, 1)]
        o_ref[0, :, grp * LANES:(grp + 1) * LANES] = _attend_pair((qg * m0, qg * m1), pieces, sinks).astype(BF16)


def _na_kernel(q_ref, kp_ref, kcur_ref, kn_ref, vp_ref, vcur_ref, vn_ref, kctx_ref, vctx_ref, bias_ref, o_ref):
    tq = q_ref.shape[1]
    m0, m1 = _half_masks()
    for pair in range(2):
        sl = slice(pair * LANES, (pair + 1) * LANES)
        qp = q_ref[0, :, sl]
        pieces = []
        for idx, (kr, vr) in enumerate(((kp_ref, vp_ref), (kcur_ref, vcur_ref), (kn_ref, vn_ref))):
            biases = tuple(bias_ref[0, 2 * pair + s, :, idx * tq:(idx + 1) * tq] for s in (0, 1))
            pieces.append(((kr[0, :, sl],) * 2, vr[0, :, sl], biases))
        pieces.append(((kctx_ref[0, :, sl],) * 2, vctx_ref[0, :, sl], (None, None)))
        o_ref[0, :, sl] = _attend_pair((qp * m0, qp * m1), pieces, None).astype(BF16)


def _na_bias_tables(rpb, rows):
    nb = rows // NA_ROWS
    heads = rpb.shape[0]
    width = 2 * NA_KW - 1
    span = 2 * GRID_W - 1
    lead = GRID_W - 2 - (NA_KW - 1)
    rpb = rpb.astype(F32)
    per_row = jnp.stack([rpb[:, NA_ROWS - 1 - a:NA_ROWS - 1 - a + 3 * NA_ROWS, :] for a in range(NA_ROWS)], axis=1)
    padded = jnp.pad(per_row, ((0, 0), (0, 0), (0, 0), (lead, span - width - lead)))
    tiled = jnp.broadcast_to(padded[:, :, :, None, :], (heads, NA_ROWS, 3 * NA_ROWS, GRID_W, span))
    skew = tiled.reshape(heads, NA_ROWS, 3 * NA_ROWS, GRID_W * span)[..., :GRID_W * (span - 1)]
    toep = skew.reshape(heads, NA_ROWS, 3 * NA_ROWS, GRID_W, span - 1)[..., GRID_W - 2:]
    table = jnp.transpose(toep, (0, 1, 3, 2, 4)).reshape(heads, NA_ROWS * GRID_W, 3 * NA_ROWS * GRID_W)
    a = np.arange(NA_ROWS)[:, None, None, None]
    cq = np.arange(GRID_W)[None, :, None, None]
    kr = np.arange(3 * NA_ROWS)[None, None, :, None]
    ck = np.arange(GRID_W)[None, None, None, :]
    valids = []
    for j in (0, 1, nb - 1):
        r = NA_ROWS * j + a
        rs = np.clip(r - NA_KH // 2, 0, rows - NA_KH)
        rk = NA_ROWS * (j - 1) + kr
        cs = np.clip(cq - NA_KW // 2, 0, GRID_W - NA_KW)
        valid = (rk >= rs) & (rk < rs + NA_KH) & (ck >= cs) & (ck < cs + NA_KW)
        valids.append(valid.reshape(NA_ROWS * GRID_W, 3 * NA_ROWS * GRID_W))
    valid = jnp.asarray(np.stack(valids))
    return jnp.where(valid[:, None], table[None], NEG_INF)


def _local_kernel(*refs):
    _swa_kernel(*refs[:10], refs[20])
    _na_kernel(*refs[10:20], refs[21])


def _local_attention(sq, sk, sv, skc, svc, sink_rows, nq, nk, nv, nkc, nvc, bias):
    b, n, _ = sq.shape
    c = skc.shape[1]
    tq = NA_ROWS * GRID_W
    nb = n // tq
    halo = SWA_WINDOW
    r = tq // halo
    nh = n // halo
    cur = lambda bi, i: (bi, i, 0)
    prev = lambda bi, i: (bi, jnp.maximum(i - 1, 0), 0)
    nxt = lambda bi, i: (bi, jnp.minimum(i + 1, nb - 1), 0)
    hprev = lambda bi, i: (bi, jnp.maximum(i * r - 1, 0), 0)
    hnxt = lambda bi, i: (bi, jnp.minimum(i * r + r, nh - 1), 0)
    ctx = lambda bi, i: (bi, 0, 0)
    variant = lambda bi, i: (jnp.where(i == 0, 0, jnp.where(i == nb - 1, 2, 1)), 0, 0, 0)
    blk = pl.BlockSpec((1, tq, 256), cur)
    halo_specs = [pl.BlockSpec((1, halo, LANES), hprev), pl.BlockSpec((1, tq, LANES), cur),
                  pl.BlockSpec((1, halo, LANES), hnxt)]
    win_specs = [pl.BlockSpec((1, tq, 256), prev), blk, pl.BlockSpec((1, tq, 256), nxt)]
    in_specs = ([blk] + halo_specs + halo_specs + [pl.BlockSpec((1, c, LANES), ctx)] * 2
                + [pl.BlockSpec((4, LANES), lambda bi, i: (0, 0))]
                + [blk] + win_specs + win_specs + [pl.BlockSpec((1, c, 256), ctx)] * 2
                + [pl.BlockSpec((1, 4, tq, 3 * tq), variant)])
    est = 2 * 4 * tq * 3 * tq * 4 + 32 * tq * (3 * tq + c) * 4 + 8 * MIB
    return pl.pallas_call(
        _local_kernel,
        out_shape=[jax.ShapeDtypeStruct((b, n, 256), BF16)] * 2,
        grid=(b, nb),
        in_specs=in_specs,
        out_specs=[blk, blk],
        compiler_params=_cparams(("arbitrary", "arbitrary"), est),
    )(sq, sk, sk, sk, sv, sv, sv, skc, svc, sink_rows, nq, nk, nk, nk, nv, nv, nv, nkc, nvc, bias)


def _ctx_attn_kernel(mq_ref, mk_ref, mv_ref, sq_ref, sk_ref, sv_ref, sink_ref, nq_ref, nk_ref, nv_ref,
                     ya_ref, yb_ref, yc_ref):
    m0, m1 = _half_masks()
    none2 = (None, None)
    for pair in range(2):
        sl = slice(pair * LANES, (pair + 1) * LANES)
        h0 = slice(2 * pair * LANES, (2 * pair + 1) * LANES)
        h1 = slice((2 * pair + 1) * LANES, (2 * pair + 2) * LANES)
        ya_ref[0, :, sl] = _attend_pair((mq_ref[0, :, h0], mq_ref[0, :, h1]),
                                        [((mk_ref[0, :, h0], mk_ref[0, :, h1]), mv_ref[0, :, sl], none2)],
                                        None).astype(BF16)
        qg = sq_ref[0, :, sl]
        sinks = [sink_ref[2 * pair + s:2 * pair + s + 1, 0:1] for s in (0, 1)]
        yb_ref[0, :, sl] = _attend_pair((qg * m0, qg * m1), [((sk_ref[0],) * 2, sv_ref[0], none2)],
                                        sinks).astype(BF16)
        qn = nq_ref[0, :, sl]
        yc_ref[0, :, sl] = _attend_pair((qn * m0, qn * m1), [((nk_ref[0, :, sl],) * 2, nv_ref[0, :, sl], none2)],
                                        None).astype(BF16)


def _ctx_attention(mq, mk, mv, sq, sk, sv, sink_rows, nq, nk, nv):
    b, c, _ = mq.shape
    spec = lambda wd: pl.BlockSpec((1, c, wd), lambda bi: (bi, 0, 0))
    return pl.pallas_call(
        _ctx_attn_kernel,
        out_shape=[jax.ShapeDtypeStruct((b, c, 256), BF16)] * 3,
        grid=(b,),
        in_specs=[spec(512), spec(512), spec(256), spec(256), spec(LANES), spec(LANES),
                  pl.BlockSpec((4, LANES), lambda bi: (0, 0)), spec(256), spec(256), spec(256)],
        out_specs=[spec(256)] * 3,
        compiler_params=_cparams(("arbitrary",), 24 * MIB),
    )(mq, mk, mv, sq, sk, sv, sink_rows, nq, nk, nv)


def _ret_tables(dec_lane, reverse, c):
    lg = -(jnp.maximum(-dec_lane, 0.0) + jnp.log1p(jnp.exp(-jnp.abs(dec_lane))))
    ti = lax.broadcasted_iota(jnp.int32, (c, LANES), 0).astype(F32)
    if reverse:
        qpow, kpow = c - ti, ti
    else:
        qpow, kpow = ti + 1.0, c - 1.0 - ti
    ii = lax.broadcasted_iota(jnp.int32, (c, c), 0)
    jj = lax.broadcasted_iota(jnp.int32, (c, c), 1)
    dist = (jj - ii) if reverse else (ii - jj)
    distf = jnp.maximum(dist, 0).astype(F32)
    masks = [jnp.where(dist >= 0, jnp.exp(lg[:, s * HEAD_DIM:s * HEAD_DIM + 1] * distf), 0.0) for s in (0, 1)]
    return jnp.exp(lg * qpow), jnp.exp(lg * kpow), masks[0], masks[1], jnp.exp(lg * float(c))


def _ret_chunk(q, k, v, g, qdec, kdec, decays, cdec, gn_lane, s_ref):
    lo = _low_half((1, LANES))
    ii = lax.broadcasted_iota(jnp.int32, (LANES, LANES), 0)
    jj = lax.broadcasted_iota(jnp.int32, (LANES, LANES), 1)
    outs = []
    for msk, decay in zip(_half_masks(), decays):
        inner = _dot_nt(q * msk, k) * decay
        outs.append(_dot(inner.astype(BF16), v))
    state = s_ref[...]
    o = jnp.where(lo, outs[0], outs[1]) + _dot((q.astype(F32) * qdec).astype(BF16), state.astype(BF16))
    kd = (k.astype(F32) * kdec).astype(BF16)
    same_head = (ii < HEAD_DIM) == (jj < HEAD_DIM)
    s_ref[...] = state * cdec + jnp.where(same_head, _dot_tn(kd, v), 0.0)

    def head_mean(x):
        s_lo = jnp.sum(jnp.where(lo, x, 0.0), axis=-1, keepdims=True)
        s_hi = jnp.sum(jnp.where(lo, 0.0, x), axis=-1, keepdims=True)
        return jnp.where(lo, s_lo, s_hi) * (1.0 / HEAD_DIM)

    dev = o - head_mean(o)
    normed = dev * lax.rsqrt(head_mean(dev * dev) + EPS)
    gf = g.astype(F32)
    return (gf * jax.nn.sigmoid(gf)) * (normed * gn_lane)


def _ret_kernel(decf_ref, decb_ref, gnf_ref, gnb_ref, s0f_ref, s0b_ref,
                qf_ref, kf_ref, vf_ref, gf_ref, qb_ref, kb_ref, vb_ref, gb_ref,
                yf_ref, yb_ref, sf_ref, sb_ref, st_scr, tab_scr, mask_scr, cdec_scr):
    t = pl.program_id(1)
    directions = ((decf_ref, gnf_ref, qf_ref, kf_ref, vf_ref, gf_ref, yf_ref),
                  (decb_ref, gnb_ref, qb_ref, kb_ref, vb_ref, gb_ref, yb_ref))

    batch = qf_ref.shape[0]

    @pl.when(t == 0)
    def _():
        for e in range(batch):
            st_scr[e, 0] = s0f_ref[e]
            st_scr[e, 1] = s0b_ref[e]
        for pair in range(2):
            for d, refs in enumerate(directions):
                qdec, kdec, mask0, mask1, cdec = _ret_tables(refs[0][pair], d == 1, qf_ref.shape[1])
                tab_scr[d, pair, 0] = qdec
                tab_scr[d, pair, 1] = kdec
                mask_scr[d, pair, 0] = mask0
                mask_scr[d, pair, 1] = mask1
                cdec_scr[d, pair] = jnp.broadcast_to(cdec, (8, LANES))

    for e in range(batch):
        for d, (_, gn_ref, q_ref, k_ref, v_ref, g_ref, y_ref) in enumerate(directions):
            for pair in range(2):
                sl = slice(pair * LANES, (pair + 1) * LANES)
                y = _ret_chunk(q_ref[e, :, sl], k_ref[e, :, sl], v_ref[e, :, sl], g_ref[e, :, sl],
                               tab_scr[d, pair, 0], tab_scr[d, pair, 1], (mask_scr[d, pair, 0], mask_scr[d, pair, 1]),
                               cdec_scr[d, pair, 0:1, :], gn_ref[:, sl], st_scr.at[e, d, pair])
                y_ref[e, :, sl] = y.astype(BF16)

    @pl.when(t == pl.num_programs(1) - 1)
    def _():
        for e in range(batch):
            sf_ref[e] = st_scr[e, 0]
            sb_ref[e] = st_scr[e, 1]


def _retention(q, k, v, gf, gb, dec_f, dec_b, gn_f, gn_b, s0f, s0b):
    b, t, width = q.shape
    c = min(RET_CHUNK, t)
    nt = t // c
    bb = RET_BATCH if b % RET_BATCH == 0 else 1
    fwd = pl.BlockSpec((bb, c, width), lambda bi, i: (bi, i, 0))
    bwd = pl.BlockSpec((bb, c, width), lambda bi, i: (bi, nt - 1 - i, 0))
    dec = pl.BlockSpec((2, 1, LANES), lambda bi, i: (0, 0, 0))
    gn = pl.BlockSpec((1, width), lambda bi, i: (0, 0))
    st = pl.BlockSpec((bb, 2, LANES, LANES), lambda bi, i: (bi, 0, 0, 0))
    return pl.pallas_call(
        _ret_kernel,
        out_shape=[jax.ShapeDtypeStruct((b, t, width), BF16)] * 2 + [jax.ShapeDtypeStruct((b, 2, LANES, LANES), F32)] * 2,
        grid=(b // bb, nt),
        in_specs=[dec, dec, gn, gn, st, st, fwd, fwd, fwd, fwd, bwd, bwd, bwd, bwd],
        out_specs=[fwd, bwd, st, st],
        scratch_shapes=[pltpu.VMEM((bb, 2, 2, LANES, LANES), F32), pltpu.VMEM((2, 2, 2, c, LANES), F32),
                        pltpu.VMEM((2, 2, 2, c, c), F32),
                        pltpu.VMEM((2, 2, 8, LANES), F32)],
        compiler_params=_cparams(("arbitrary", "arbitrary"), 24 * MIB),
    )(dec_f, dec_b, gn_f, gn_b, s0f, s0b, q, k, v, gf, q, k, v, gb)


def _outproj_kernel(x_ref, ya_ref, yb_ref, yc_ref, ydf_ref, ydb_ref, beta_ref, w_ref, gate_ref, o_ref):
    ys = (ya_ref[0].astype(F32), yb_ref[0].astype(F32), yc_ref[0].astype(F32),
          ydf_ref[0].astype(F32) + ydb_ref[0].astype(F32))
    acc = 0.0
    for s, y in enumerate(ys):
        sl = slice(s * 256, (s + 1) * 256)
        acc = acc + _dot((y * beta_ref[:, sl]).astype(BF16), w_ref[sl, :])
    o_ref[0] = x_ref[0] + gate_ref[0] * acc


def _out_projection(x, ya, yb, yc, ydf, ydb, beta, w, gate):
    b, t, d = x.shape
    tm = min(512, t)
    row = lambda wd: pl.BlockSpec((1, tm, wd), lambda bi, i: (bi, i, 0))
    est = 2 * (2 * tm * d * 4 + 5 * tm * 256 * 2 + d * d * 2) + 8 * MIB
    return pl.pallas_call(
        _outproj_kernel,
        out_shape=jax.ShapeDtypeStruct((b, t, d), F32),
        grid=(b, t // tm),
        in_specs=[row(d), row(256), row(256), row(256), row(256), row(256),
                  pl.BlockSpec((1, d), lambda bi, i: (0, 0)), pl.BlockSpec((d, d), lambda bi, i: (0, 0)),
                  pl.BlockSpec((1, 1, d), lambda bi, i: (bi, 0, 0))],
        out_specs=row(d),
        compiler_params=_cparams(("arbitrary", "arbitrary"), est),
    )(x, ya, yb, yc, ydf, ydb, beta, w, gate)


def _sorting_network(n):
    size = 1
    while size < n:
        size *= 2
    pairs = []
    p = 1
    while p < size:
        k = p
        while k >= 1:
            for j in range(k % p, size - k, 2 * k):
                for i in range(min(k, size - j - k)):
                    if (i + j) // (2 * p) == (i + j + k) // (2 * p):
                        pairs.append((i + j, i + j + k))
            k //= 2
        p *= 2
    return [(i, j) for i, j in pairs if j < n]


def _top_values(scores, out_ref, count):
    groups = scores.shape[0] // 8
    lists = [scores[8 * g:8 * g + 8, :] for g in range(groups)]
    for i, j in _sorting_network(groups):
        lists[i], lists[j] = jnp.maximum(lists[i], lists[j]), jnp.minimum(lists[i], lists[j])
    for r in range(count):
        m = jnp.max(lists[0], axis=0, keepdims=True)
        out_ref[r:r + 1, :] = m
        hit = lists[0] == m
        need = count - r - 1
        for p in range(min(groups - 1, need)):
            lists[p] = jnp.where(hit, lists[p + 1], lists[p])
        if need >= groups:
            lists[groups - 1] = jnp.where(hit, -jnp.inf, lists[groups - 1])


def _staircase(w1, w2):
    row8 = lax.broadcasted_iota(jnp.int32, (8, w1.shape[1]), 0)
    cands = [w1[0:1] + w2]
    for a in range(1, 8):
        cands.append(jnp.where(row8 < PEER_TOPK // (a + 1), w1[a:a + 1] + w2[0:8], -jnp.inf))
    cands.append(w1[8:16] + w2[0:1])
    return jnp.concatenate(cands, axis=0)


def _bf16_twice(x):
    hi = pltpu.bitcast(x.astype(BF16).astype(F32), jnp.uint32)
    return hi | (hi >> 16)


def _bf16_rows(words, rows):
    return pltpu.bitcast(jnp.broadcast_to(words, (rows // 2, words.shape[1])), BF16)


def _peer_route(hd, q_scr, k1_ref, k2_ref, rk_scr, e2_scr, n_scr, e1_scr, v1_scr, v2_scr, top_scr):
    half = PEER_NKEYS
    base = pl.multiple_of(hd * 2 * half, 2 * half)
    s1 = _dot(k1_ref[...], q_scr[pl.ds(base, half), :])
    s2 = _dot(k2_ref[...], q_scr[pl.ds(base + half, half), :])
    _top_values(s1, v1_scr, PEER_TOPK)
    _top_values(s2, v2_scr, PEER_TOPK)
    v1, v2 = v1_scr[...], v2_scr[...]
    cand = _staircase(v1, v2)
    _top_values(cand, top_scr, PEER_TOPK)
    top = top_scr[...]
    z = jnp.sum(jnp.exp(top - top[0:1]), axis=0, keepdims=True)
    sel = jnp.where(cand >= top[PEER_TOPK - 1:PEER_TOPK], 1.0, 0.0)
    count = lambda picked: jnp.sum(picked, axis=0, keepdims=True)
    lens = [count(sel[0:16])] + [count(sel[8 + 8 * a:16 + 8 * a]) for a in range(1, 8)]
    lens += [sel[72 + a:73 + a] for a in range(8)]
    n_sel = jnp.zeros(s1.shape, F32)
    for a in reversed(range(PEER_TOPK)):
        n_sel = jnp.where(s1 >= v1[a:a + 1], lens[a], n_sel)
    rank = jnp.zeros(s2.shape, F32)
    for b in range(PEER_TOPK):
        rank = rank + jnp.where(s2 < v2[b:b + 1], 1.0, 0.0)
    rk_scr[hd] = rank.astype(BF16)
    n_scr[hd] = _bf16_twice(n_sel)
    e1_scr[hd] = _bf16_twice(jnp.exp(s1 - v1[0:1]) / z)
    e2_scr[hd] = (jnp.exp(s2 - v2[0:1]) * 0.5).astype(BF16)


def _peer_kernel(*refs, final_norm):
    if final_norm:
        (x_ref, sh_ref, sc_ref, gt_ref, g_ref, wqt_ref, k1_ref, k2_ref, u_ref, vt_ref, fg_ref, o_ref,
         ht_all, q_scr, rk_all, e2_all, n_all, e1_all, v1_scr, v2_scr, top_scr, a0_all, a1_all, acc_all) = refs
    else:
        (x_ref, sh_ref, sc_ref, gt_ref, g_ref, wqt_ref, k1_ref, k2_ref, u_ref, vt_ref, o_ref,
         ht_all, q_scr, rk_all, e2_all, n_all, e1_all, v1_scr, v2_scr, top_scr, a0_all, a1_all, acc_all) = refs
        fg_ref = None
    step = pl.program_id(2)
    sub = pl.program_id(3)
    ht_scr, rk_scr, e2_scr, n_scr, e1_scr, acc_scr = (r.at[sub] for r in (ht_all, rk_all, e2_all, n_all, e1_all, acc_all))
    tokens = ht_all.shape[2]
    tok_rows = pl.ds(pl.multiple_of(sub * tokens, tokens), tokens)

    @pl.when(step == 0)
    def _():
        h = _norm_mod(x_ref[0, tok_rows, :], g_ref[...], sh_ref[0], sc_ref[0])
        ht = jnp.transpose(h).astype(BF16)
        ht_scr[...] = ht
        q_scr[...] = _dot(wqt_ref[...], ht).astype(BF16)
        acc_scr[...] = jnp.zeros(acc_scr.shape, F32)

        def route(pair, carry):
            for k in range(2):
                _peer_route(2 * pair + k, q_scr, k1_ref, k2_ref, rk_scr, e2_scr, n_scr, e1_scr,
                            v1_scr.at[k], v2_scr.at[k], top_scr.at[k])
            return carry

        lax.fori_loop(0, PEER_HEADS // 2, route, 0)

    def project(a_write):
        a_write[...] = _dot(u_ref[...], ht_scr[...])

    def consume(a_read):
        y = None
        piece = 2 * PEER_NKEYS
        for kt in range(PEER_ROWS // 2):
            ws = []
            for r in (2 * kt, 2 * kt + 1):
                rows = slice(r * PEER_NKEYS, (r + 1) * PEER_NKEYS)
                i = (step - 1) * PEER_ROWS + r
                gate = jnp.zeros((PEER_NKEYS, tokens), BF16)
                for hd in range(PEER_HEADS):
                    n_row = _bf16_rows(n_scr[hd, pl.ds(i, 1), :], PEER_NKEYS)
                    e1_row = _bf16_rows(e1_scr[hd, pl.ds(i, 1), :], PEER_NKEYS)
                    gate = gate + jnp.where(rk_scr[hd] < n_row, e2_scr[hd] * e1_row, jnp.zeros_like(gate))
                a = a_read[rows, :].astype(BF16)
                gelu2 = a * (1.0 + lax.erf(a * np.sqrt(0.5).astype(BF16)))
                ws.append(gelu2 * gate)
            part = _dot(vt_ref[:, kt * piece:(kt + 1) * piece], jnp.concatenate(ws, axis=0))
            y = part if y is None else y + part
        acc_scr[...] += y

    last = pl.num_programs(2) - 1
    bufs = (a0_all.at[sub], a1_all.at[sub])

    @pl.when(step == 0)
    def _():
        project(bufs[0])

    for parity in (0, 1):
        @pl.when(jnp.logical_and(jnp.logical_and(step > 0, step < last), step % 2 == parity))
        def _():
            project(bufs[parity])
            consume(bufs[1 - parity])

        @pl.when(jnp.logical_and(step == last, step % 2 == parity))
        def _():
            consume(bufs[1 - parity])

    @pl.when(step == last)
    def _():
        out = x_ref[0, tok_rows, :] + gt_ref[0] * jnp.transpose(acc_scr[...])
        if final_norm:
            out = _rmsnorm(out, fg_ref[...])
        o_ref[0, tok_rows, :] = out


def _peer_ffn(x, shift, scale, gate, g, wqt, k1, k2, u, vt, final_g=None):
    b, t, d = x.shape
    tt = PEER_TOKENS
    share = PEER_SHARE if t % (PEER_SHARE * tt) == 0 else 1
    ec = PEER_ROWS * PEER_NKEYS
    n_blocks = u.shape[0] // ec
    nq = wqt.shape[0]
    final_norm = final_g is not None
    tok = pl.BlockSpec((1, share * tt, d), lambda bi, i, s, k: (bi, i, 0))
    vec = pl.BlockSpec((1, 1, d), lambda bi, i, s, k: (bi, 0, 0))
    const = lambda shape: pl.BlockSpec(shape, lambda bi, i, s, k: (0,) * len(shape))
    in_specs = [tok, vec, vec, vec, const((1, d)), const((nq, d)), const(k1.shape), const(k2.shape),
                pl.BlockSpec((ec, d), lambda bi, i, s, k: (jnp.minimum(s, n_blocks - 1), 0)),
                pl.BlockSpec((d, ec), lambda bi, i, s, k: (0, jnp.clip(s - 1, 0, n_blocks - 1)))]
    args = [x, shift, scale, gate, g, wqt, k1, k2, u, vt]
    if final_norm:
        in_specs.append(const((1, d)))
        args.append(final_g)
    table = pltpu.VMEM((share, PEER_HEADS, PEER_NKEYS, tt), jnp.uint32)
    table16 = pltpu.VMEM((share, PEER_HEADS, PEER_NKEYS, tt), BF16)
    top = pltpu.VMEM((2, PEER_TOPK, tt), F32)
    a_buf = pltpu.VMEM((share, ec, tt), F32)
    scratch = [pltpu.VMEM((share, d, tt), BF16), pltpu.VMEM((nq, tt), BF16), table16, table16, table, table,
               top, top, top, a_buf, a_buf, pltpu.VMEM((share, d, tt), F32)]
    est = (4 * share * tt * d * 4 + 2 * nq * d * 2 + 4 * ec * d * 2 + nq * tt * 2
           + share * (tt * d * 2 + 12 * PEER_HEADS * PEER_NKEYS * tt + d * tt * 4 + 2 * ec * tt * 4) + 8 * MIB)
    return pl.pallas_call(
        functools.partial(_peer_kernel, final_norm=final_norm),
        out_shape=jax.ShapeDtypeStruct((b, t, d), F32),
        grid=(b, t // (share * tt), n_blocks + 1, share),
        in_specs=in_specs,
        out_specs=tok,
        scratch_shapes=scratch,
        compiler_params=_cparams(("arbitrary",) * 4, est),
    )(*args)


def _prep_in_weights(w_in):
    d = w_in.shape[0]
    zeros = lambda n: jnp.zeros((d, n), w_in.dtype)
    cols = []
    for hd in range(4):
        cols += [w_in[:, hd * MLA_QK:(hd + 1) * MLA_QK], zeros(LANES - MLA_QK)]
    cols.append(w_in[:, 384:512])
    cols += [zeros(HEAD_DIM), w_in[:, 512:544], zeros(LANES - HEAD_DIM - MLA_ROPE)]
    for hd in (0, 2, 1, 3):
        cols.append(w_in[:, 544 + hd * HEAD_DIM:544 + (hd + 1) * HEAD_DIM])
    cols.append(w_in[:, 800:])
    return jnp.concatenate(cols, axis=1).astype(BF16)


def _prep_kv_weights(w_ukv):
    zeros = jnp.zeros((MLA_RANK, HEAD_DIM), w_ukv.dtype)
    wkn = jnp.concatenate([blk for hd in range(4) for blk in (w_ukv[:, hd * LANES:hd * LANES + HEAD_DIM], zeros)], axis=1)
    wv = jnp.concatenate([w_ukv[:, hd * LANES + HEAD_DIM:(hd + 1) * LANES] for hd in range(4)], axis=1)
    sel = np.zeros((LANES, 4 * LANES), np.float32)
    for hd in range(4):
        for r in range(MLA_ROPE):
            sel[HEAD_DIM + r, hd * LANES + HEAD_DIM + r] = 1.0
    return wkn.astype(BF16), jnp.asarray(sel, BF16), wv.astype(BF16)


def _rope_lane_tables(n):
    def tables(rot_dim):
        quarter = rot_dim // 4
        inv = ROPE_BASE ** (-jnp.arange(quarter, dtype=F32) / quarter)
        t = jnp.arange(n, dtype=jnp.int32)
        pos = jnp.stack([t // GRID_W, t % GRID_W], axis=-1).astype(F32)
        ang = pos[:, :, None] * inv
        lanes = lambda tb: jnp.concatenate([tb[:, 0], tb[:, 0], tb[:, 1], tb[:, 1]], axis=-1)
        return lanes(jnp.cos(ang)), lanes(jnp.sin(ang))

    c32, s32 = tables(MLA_ROPE)
    c64, s64 = tables(HEAD_DIM)
    ones = lambda w: jnp.ones((n, w), F32)
    zeros = lambda w: jnp.zeros((n, w), F32)
    return (jnp.concatenate([ones(HEAD_DIM), c32, ones(LANES - HEAD_DIM - MLA_ROPE)], axis=-1),
            jnp.concatenate([zeros(HEAD_DIM), s32, zeros(LANES - HEAD_DIM - MLA_ROPE)], axis=-1),
            jnp.concatenate([c64, c64], axis=-1), jnp.concatenate([s64, s64], axis=-1))


_SWA_SLOT_HEADS = (0, 2, 1, 3)


def kernel(x, c, ctx, c_ctx, ada_w, ada_b, norm_mix_g, w_in, mla_kv_norm_g, mla_w_ukv, swa_sink, na_rpb, ret_decay_f, ret_decay_b, ret_gn_f, ret_gn_b, mix_beta, w_out, norm_ffn_g, peer_wq, peer_k1, peer_k2, peer_u, peer_v, final_norm_g):
    b, n, d = x.shape
    n_ctx = ctx.shape[1]
    depth = ada_w.shape[0]
    rows = n // GRID_W

    pad_rows = -(b + 1) % 8
    cc = jnp.concatenate([c, c_ctx[None, :], jnp.zeros((pad_rows, d), F32)], axis=0)
    mod = _ada_modulation(cc, ada_w, ada_b)

    lat_tabs = _rope_lane_tables(n)
    ctx_tabs = (jnp.ones((n_ctx, LANES), F32), jnp.zeros((n_ctx, LANES), F32)) * 2
    slot_heads = np.asarray(_SWA_SLOT_HEADS)
    out_perm = np.arange(d)
    out_perm[256:512] = 256 + (slot_heads[:, None] * HEAD_DIM + np.arange(HEAD_DIM)[None, :]).reshape(-1)
    zero_state = jnp.zeros((b, 2, LANES, LANES), F32)
    pair_lanes = lambda p: jnp.repeat(p.astype(F32), HEAD_DIM).reshape(2, 1, LANES)

    xc = ctx
    for layer in range(depth):
        with_ctx = layer < depth - 1
        last = layer == depth - 1
        chunk = lambda k, lo, hi: mod[layer, lo:hi, k * d:(k + 1) * d][:, None, :]
        mod_l = [chunk(k, 0, b) for k in range(6)]
        mod_c = [jnp.broadcast_to(chunk(k, b, b + 1), (b, 1, d)) for k in range(6)]

        w_cols = _prep_in_weights(w_in[layer])
        wkn, emat, wv = _prep_kv_weights(mla_w_ukv[layer])
        g_mix = norm_mix_g[layer][None, :]
        kvg = mla_kv_norm_g[layer][None, :]
        p_l = _in_projection(x, mod_l[0], mod_l[1], g_mix, w_cols, kvg, wkn, emat, wv, lat_tabs)
        p_c = _in_projection(xc, mod_c[0], mod_c[1], g_mix, w_cols, kvg, wkn, emat, wv, ctx_tabs)
        mq, mk, mv, sq, sk, sv, nq, nk, nv, rq, rk, rv, gf, gb = p_l
        cmq, cmk, cmv, csq, csk, csv, cnq, cnk, cnv, crq, crk, crv, cgf, cgb = p_c

        sink_rows = jnp.broadcast_to(swa_sink[layer].astype(F32)[slot_heads][:, None], (4, LANES))
        bias = _na_bias_tables(na_rpb[layer], rows)
        dec_f, dec_b = pair_lanes(ret_decay_f[layer]), pair_lanes(ret_decay_b[layer])
        gn_f, gn_b = ret_gn_f[layer][None, :], ret_gn_b[layer][None, :]

        ya = _mla_attention(mq, cmk, cmv, mk, mv)
        yb, yc = _local_attention(sq, sk, sv, csk, csv, sink_rows, nq, nk, nv, cnk, cnv, bias)
        cyf, cyb, s_f, s_b = _retention(crq, crk, crv, cgf, cgb, dec_f, dec_b, gn_f, gn_b, zero_state, zero_state)
        ydf, ydb, _, _ = _retention(rq, rk, rv, gf, gb, dec_f, dec_b, gn_f, gn_b, s_f, s_b)

        beta = mix_beta[layer][out_perm][None, :]
        w_o = w_out[layer][out_perm, :].astype(BF16)
        x = _out_projection(x, ya, yb, yc, ydf, ydb, beta, w_o, mod_l[2])

        g_ffn = norm_ffn_g[layer][None, :]
        wqt = peer_wq[layer].T.astype(BF16)
        k1, k2 = peer_k1[layer].astype(BF16), peer_k2[layer].astype(BF16)
        u, vt = peer_u[layer].astype(BF16), peer_v[layer].T.astype(BF16)
        x = _peer_ffn(x, mod_l[3], mod_l[4], mod_l[5], g_ffn, wqt, k1, k2, u, vt,
                      final_norm_g[None, :] if last else None)
        if with_ctx:
            cya, cyb_, cyc = _ctx_attention(cmq, cmk, cmv, csq, csk, csv, sink_rows, cnq, cnk, cnv)
            xc = _out_projection(xc, cya, cyb_, cyc, cyf, cyb, beta, w_o, mod_c[2])
            xc = _peer_ffn(xc, mod_c[3], mod_c[4], mod_c[5], g_ffn, wqt, k1, k2, u, vt)
    return x
```

```python
import functools

import numpy as np
import jax
import jax.numpy as jnp
from jax import lax
from jax.experimental import pallas as pl
from jax.experimental.pallas import tpu as pltpu

F32 = jnp.float32
BF16 = jnp.bfloat16

EPS = 1e-6
NEG_INF = -1e30
ROPE_BASE = 10000.0
GRID_W = 64
HEAD_DIM = 64
GROUP_WIDTH = 256
MLA_WIDTH = 512
LANES = 128
V7X_VMEM_BYTES = 64 * 1024 * 1024
MIB = 1024 * 1024

MLA_ROPE = 32
MLA_QK = 96
MLA_RANK = 128
SWA_WINDOW = 128
NA_KH, NA_KW = 8, 16
NA_ROWS = 4
RET_CHUNK = 256
RET_BATCH = 2
PEER_HEADS = 8
PEER_NKEYS = 128
PEER_TOPK = 16
PEER_TOKENS = 256
PEER_ROWS = 16
PEER_PIECE_ROWS = 2
PEER_SHARE = 2

_MQ, _CKV, _KR, _SQ, _SK, _SV, _NQ, _NK, _NV, _RQ, _RK, _RV, _GF, _GB, _IN_COLS = (
    0, 512, 640, 768, 1024, 1152, 1280, 1536, 1792, 2048, 2304, 2560, 2816, 3072, 3328)


def _cparams(semantics, vmem_bytes):
    limit = int(min(vmem_bytes, V7X_VMEM_BYTES * 7 // 8))
    return pltpu.CompilerParams(dimension_semantics=semantics, vmem_limit_bytes=limit)


def _dot(a, b):
    return jnp.dot(a, b, preferred_element_type=F32)


def _dot_nt(a, b):
    return lax.dot_general(a, b, (((1,), (1,)), ((), ())), preferred_element_type=F32)


def _dot_tn(a, b):
    return lax.dot_general(a, b, (((0,), (0,)), ((), ())), preferred_element_type=F32)


def _low_half(shape):
    return lax.broadcasted_iota(jnp.int32, shape, len(shape) - 1) < HEAD_DIM


def _half_masks():
    lo = _low_half((1, LANES))
    return (jnp.where(lo, 1.0, 0.0).astype(BF16), jnp.where(lo, 0.0, 1.0).astype(BF16))


def _norm_mod(x, g, shift, scale):
    ms = jnp.mean(x * x, axis=-1, keepdims=True)
    return (x * lax.rsqrt(ms + EPS) * g) * (1.0 + scale) + shift


def _rmsnorm(x, g):
    ms = jnp.mean(x * x, axis=-1, keepdims=True)
    return x * lax.rsqrt(ms + EPS) * g


def _ada_kernel(c_ref, w_ref, b_ref, o_ref):
    c = c_ref[...]
    o_ref[0] = _dot(c * jax.nn.sigmoid(c), w_ref[0]) + b_ref[0]


def _ada_modulation(cc, ada_w, ada_b):
    depth, d, width = ada_w.shape
    rows = cc.shape[0]
    tn = 1024
    return pl.pallas_call(
        _ada_kernel,
        out_shape=jax.ShapeDtypeStruct((depth, rows, width), F32),
        grid=(depth, width // tn),
        in_specs=[pl.BlockSpec((rows, d), lambda l, j: (0, 0)),
                  pl.BlockSpec((1, d, tn), lambda l, j: (l, 0, j)),
                  pl.BlockSpec((1, 1, tn), lambda l, j: (l, 0, j))],
        out_specs=pl.BlockSpec((1, rows, tn), lambda l, j: (l, 0, j)),
        compiler_params=_cparams(("arbitrary", "arbitrary"), 2 * d * tn * 4 + 8 * MIB),
    )(cc, ada_w, ada_b.reshape(depth, 1, width))


def _rope(a, cos, sin, half):
    lane = lax.broadcasted_iota(jnp.int32, a.shape, 1)
    first = (lane % (2 * half)) < half
    rot = jnp.where(first, -pltpu.roll(a, LANES - half, 1), pltpu.roll(a, half, 1))
    return a * cos + rot * sin


def _inproj_kernel(x_ref, sh_ref, sc_ref, g_ref, w_ref, kvg_ref, wkn_ref, e_ref, wv_ref,
                   c32_ref, s32_ref, c64_ref, s64_ref,
                   mq_ref, mk_ref, mv_ref, sq_ref, sk_ref, sv_ref, nq_ref, nk_ref, nv_ref,
                   rq_ref, rk_ref, rv_ref, gf_ref, gb_ref):
    h = _norm_mod(x_ref[0], g_ref[...], sh_ref[0], sc_ref[0]).astype(BF16)

    def seg(lo, width):
        return _dot(h, w_ref[:, lo:lo + width])

    c32, s32, c64, s64 = c32_ref[...], s32_ref[...], c64_ref[...], s64_ref[...]
    mla_scale = MLA_QK ** -0.5
    head_scale = HEAD_DIM ** -0.5
    for hd in range(4):
        a = seg(_MQ + hd * LANES, LANES)
        mq_ref[0, :, hd * LANES:(hd + 1) * LANES] = (_rope(a, c32, s32, MLA_ROPE // 4) * mla_scale).astype(BF16)
    kvn = _rmsnorm(seg(_CKV, MLA_RANK), kvg_ref[...]).astype(BF16)
    kr = _rope(seg(_KR, LANES), c32, s32, MLA_ROPE // 4).astype(BF16)
    mk_ref[0] = (_dot(kvn, wkn_ref[...]) + _dot(kr, e_ref[...])).astype(BF16)
    mv_ref[0] = _dot(kvn, wv_ref[...]).astype(BF16)
    for grp in range(2):
        a = seg(_SQ + grp * LANES, LANES)
        sq_ref[0, :, grp * LANES:(grp + 1) * LANES] = (_rope(a, c64, s64, HEAD_DIM // 4) * head_scale).astype(BF16)
    sk_ref[0] = _rope(seg(_SK, LANES), c64, s64, HEAD_DIM // 4).astype(BF16)
    sv_ref[0] = seg(_SV, LANES).astype(BF16)
    nq_ref[0] = (seg(_NQ, GROUP_WIDTH) * head_scale).astype(BF16)
    nk_ref[0] = seg(_NK, GROUP_WIDTH).astype(BF16)
    nv_ref[0] = seg(_NV, GROUP_WIDTH).astype(BF16)
    rq_ref[0] = seg(_RQ, GROUP_WIDTH).astype(BF16)
    rk_ref[0] = (seg(_RK, GROUP_WIDTH) * head_scale).astype(BF16)
    rv_ref[0] = seg(_RV, GROUP_WIDTH).astype(BF16)
    gf_ref[0] = seg(_GF, GROUP_WIDTH).astype(BF16)
    gb_ref[0] = seg(_GB, GROUP_WIDTH).astype(BF16)


_INPROJ_WIDTHS = (MLA_WIDTH, MLA_WIDTH, GROUP_WIDTH, GROUP_WIDTH, LANES, LANES) + (GROUP_WIDTH,) * 8


def _in_projection(x, shift, scale, g, w, kvg, wkn, emat, wv, tabs):
    b, t, d = x.shape
    tm = min(512, t)
    const = lambda shape: pl.BlockSpec(shape, lambda bi, i: (0,) * len(shape))
    tab = pl.BlockSpec((tm, LANES), lambda bi, i: (i, 0))
    vec = pl.BlockSpec((1, 1, d), lambda bi, i: (bi, 0, 0))
    est = 2 * (tm * d * 4 + d * _IN_COLS * 2 + tm * sum(_INPROJ_WIDTHS) * 2 + 4 * tm * LANES * 4) + 12 * MIB
    return pl.pallas_call(
        _inproj_kernel,
        out_shape=[jax.ShapeDtypeStruct((b, t, wd), BF16) for wd in _INPROJ_WIDTHS],
        grid=(b, t // tm),
        in_specs=[pl.BlockSpec((1, tm, d), lambda bi, i: (bi, i, 0)), vec, vec, const((1, d)),
                  const((d, _IN_COLS)), const((1, MLA_RANK)), const((MLA_RANK, MLA_WIDTH)), const((LANES, MLA_WIDTH)),
                  const((MLA_RANK, GROUP_WIDTH)), tab, tab, tab, tab],
        out_specs=[pl.BlockSpec((1, tm, wd), lambda bi, i: (bi, i, 0)) for wd in _INPROJ_WIDTHS],
        compiler_params=_cparams(("arbitrary", "arbitrary"), est),
    )(x, shift, scale, g, w, kvg, wkn, emat, wv, *tabs)


def _attend_pair(qs, pieces, sinks):
    lo = _low_half((1, LANES))
    m0, m1 = _half_masks()
    outs = []
    for s in (0, 1):
        own, other = (m0, m1) if s == 0 else (m1, m0)
        scores = []
        for ks, _, biases in pieces:
            sc = _dot_nt(qs[s], ks[s])
            if biases[s] is not None:
                sc = sc + biases[s]
            scores.append(sc)
        m = functools.reduce(jnp.maximum, [jnp.max(sc, axis=-1, keepdims=True) for sc in scores])
        if sinks is not None:
            m = jnp.maximum(m, sinks[s])
        o = 0.0
        for sc, (_, v, _) in zip(scores, pieces):
            o = o + _dot(jnp.exp(sc - m).astype(BF16), v * own + other)
        own_lanes = lo if s == 0 else jnp.logical_not(lo)
        den = jnp.where(own_lanes, 1.0, o)
        if sinks is not None:
            den = den + jnp.exp(sinks[s] - m)
        outs.append(o * pltpu.roll(1.0 / den, HEAD_DIM, 1))
    return jnp.where(lo, outs[0], outs[1])


def _mla_kernel(q_ref, kc_ref, vc_ref, kl_ref, vl_ref, o_ref, m_ref, l_ref, acc_ref):
    j = pl.program_id(2)

    @pl.when(j == 0)
    def _():
        m_ref[...] = jnp.full(m_ref.shape, NEG_INF, F32)
        l_ref[...] = jnp.zeros(l_ref.shape, F32)
        acc_ref[...] = jnp.zeros(acc_ref.shape, F32)

    def update(k, v):
        lo = _low_half((1, LANES))
        m0, m1 = _half_masks()
        reps = k.shape[0] // LANES
        for pair in range(2):
            vp = v[:, pair * LANES:(pair + 1) * LANES]
            vals = (vp * m0 + m1, vp * m1 + m0)
            alphas, pvs = [], []
            for s in (0, 1):
                hd = 2 * pair + s
                q = q_ref[0, :, hd * LANES:(hd + 1) * LANES]
                sc = _dot_nt(q, k[:, hd * LANES:(hd + 1) * LANES])
                m_prev = m_ref[hd]
                m_new = jnp.maximum(m_prev, jnp.max(sc, axis=-1, keepdims=True))
                m_ref[hd] = m_new
                alphas.append(jnp.exp(m_prev - m_new))
                p = jnp.exp(sc - jnp.tile(m_new, (1, reps)))
                pvs.append(_dot(p.astype(BF16), vals[s]))
            acc_ref[pair] = (acc_ref[pair] * jnp.where(lo, alphas[0], alphas[1])
                             + jnp.where(lo, pvs[0], pvs[1]))
            l_ref[pair] = (l_ref[pair] * jnp.where(lo, alphas[1], alphas[0])
                           + jnp.where(lo, pvs[1], pvs[0]))

    @pl.when(j == 0)
    def _():
        update(kc_ref[0], vc_ref[0])

    @pl.when(j > 0)
    def _():
        update(kl_ref[0], vl_ref[0])

    @pl.when(j == pl.num_programs(2) - 1)
    def _():
        for pair in range(2):
            inv = pltpu.roll(1.0 / l_ref[pair], HEAD_DIM, 1)
            o_ref[0, :, pair * LANES:(pair + 1) * LANES] = (acc_ref[pair] * inv).astype(BF16)


def _mla_attention(q, kc, vc, kl, vl):
    b, n, _ = q.shape
    c = kc.shape[1]
    tq = min(2048, n)
    tk = min(1024, n)
    kv_width = MLA_WIDTH + GROUP_WIDTH
    est = (2 * 2 * (tq * MLA_WIDTH + (c + tk) * kv_width + tq * GROUP_WIDTH) + 6 * tq * LANES * 4
           + 10 * tq * tk * 4 + 8 * MIB)
    return pl.pallas_call(
        _mla_kernel,
        out_shape=jax.ShapeDtypeStruct((b, n, GROUP_WIDTH), BF16),
        grid=(b, n // tq, 1 + n // tk),
        in_specs=[pl.BlockSpec((1, tq, MLA_WIDTH), lambda bi, i, j: (bi, i, 0)),
                  pl.BlockSpec((1, c, MLA_WIDTH), lambda bi, i, j: (bi, 0, 0)),
                  pl.BlockSpec((1, c, GROUP_WIDTH), lambda bi, i, j: (bi, 0, 0)),
                  pl.BlockSpec((1, tk, MLA_WIDTH), lambda bi, i, j: (bi, jnp.maximum(j - 1, 0), 0)),
                  pl.BlockSpec((1, tk, GROUP_WIDTH), lambda bi, i, j: (bi, jnp.maximum(j - 1, 0), 0))],
        out_specs=pl.BlockSpec((1, tq, GROUP_WIDTH), lambda bi, i, j: (bi, i, 0)),
        scratch_shapes=[pltpu.VMEM((4, tq, LANES), F32), pltpu.VMEM((2, tq, LANES), F32), pltpu.VMEM((2, tq, LANES), F32)],
        compiler_params=_cparams(("arbitrary", "arbitrary", "arbitrary"), est),
    )(q, kc, vc, kl, vl)


def _swa_kernel(q_ref, kp_ref, kcur_ref, kn_ref, vp_ref, vcur_ref, vn_ref, kctx_ref, vctx_ref, sink_ref, o_ref):
    i = pl.program_id(1)
    last = pl.num_programs(1) - 1
    tq = q_ref.shape[1]

    def band_bias(width, offset, edge_penalty):
        r = lax.broadcasted_iota(jnp.int32, (tq, width), 0)
        cidx = lax.broadcasted_iota(jnp.int32, (tq, width), 1)
        rel = cidx + offset - r
        inside = jnp.where(rel >= -SWA_WINDOW, jnp.where(rel <= SWA_WINDOW, 1, 0), 0)
        return jnp.where(inside == 1, edge_penalty, NEG_INF)

    halo = kp_ref.shape[1]
    b_prev = band_bias(halo, -halo, jnp.where(i > 0, 0.0, NEG_INF))
    b_cur = band_bias(tq, 0, 0.0)
    b_next = band_bias(halo, tq, jnp.where(i < last, 0.0, NEG_INF))
    m0, m1 = _half_masks()
    kctx, vctx = kctx_ref[0], vctx_ref[0]
    pieces = [((kp_ref[0],) * 2, vp_ref[0], (b_prev,) * 2),
              ((kcur_ref[0],) * 2, vcur_ref[0], (b_cur,) * 2),
              ((kn_ref[0],) * 2, vn_ref[0], (b_next,) * 2),
              ((kctx,) * 2, vctx, (None, None))]
    for grp in range(2):
        qg = q_ref[0, :, grp * LANES:(grp + 1) * LANES]
        sinks = [sink_ref[2 * grp + s:2 * grp + s + 1, 0:1] for s in (0, 1)]
        o_ref[0, :, grp * LANES:(grp + 1) * LANES] = _attend_pair((qg * m0, qg * m1), pieces, sinks).astype(BF16)


def _na_kernel(q_ref, kp_ref, kcur_ref, kn_ref, vp_ref, vcur_ref, vn_ref, kctx_ref, vctx_ref, bias_ref, o_ref):
    tq = q_ref.shape[1]
    m0, m1 = _half_masks()
    for pair in range(2):
        sl = slice(pair * LANES, (pair + 1) * LANES)
        qp = q_ref[0, :, sl]
        pieces = []
        for idx, (kr, vr) in enumerate(((kp_ref, vp_ref), (kcur_ref, vcur_ref), (kn_ref, vn_ref))):
            biases = tuple(bias_ref[0, 2 * pair + s, :, idx * tq:(idx + 1) * tq] for s in (0, 1))
            pieces.append(((kr[0, :, sl],) * 2, vr[0, :, sl], biases))
        pieces.append(((kctx_ref[0, :, sl],) * 2, vctx_ref[0, :, sl], (None, None)))
        o_ref[0, :, sl] = _attend_pair((qp * m0, qp * m1), pieces, None).astype(BF16)


def _na_bias_tables(rpb, rows):
    nb = rows // NA_ROWS
    heads = rpb.shape[0]
    width = 2 * NA_KW - 1
    span = 2 * GRID_W - 1
    lead = GRID_W - 2 - (NA_KW - 1)
    rpb = rpb.astype(F32)
    per_row = jnp.stack([rpb[:, NA_ROWS - 1 - a:NA_ROWS - 1 - a + 3 * NA_ROWS, :] for a in range(NA_ROWS)], axis=1)
    padded = jnp.pad(per_row, ((0, 0), (0, 0), (0, 0), (lead, span - width - lead)))
    tiled = jnp.broadcast_to(padded[:, :, :, None, :], (heads, NA_ROWS, 3 * NA_ROWS, GRID_W, span))
    skew = tiled.reshape(heads, NA_ROWS, 3 * NA_ROWS, GRID_W * span)[..., :GRID_W * (span - 1)]
    toep = skew.reshape(heads, NA_ROWS, 3 * NA_ROWS, GRID_W, span - 1)[..., GRID_W - 2:]
    table = jnp.transpose(toep, (0, 1, 3, 2, 4)).reshape(heads, NA_ROWS * GRID_W, 3 * NA_ROWS * GRID_W)
    a = np.arange(NA_ROWS)[:, None, None, None]
    cq = np.arange(GRID_W)[None, :, None, None]
    kr = np.arange(3 * NA_ROWS)[None, None, :, None]
    ck = np.arange(GRID_W)[None, None, None, :]
    valids = []
    for j in (0, 1, nb - 1):
        r = NA_ROWS * j + a
        rs = np.clip(r - NA_KH // 2, 0, rows - NA_KH)
        rk = NA_ROWS * (j - 1) + kr
        cs = np.clip(cq - NA_KW // 2, 0, GRID_W - NA_KW)
        valid = (rk >= rs) & (rk < rs + NA_KH) & (ck >= cs) & (ck < cs + NA_KW)
        valids.append(valid.reshape(NA_ROWS * GRID_W, 3 * NA_ROWS * GRID_W))
    valid = jnp.asarray(np.stack(valids))
    return jnp.where(valid[:, None], table[None], NEG_INF)


def _local_kernel(*refs):
    _swa_kernel(*refs[:10], refs[20])
    _na_kernel(*refs[10:20], refs[21])


def _local_attention(sq, sk, sv, skc, svc, sink_rows, nq, nk, nv, nkc, nvc, bias):
    b, n, _ = sq.shape
    c = skc.shape[1]
    tq = NA_ROWS * GRID_W
    nb = n // tq
    halo = SWA_WINDOW
    r = tq // halo
    nh = n // halo
    cur = lambda bi, i: (bi, i, 0)
    prev = lambda bi, i: (bi, jnp.maximum(i - 1, 0), 0)
    nxt = lambda bi, i: (bi, jnp.minimum(i + 1, nb - 1), 0)
    hprev = lambda bi, i: (bi, jnp.maximum(i * r - 1, 0), 0)
    hnxt = lambda bi, i: (bi, jnp.minimum(i * r + r, nh - 1), 0)
    ctx = lambda bi, i: (bi, 0, 0)
    variant = lambda bi, i: (jnp.where(i == 0, 0, jnp.where(i == nb - 1, 2, 1)), 0, 0, 0)
    blk = pl.BlockSpec((1, tq, GROUP_WIDTH), cur)
    halo_specs = [pl.BlockSpec((1, halo, LANES), hprev), pl.BlockSpec((1, tq, LANES), cur),
                  pl.BlockSpec((1, halo, LANES), hnxt)]
    win_specs = [pl.BlockSpec((1, tq, GROUP_WIDTH), prev), blk, pl.BlockSpec((1, tq, GROUP_WIDTH), nxt)]
    in_specs = ([blk] + halo_specs + halo_specs + [pl.BlockSpec((1, c, LANES), ctx)] * 2
                + [pl.BlockSpec((4, LANES), lambda bi, i: (0, 0))]
                + [blk] + win_specs + win_specs + [pl.BlockSpec((1, c, GROUP_WIDTH), ctx)] * 2
                + [pl.BlockSpec((1, 4, tq, 3 * tq), variant)])
    est = 2 * 4 * tq * 3 * tq * 4 + 32 * tq * (3 * tq + c) * 4 + 8 * MIB
    return pl.pallas_call(
        _local_kernel,
        out_shape=[jax.ShapeDtypeStruct((b, n, GROUP_WIDTH), BF16)] * 2,
        grid=(b, nb),
        in_specs=in_specs,
        out_specs=[blk, blk],
        compiler_params=_cparams(("arbitrary", "arbitrary"), est),
    )(sq, sk, sk, sk, sv, sv, sv, skc, svc, sink_rows, nq, nk, nk, nk, nv, nv, nv, nkc, nvc, bias)


def _ctx_attn_kernel(mq_ref, mk_ref, mv_ref, sq_ref, sk_ref, sv_ref, sink_ref, nq_ref, nk_ref, nv_ref,
                     ya_ref, yb_ref, yc_ref):
    m0, m1 = _half_masks()
    none2 = (None, None)
    for pair in range(2):
        sl = slice(pair * LANES, (pair + 1) * LANES)
        h0 = slice(2 * pair * LANES, (2 * pair + 1) * LANES)
        h1 = slice((2 * pair + 1) * LANES, (2 * pair + 2) * LANES)
        ya_ref[0, :, sl] = _attend_pair((mq_ref[0, :, h0], mq_ref[0, :, h1]),
                                        [((mk_ref[0, :, h0], mk_ref[0, :, h1]), mv_ref[0, :, sl], none2)],
                                        None).astype(BF16)
        qg = sq_ref[0, :, sl]
        sinks = [sink_ref[2 * pair + s:2 * pair + s + 1, 0:1] for s in (0, 1)]
        yb_ref[0, :, sl] = _attend_pair((qg * m0, qg * m1), [((sk_ref[0],) * 2, sv_ref[0], none2)],
                                        sinks).astype(BF16)
        qn = nq_ref[0, :, sl]
        yc_ref[0, :, sl] = _attend_pair((qn * m0, qn * m1), [((nk_ref[0, :, sl],) * 2, nv_ref[0, :, sl], none2)],
                                        None).astype(BF16)


def _ctx_attention(mq, mk, mv, sq, sk, sv, sink_rows, nq, nk, nv):
    b, c, _ = mq.shape
    spec = lambda wd: pl.BlockSpec((1, c, wd), lambda bi: (bi, 0, 0))
    return pl.pallas_call(
        _ctx_attn_kernel,
        out_shape=[jax.ShapeDtypeStruct((b, c, GROUP_WIDTH), BF16)] * 3,
        grid=(b,),
        in_specs=[spec(MLA_WIDTH), spec(MLA_WIDTH), spec(GROUP_WIDTH), spec(GROUP_WIDTH), spec(LANES), spec(LANES),
                  pl.BlockSpec((4, LANES), lambda bi: (0, 0)), spec(GROUP_WIDTH), spec(GROUP_WIDTH), spec(GROUP_WIDTH)],
        out_specs=[spec(GROUP_WIDTH)] * 3,
        compiler_params=_cparams(("arbitrary",), 24 * MIB),
    )(mq, mk, mv, sq, sk, sv, sink_rows, nq, nk, nv)


def _ret_tables(dec_lane, reverse, c):
    lg = -(jnp.maximum(-dec_lane, 0.0) + jnp.log1p(jnp.exp(-jnp.abs(dec_lane))))
    ti = lax.broadcasted_iota(jnp.int32, (c, LANES), 0).astype(F32)
    if reverse:
        qpow, kpow = c - ti, ti
    else:
        qpow, kpow = ti + 1.0, c - 1.0 - ti
    ii = lax.broadcasted_iota(jnp.int32, (c, c), 0)
    jj = lax.broadcasted_iota(jnp.int32, (c, c), 1)
    dist = (jj - ii) if reverse else (ii - jj)
    distf = jnp.maximum(dist, 0).astype(F32)
    masks = [jnp.where(dist >= 0, jnp.exp(lg[:, s * HEAD_DIM:s * HEAD_DIM + 1] * distf), 0.0) for s in (0, 1)]
    return jnp.exp(lg * qpow), jnp.exp(lg * kpow), masks[0], masks[1], jnp.exp(lg * float(c))


def _ret_chunk(q, k, v, g, qdec, kdec, decays, cdec, gn_lane, s_ref):
    lo = _low_half((1, LANES))
    ii = lax.broadcasted_iota(jnp.int32, (LANES, LANES), 0)
    jj = lax.broadcasted_iota(jnp.int32, (LANES, LANES), 1)
    outs = []
    for msk, decay in zip(_half_masks(), decays):
        inner = _dot_nt(q * msk, k) * decay
        outs.append(_dot(inner.astype(BF16), v))
    state = s_ref[...]
    o = jnp.where(lo, outs[0], outs[1]) + _dot((q.astype(F32) * qdec).astype(BF16), state.astype(BF16))
    kd = (k.astype(F32) * kdec).astype(BF16)
    same_head = (ii < HEAD_DIM) == (jj < HEAD_DIM)
    s_ref[...] = state * cdec + jnp.where(same_head, _dot_tn(kd, v), 0.0)

    def head_mean(x):
        s_lo = jnp.sum(jnp.where(lo, x, 0.0), axis=-1, keepdims=True)
        s_hi = jnp.sum(jnp.where(lo, 0.0, x), axis=-1, keepdims=True)
        return jnp.where(lo, s_lo, s_hi) * (1.0 / HEAD_DIM)

    dev = o - head_mean(o)
    normed = dev * lax.rsqrt(head_mean(dev * dev) + EPS)
    gf = g.astype(F32)
    return (gf * jax.nn.sigmoid(gf)) * (normed * gn_lane)


def _ret_kernel(decf_ref, decb_ref, gnf_ref, gnb_ref, s0f_ref, s0b_ref,
                qf_ref, kf_ref, vf_ref, gf_ref, qb_ref, kb_ref, vb_ref, gb_ref,
                yf_ref, yb_ref, sf_ref, sb_ref, st_scr, tab_scr, mask_scr, cdec_scr):
    t = pl.program_id(1)
    directions = ((decf_ref, gnf_ref, qf_ref, kf_ref, vf_ref, gf_ref, yf_ref),
                  (decb_ref, gnb_ref, qb_ref, kb_ref, vb_ref, gb_ref, yb_ref))

    batch = qf_ref.shape[0]

    @pl.when(t == 0)
    def _():
        for e in range(batch):
            st_scr[e, 0] = s0f_ref[e]
            st_scr[e, 1] = s0b_ref[e]
        for pair in range(2):
            for d, refs in enumerate(directions):
                qdec, kdec, mask0, mask1, cdec = _ret_tables(refs[0][pair], d == 1, qf_ref.shape[1])
                tab_scr[d, pair, 0] = qdec
                tab_scr[d, pair, 1] = kdec
                mask_scr[d, pair, 0] = mask0
                mask_scr[d, pair, 1] = mask1
                cdec_scr[d, pair] = jnp.broadcast_to(cdec, (8, LANES))

    for e in range(batch):
        for d, (_, gn_ref, q_ref, k_ref, v_ref, g_ref, y_ref) in enumerate(directions):
            for pair in range(2):
                sl = slice(pair * LANES, (pair + 1) * LANES)
                y = _ret_chunk(q_ref[e, :, sl], k_ref[e, :, sl], v_ref[e, :, sl], g_ref[e, :, sl],
                               tab_scr[d, pair, 0], tab_scr[d, pair, 1], (mask_scr[d, pair, 0], mask_scr[d, pair, 1]),
                               cdec_scr[d, pair, 0:1, :], gn_ref[:, sl], st_scr.at[e, d, pair])
                y_ref[e, :, sl] = y.astype(BF16)

    @pl.when(t == pl.num_programs(1) - 1)
    def _():
        for e in range(batch):
            sf_ref[e] = st_scr[e, 0]
            sb_ref[e] = st_scr[e, 1]


def _retention(q, k, v, gf, gb, dec_f, dec_b, gn_f, gn_b, s0f, s0b):
    b, t, width = q.shape
    c = min(RET_CHUNK, t)
    nt = t // c
    bb = RET_BATCH if b % RET_BATCH == 0 else 1
    fwd = pl.BlockSpec((bb, c, width), lambda bi, i: (bi, i, 0))
    bwd = pl.BlockSpec((bb, c, width), lambda bi, i: (bi, nt - 1 - i, 0))
    dec = pl.BlockSpec((2, 1, LANES), lambda bi, i: (0, 0, 0))
    gn = pl.BlockSpec((1, width), lambda bi, i: (0, 0))
    st = pl.BlockSpec((bb, 2, LANES, LANES), lambda bi, i: (bi, 0, 0, 0))
    return pl.pallas_call(
        _ret_kernel,
        out_shape=[jax.ShapeDtypeStruct((b, t, width), BF16)] * 2 + [jax.ShapeDtypeStruct((b, 2, LANES, LANES), F32)] * 2,
        grid=(b // bb, nt),
        in_specs=[dec, dec, gn, gn, st, st, fwd, fwd, fwd, fwd, bwd, bwd, bwd, bwd],
        out_specs=[fwd, bwd, st, st],
        scratch_shapes=[pltpu.VMEM((bb, 2, 2, LANES, LANES), F32), pltpu.VMEM((2, 2, 2, c, LANES), F32),
                        pltpu.VMEM((2, 2, 2, c, c), F32),
                        pltpu.VMEM((2, 2, 8, LANES), F32)],
        compiler_params=_cparams(("arbitrary", "arbitrary"), 24 * MIB),
    )(dec_f, dec_b, gn_f, gn_b, s0f, s0b, q, k, v, gf, q, k, v, gb)


def _outproj_kernel(x_ref, ya_ref, yb_ref, yc_ref, ydf_ref, ydb_ref, beta_ref, w_ref, gate_ref, o_ref):
    ys = (ya_ref[0].astype(F32), yb_ref[0].astype(F32), yc_ref[0].astype(F32),
          ydf_ref[0].astype(F32) + ydb_ref[0].astype(F32))
    acc = 0.0
    for s, y in enumerate(ys):
        sl = slice(s * GROUP_WIDTH, (s + 1) * GROUP_WIDTH)
        acc = acc + _dot((y * beta_ref[:, sl]).astype(BF16), w_ref[sl, :])
    o_ref[0] = x_ref[0] + gate_ref[0] * acc


def _out_projection(x, ya, yb, yc, ydf, ydb, beta, w, gate):
    b, t, d = x.shape
    tm = min(512, t)
    row = lambda wd: pl.BlockSpec((1, tm, wd), lambda bi, i: (bi, i, 0))
    est = 2 * (2 * tm * d * 4 + 5 * tm * GROUP_WIDTH * 2 + d * d * 2) + 8 * MIB
    return pl.pallas_call(
        _outproj_kernel,
        out_shape=jax.ShapeDtypeStruct((b, t, d), F32),
        grid=(b, t // tm),
        in_specs=[row(d), row(GROUP_WIDTH), row(GROUP_WIDTH), row(GROUP_WIDTH), row(GROUP_WIDTH), row(GROUP_WIDTH),
                  pl.BlockSpec((1, d), lambda bi, i: (0, 0)), pl.BlockSpec((d, d), lambda bi, i: (0, 0)),
                  pl.BlockSpec((1, 1, d), lambda bi, i: (bi, 0, 0))],
        out_specs=row(d),
        compiler_params=_cparams(("arbitrary", "arbitrary"), est),
    )(x, ya, yb, yc, ydf, ydb, beta, w, gate)


def _sorting_network(n):
    size = 1
    while size < n:
        size *= 2
    pairs = []
    p = 1
    while p < size:
        k = p
        while k >= 1:
            for j in range(k % p, size - k, 2 * k):
                for i in range(min(k, size - j - k)):
                    if (i + j) // (2 * p) == (i + j + k) // (2 * p):
                        pairs.append((i + j, i + j + k))
            k //= 2
        p *= 2
    return [(i, j) for i, j in pairs if j < n]


def _top_values(scores, out_ref, count):
    groups = scores.shape[0] // 8
    lists = [scores[8 * g:8 * g + 8, :] for g in range(groups)]
    for i, j in _sorting_network(groups):
        lists[i], lists[j] = jnp.maximum(lists[i], lists[j]), jnp.minimum(lists[i], lists[j])
    for r in range(count):
        m = jnp.max(lists[0], axis=0, keepdims=True)
        out_ref[r:r + 1, :] = m
        hit = lists[0] == m
        need = count - r - 1
        for p in range(min(groups - 1, need)):
            lists[p] = jnp.where(hit, lists[p + 1], lists[p])
        if need >= groups:
            lists[groups - 1] = jnp.where(hit, -jnp.inf, lists[groups - 1])


def _staircase(w1, w2):
    row8 = lax.broadcasted_iota(jnp.int32, (8, w1.shape[1]), 0)
    cands = [w1[0:1] + w2]
    for a in range(1, 8):
        cands.append(jnp.where(row8 < PEER_TOPK // (a + 1), w1[a:a + 1] + w2[0:8], -jnp.inf))
    cands.append(w1[8:16] + w2[0:1])
    return jnp.concatenate(cands, axis=0)


def _bf16_twice(x):
    hi = pltpu.bitcast(x.astype(BF16).astype(F32), jnp.uint32)
    return hi | (hi >> 16)


def _bf16_rows(words, rows):
    return pltpu.bitcast(jnp.broadcast_to(words, (rows // 2, words.shape[1])), BF16)


def _peer_route(hd, q_scr, k1_ref, k2_ref, rk_scr, e2_scr, n_scr, e1_scr, v1_scr, v2_scr, top_scr):
    half = PEER_NKEYS
    base = pl.multiple_of(hd * 2 * half, 2 * half)
    s1 = _dot(k1_ref[...], q_scr[pl.ds(base, half), :])
    s2 = _dot(k2_ref[...], q_scr[pl.ds(base + half, half), :])
    _top_values(s1, v1_scr, PEER_TOPK)
    _top_values(s2, v2_scr, PEER_TOPK)
    v1, v2 = v1_scr[...], v2_scr[...]
    cand = _staircase(v1, v2)
    _top_values(cand, top_scr, PEER_TOPK)
    top = top_scr[...]
    z = jnp.sum(jnp.exp(top - top[0:1]), axis=0, keepdims=True)
    sel = jnp.where(cand >= top[PEER_TOPK - 1:PEER_TOPK], 1.0, 0.0)
    count = lambda picked: jnp.sum(picked, axis=0, keepdims=True)
    lens = [count(sel[0:16])] + [count(sel[8 + 8 * a:16 + 8 * a]) for a in range(1, 8)]
    lens += [sel[72 + a:73 + a] for a in range(8)]
    n_sel = jnp.zeros(s1.shape, F32)
    for a in reversed(range(PEER_TOPK)):
        n_sel = jnp.where(s1 >= v1[a:a + 1], lens[a], n_sel)
    rank = jnp.zeros(s2.shape, F32)
    for b in range(PEER_TOPK):
        rank = rank + jnp.where(s2 < v2[b:b + 1], 1.0, 0.0)
    rk_scr[hd] = rank.astype(BF16)
    n_scr[hd] = _bf16_twice(n_sel)
    e1_scr[hd] = _bf16_twice(jnp.exp(s1 - v1[0:1]) / z)
    e2_scr[hd] = (jnp.exp(s2 - v2[0:1]) * 0.5).astype(BF16)


def _peer_kernel(*refs, final_norm):
    if final_norm:
        (x_ref, sh_ref, sc_ref, gt_ref, g_ref, wqt_ref, k1_ref, k2_ref, u_ref, vt_ref, fg_ref, o_ref,
         ht_all, q_scr, rk_all, e2_all, n_all, e1_all, v1_scr, v2_scr, top_scr, a0_all, a1_all, acc_all) = refs
    else:
        (x_ref, sh_ref, sc_ref, gt_ref, g_ref, wqt_ref, k1_ref, k2_ref, u_ref, vt_ref, o_ref,
         ht_all, q_scr, rk_all, e2_all, n_all, e1_all, v1_scr, v2_scr, top_scr, a0_all, a1_all, acc_all) = refs
        fg_ref = None
    step = pl.program_id(2)
    sub = pl.program_id(3)
    ht_scr, rk_scr, e2_scr, n_scr, e1_scr, acc_scr = (r.at[sub] for r in (ht_all, rk_all, e2_all, n_all, e1_all, acc_all))
    tokens = ht_all.shape[2]
    tok_rows = pl.ds(pl.multiple_of(sub * tokens, tokens), tokens)

    @pl.when(step == 0)
    def _():
        h = _norm_mod(x_ref[0, tok_rows, :], g_ref[...], sh_ref[0], sc_ref[0])
        ht = jnp.transpose(h).astype(BF16)
        ht_scr[...] = ht
        q_scr[...] = _dot(wqt_ref[...], ht).astype(BF16)
        acc_scr[...] = jnp.zeros(acc_scr.shape, F32)

        def route(pair, carry):
            for k in range(2):
                _peer_route(2 * pair + k, q_scr, k1_ref, k2_ref, rk_scr, e2_scr, n_scr, e1_scr,
                            v1_scr.at[k], v2_scr.at[k], top_scr.at[k])
            return carry

        lax.fori_loop(0, PEER_HEADS // 2, route, 0)

    def project(a_write):
        a_write[...] = _dot(u_ref[...], ht_scr[...])

    def consume(a_read):
        y = None
        piece = PEER_PIECE_ROWS * PEER_NKEYS
        for kt in range(PEER_ROWS // PEER_PIECE_ROWS):
            ws = []
            for r in range(PEER_PIECE_ROWS * kt, PEER_PIECE_ROWS * (kt + 1)):
                rows = slice(r * PEER_NKEYS, (r + 1) * PEER_NKEYS)
                i = (step - 1) * PEER_ROWS + r
                gate = jnp.zeros((PEER_NKEYS, tokens), BF16)
                for hd in range(PEER_HEADS):
                    n_row = _bf16_rows(n_scr[hd, pl.ds(i, 1), :], PEER_NKEYS)
                    e1_row = _bf16_rows(e1_scr[hd, pl.ds(i, 1), :], PEER_NKEYS)
                    gate = gate + jnp.where(rk_scr[hd] < n_row, e2_scr[hd] * e1_row, jnp.zeros_like(gate))
                a = a_read[rows, :].astype(BF16)
                gelu2 = a * (1.0 + lax.erf(a * np.sqrt(0.5).astype(BF16)))
                ws.append(gelu2 * gate)
            part = _dot(vt_ref[:, kt * piece:(kt + 1) * piece], jnp.concatenate(ws, axis=0))
            y = part if y is None else y + part
        acc_scr[...] += y

    last = pl.num_programs(2) - 1
    bufs = (a0_all.at[sub], a1_all.at[sub])

    @pl.when(step == 0)
    def _():
        project(bufs[0])

    for parity in (0, 1):
        @pl.when(jnp.logical_and(jnp.logical_and(step > 0, step < last), step % 2 == parity))
        def _():
            project(bufs[parity])
            consume(bufs[1 - parity])

        @pl.when(jnp.logical_and(step == last, step % 2 == parity))
        def _():
            consume(bufs[1 - parity])

    @pl.when(step == last)
    def _():
        out = x_ref[0, tok_rows, :] + gt_ref[0] * jnp.transpose(acc_scr[...])
        if final_norm:
            out = _rmsnorm(out, fg_ref[...])
        o_ref[0, tok_rows, :] = out


def _peer_ffn(x, shift, scale, gate, g, wqt, k1, k2, u, vt, final_g=None):
    b, t, d = x.shape
    tt = PEER_TOKENS
    share = PEER_SHARE if t % (PEER_SHARE * tt) == 0 else 1
    ec = PEER_ROWS * PEER_NKEYS
    n_blocks = u.shape[0] // ec
    nq = wqt.shape[0]
    final_norm = final_g is not None
    tok = pl.BlockSpec((1, share * tt, d), lambda bi, i, s, k: (bi, i, 0))
    vec = pl.BlockSpec((1, 1, d), lambda bi, i, s, k: (bi, 0, 0))
    const = lambda shape: pl.BlockSpec(shape, lambda bi, i, s, k: (0,) * len(shape))
    in_specs = [tok, vec, vec, vec, const((1, d)), const((nq, d)), const(k1.shape), const(k2.shape),
                pl.BlockSpec((ec, d), lambda bi, i, s, k: (jnp.minimum(s, n_blocks - 1), 0)),
                pl.BlockSpec((d, ec), lambda bi, i, s, k: (0, jnp.clip(s - 1, 0, n_blocks - 1)))]
    args = [x, shift, scale, gate, g, wqt, k1, k2, u, vt]
    if final_norm:
        in_specs.append(const((1, d)))
        args.append(final_g)
    table = pltpu.VMEM((share, PEER_HEADS, PEER_NKEYS, tt), jnp.uint32)
    table16 = pltpu.VMEM((share, PEER_HEADS, PEER_NKEYS, tt), BF16)
    top = pltpu.VMEM((2, PEER_TOPK, tt), F32)
    a_buf = pltpu.VMEM((share, ec, tt), F32)
    scratch = [pltpu.VMEM((share, d, tt), BF16), pltpu.VMEM((nq, tt), BF16), table16, table16, table, table,
               top, top, top, a_buf, a_buf, pltpu.VMEM((share, d, tt), F32)]
    est = (4 * share * tt * d * 4 + 2 * nq * d * 2 + 4 * ec * d * 2 + nq * tt * 2
           + share * (tt * d * 2 + 12 * PEER_HEADS * PEER_NKEYS * tt + d * tt * 4 + 2 * ec * tt * 4) + 8 * MIB)
    return pl.pallas_call(
        functools.partial(_peer_kernel, final_norm=final_norm),
        out_shape=jax.ShapeDtypeStruct((b, t, d), F32),
        grid=(b, t // (share * tt), n_blocks + 1, share),
        in_specs=in_specs,
        out_specs=tok,
        scratch_shapes=scratch,
        compiler_params=_cparams(("arbitrary",) * 4, est),
    )(*args)


def _prep_in_weights(w_in):
    d = w_in.shape[0]
    zeros = lambda n: jnp.zeros((d, n), w_in.dtype)
    cols = []
    for hd in range(4):
        cols += [w_in[:, hd * MLA_QK:(hd + 1) * MLA_QK], zeros(LANES - MLA_QK)]
    cols.append(w_in[:, 384:512])
    cols += [zeros(HEAD_DIM), w_in[:, 512:544], zeros(LANES - HEAD_DIM - MLA_ROPE)]
    for hd in (0, 2, 1, 3):
        cols.append(w_in[:, 544 + hd * HEAD_DIM:544 + (hd + 1) * HEAD_DIM])
    cols.append(w_in[:, 800:])
    return jnp.concatenate(cols, axis=1).astype(BF16)


def _prep_kv_weights(w_ukv):
    zeros = jnp.zeros((MLA_RANK, HEAD_DIM), w_ukv.dtype)
    wkn = jnp.concatenate([blk for hd in range(4) for blk in (w_ukv[:, hd * LANES:hd * LANES + HEAD_DIM], zeros)], axis=1)
    wv = jnp.concatenate([w_ukv[:, hd * LANES + HEAD_DIM:(hd + 1) * LANES] for hd in range(4)], axis=1)
    sel = np.zeros((LANES, 4 * LANES), np.float32)
    for hd in range(4):
        for r in range(MLA_ROPE):
            sel[HEAD_DIM + r, hd * LANES + HEAD_DIM + r] = 1.0
    return wkn.astype(BF16), jnp.asarray(sel, BF16), wv.astype(BF16)


def _rope_lane_tables(n):
    def tables(rot_dim):
        quarter = rot_dim // 4
        inv = ROPE_BASE ** (-jnp.arange(quarter, dtype=F32) / quarter)
        t = jnp.arange(n, dtype=jnp.int32)
        pos = jnp.stack([t // GRID_W, t % GRID_W], axis=-1).astype(F32)
        ang = pos[:, :, None] * inv
        lanes = lambda tb: jnp.concatenate([tb[:, 0], tb[:, 0], tb[:, 1], tb[:, 1]], axis=-1)
        return lanes(jnp.cos(ang)), lanes(jnp.sin(ang))

    c32, s32 = tables(MLA_ROPE)
    c64, s64 = tables(HEAD_DIM)
    ones = lambda w: jnp.ones((n, w), F32)
    zeros = lambda w: jnp.zeros((n, w), F32)
    return (jnp.concatenate([ones(HEAD_DIM), c32, ones(LANES - HEAD_DIM - MLA_ROPE)], axis=-1),
            jnp.concatenate([zeros(HEAD_DIM), s32, zeros(LANES - HEAD_DIM - MLA_ROPE)], axis=-1),
            jnp.concatenate([c64, c64], axis=-1), jnp.concatenate([s64, s64], axis=-1))


_SWA_SLOT_HEADS = (0, 2, 1, 3)


def kernel(x, c, ctx, c_ctx, ada_w, ada_b, norm_mix_g, w_in, mla_kv_norm_g, mla_w_ukv, swa_sink, na_rpb, ret_decay_f, ret_decay_b, ret_gn_f, ret_gn_b, mix_beta, w_out, norm_ffn_g, peer_wq, peer_k1, peer_k2, peer_u, peer_v, final_norm_g):
    b, n, d = x.shape
    n_ctx = ctx.shape[1]
    depth = ada_w.shape[0]
    rows = n // GRID_W

    pad_rows = -(b + 1) % 8
    cc = jnp.concatenate([c, c_ctx[None, :], jnp.zeros((pad_rows, d), F32)], axis=0)
    mod = _ada_modulation(cc, ada_w, ada_b)

    lat_tabs = _rope_lane_tables(n)
    ctx_tabs = (jnp.ones((n_ctx, LANES), F32), jnp.zeros((n_ctx, LANES), F32)) * 2
    slot_heads = np.asarray(_SWA_SLOT_HEADS)
    out_perm = np.arange(d)
    out_perm[GROUP_WIDTH:2 * GROUP_WIDTH] = GROUP_WIDTH + (slot_heads[:, None] * HEAD_DIM + np.arange(HEAD_DIM)[None, :]).reshape(-1)
    zero_state = jnp.zeros((b, 2, LANES, LANES), F32)
    pair_lanes = lambda p: jnp.repeat(p.astype(F32), HEAD_DIM).reshape(2, 1, LANES)

    xc = ctx
    for layer in range(depth):
        with_ctx = layer < depth - 1
        last = layer == depth - 1
        chunk = lambda k, lo, hi: mod[layer, lo:hi, k * d:(k + 1) * d][:, None, :]
        mod_l = [chunk(k, 0, b) for k in range(6)]
        mod_c = [jnp.broadcast_to(chunk(k, b, b + 1), (b, 1, d)) for k in range(6)]

        w_cols = _prep_in_weights(w_in[layer])
        wkn, emat, wv = _prep_kv_weights(mla_w_ukv[layer])
        g_mix = norm_mix_g[layer][None, :]
        kvg = mla_kv_norm_g[layer][None, :]
        p_l = _in_projection(x, mod_l[0], mod_l[1], g_mix, w_cols, kvg, wkn, emat, wv, lat_tabs)
        p_c = _in_projection(xc, mod_c[0], mod_c[1], g_mix, w_cols, kvg, wkn, emat, wv, ctx_tabs)
        mq, mk, mv, sq, sk, sv, nq, nk, nv, rq, rk, rv, gf, gb = p_l
        cmq, cmk, cmv, csq, csk, csv, cnq, cnk, cnv, crq, crk, crv, cgf, cgb = p_c

        sink_rows = jnp.broadcast_to(swa_sink[layer].astype(F32)[slot_heads][:, None], (4, LANES))
        bias = _na_bias_tables(na_rpb[layer], rows)
        dec_f, dec_b = pair_lanes(ret_decay_f[layer]), pair_lanes(ret_decay_b[layer])
        gn_f, gn_b = ret_gn_f[layer][None, :], ret_gn_b[layer][None, :]

        ya = _mla_attention(mq, cmk, cmv, mk, mv)
        yb, yc = _local_attention(sq, sk, sv, csk, csv, sink_rows, nq, nk, nv, cnk, cnv, bias)
        cyf, cyb, s_f, s_b = _retention(crq, crk, crv, cgf, cgb, dec_f, dec_b, gn_f, gn_b, zero_state, zero_state)
        ydf, ydb, _, _ = _retention(rq, rk, rv, gf, gb, dec_f, dec_b, gn_f, gn_b, s_f, s_b)

        beta = mix_beta[layer][out_perm][None, :]
        w_o = w_out[layer][out_perm, :].astype(BF16)
        x = _out_projection(x, ya, yb, yc, ydf, ydb, beta, w_o, mod_l[2])

        g_ffn = norm_ffn_g[layer][None, :]
        wqt = peer_wq[layer].T.astype(BF16)
        k1, k2 = peer_k1[layer].astype(BF16), peer_k2[layer].astype(BF16)
        u, vt = peer_u[layer].astype(BF16), peer_v[layer].T.astype(BF16)
        x = _peer_ffn(x, mod_l[3], mod_l[4], mod_l[5], g_ffn, wqt, k1, k2, u, vt,
                      final_norm_g[None, :] if last else None)
        if with_ctx:
            cya, cyb_, cyc = _ctx_attention(cmq, cmk, cmv, csq, csk, csv, sink_rows, cnq, cnk, cnv)
            xc = _out_projection(xc, cya, cyb_, cyc, cyf, cyb, beta, w_o, mod_c[2])
            xc = _peer_ffn(xc, mod_c[3], mod_c[4], mod_c[5], g_ffn, wqt, k1, k2, u, vt)
    return x
```

```python
import functools

import numpy as np
import jax
import jax.numpy as jnp
from jax import lax
from jax.experimental import pallas as pl
from jax.experimental.pallas import tpu as pltpu

F32 = jnp.float32
BF16 = jnp.bfloat16

EPS = 1e-6
NEG_INF = -1e30
ROPE_BASE = 10000.0
GRID_W = 64
HEAD_DIM = 64
GROUP_WIDTH = 256
MLA_WIDTH = 512
LANES = 128
V7X_VMEM_BYTES = 64 * 1024 * 1024
MIB = 1024 * 1024

MLA_ROPE = 32
MLA_QK = 96
MLA_RANK = 128
SWA_WINDOW = 128
NA_KH, NA_KW = 8, 16
NA_ROWS = 4
RET_CHUNK = 256
RET_BATCH = 2
PEER_HEADS = 8
PEER_NKEYS = 128
PEER_TOPK = 16
PEER_TOKENS = 256
PEER_ROWS = 16
PEER_PIECE_ROWS = 2
PEER_SHARE = 2

_MQ, _CKV, _KR, _SQ, _SK, _SV, _NQ, _NK, _NV, _RQ, _RK, _RV, _GF, _GB, _IN_COLS = (
    0, 512, 640, 768, 1024, 1152, 1280, 1536, 1792, 2048, 2304, 2560, 2816, 3072, 3328)


def _cparams(semantics, vmem_bytes):
    limit = int(min(vmem_bytes, V7X_VMEM_BYTES * 7 // 8))
    return pltpu.CompilerParams(dimension_semantics=semantics, vmem_limit_bytes=limit)


def _dot(a, b):
    return jnp.dot(a, b, preferred_element_type=F32)


def _dot_nt(a, b):
    return lax.dot_general(a, b, (((1,), (1,)), ((), ())), preferred_element_type=F32)


def _dot_tn(a, b):
    return lax.dot_general(a, b, (((0,), (0,)), ((), ())), preferred_element_type=F32)


def _low_half(shape):
    return lax.broadcasted_iota(jnp.int32, shape, len(shape) - 1) < HEAD_DIM


def _half_masks():
    lo = _low_half((1, LANES))
    return (jnp.where(lo, 1.0, 0.0).astype(BF16), jnp.where(lo, 0.0, 1.0).astype(BF16))


def _norm_mod(x, g, shift, scale):
    ms = jnp.mean(x * x, axis=-1, keepdims=True)
    return (x * lax.rsqrt(ms + EPS) * g) * (1.0 + scale) + shift


def _rmsnorm(x, g):
    ms = jnp.mean(x * x, axis=-1, keepdims=True)
    return x * lax.rsqrt(ms + EPS) * g


def _ada_kernel(c_ref, w_ref, b_ref, o_ref):
    c = c_ref[...]
    o_ref[0] = _dot(c * jax.nn.sigmoid(c), w_ref[0]) + b_ref[0]


def _ada_modulation(cc, ada_w, ada_b):
    depth, d, width = ada_w.shape
    rows = cc.shape[0]
    tn = 1024
    return pl.pallas_call(
        _ada_kernel,
        out_shape=jax.ShapeDtypeStruct((depth, rows, width), F32),
        grid=(depth, width // tn),
        in_specs=[pl.BlockSpec((rows, d), lambda l, j: (0, 0)),
                  pl.BlockSpec((1, d, tn), lambda l, j: (l, 0, j)),
                  pl.BlockSpec((1, 1, tn), lambda l, j: (l, 0, j))],
        out_specs=pl.BlockSpec((1, rows, tn), lambda l, j: (l, 0, j)),
        compiler_params=_cparams(("arbitrary", "arbitrary"), 2 * d * tn * 4 + 8 * MIB),
    )(cc, ada_w, ada_b.reshape(depth, 1, width))


def _rope(a, cos, sin, half):
    lane = lax.broadcasted_iota(jnp.int32, a.shape, 1)
    first = (lane % (2 * half)) < half
    rot = jnp.where(first, -pltpu.roll(a, LANES - half, 1), pltpu.roll(a, half, 1))
    return a * cos + rot * sin


def _inproj_kernel(x_ref, sh_ref, sc_ref, g_ref, w_ref, kvg_ref, wk_ref, wv_ref,
                   c32_ref, s32_ref, c64_ref, s64_ref,
                   mq_ref, mk_ref, mv_ref, sq_ref, sk_ref, sv_ref, nq_ref, nk_ref, nv_ref,
                   rq_ref, rk_ref, rv_ref, gf_ref, gb_ref):
    h = _norm_mod(x_ref[0], g_ref[...], sh_ref[0], sc_ref[0]).astype(BF16)

    full = _dot(h, w_ref[...])

    def seg(lo, width):
        return full[:, lo:lo + width]

    c32, s32, c64, s64 = c32_ref[...], s32_ref[...], c64_ref[...], s64_ref[...]
    mla_scale = MLA_QK ** -0.5
    head_scale = HEAD_DIM ** -0.5
    for hd in range(4):
        a = seg(_MQ + hd * LANES, LANES)
        mq_ref[0, :, hd * LANES:(hd + 1) * LANES] = (_rope(a, c32, s32, MLA_ROPE // 4) * mla_scale).astype(BF16)
    kvn = _rmsnorm(seg(_CKV, MLA_RANK), kvg_ref[...]).astype(BF16)
    kr = _rope(seg(_KR, LANES), c32, s32, MLA_ROPE // 4).astype(BF16)
    mk_ref[0] = _dot(jnp.concatenate([kvn, kr], axis=1), wk_ref[...]).astype(BF16)
    mv_ref[0] = _dot(kvn, wv_ref[...]).astype(BF16)
    for grp in range(2):
        a = seg(_SQ + grp * LANES, LANES)
        sq_ref[0, :, grp * LANES:(grp + 1) * LANES] = (_rope(a, c64, s64, HEAD_DIM // 4) * head_scale).astype(BF16)
    sk_ref[0] = _rope(seg(_SK, LANES), c64, s64, HEAD_DIM // 4).astype(BF16)
    sv_ref[0] = seg(_SV, LANES).astype(BF16)
    nq_ref[0] = (seg(_NQ, GROUP_WIDTH) * head_scale).astype(BF16)
    nk_ref[0] = seg(_NK, GROUP_WIDTH).astype(BF16)
    nv_ref[0] = seg(_NV, GROUP_WIDTH).astype(BF16)
    rq_ref[0] = seg(_RQ, GROUP_WIDTH).astype(BF16)
    rk_ref[0] = (seg(_RK, GROUP_WIDTH) * head_scale).astype(BF16)
    rv_ref[0] = seg(_RV, GROUP_WIDTH).astype(BF16)
    gf_ref[0] = seg(_GF, GROUP_WIDTH).astype(BF16)
    gb_ref[0] = seg(_GB, GROUP_WIDTH).astype(BF16)


_INPROJ_WIDTHS = (MLA_WIDTH, MLA_WIDTH, GROUP_WIDTH, GROUP_WIDTH, LANES, LANES) + (GROUP_WIDTH,) * 8


def _in_projection(x, shift, scale, g, w, kvg, wk, wv, tabs):
    b, t, d = x.shape
    tm = min(512, t)
    const = lambda shape: pl.BlockSpec(shape, lambda bi, i: (0,) * len(shape))
    tab = pl.BlockSpec((tm, LANES), lambda bi, i: (i, 0))
    vec = pl.BlockSpec((1, 1, d), lambda bi, i: (bi, 0, 0))
    est = 2 * (tm * d * 4 + d * _IN_COLS * 2 + tm * sum(_INPROJ_WIDTHS) * 2 + 4 * tm * LANES * 4) + 12 * MIB
    return pl.pallas_call(
        _inproj_kernel,
        out_shape=[jax.ShapeDtypeStruct((b, t, wd), BF16) for wd in _INPROJ_WIDTHS],
        grid=(b, t // tm),
        in_specs=[pl.BlockSpec((1, tm, d), lambda bi, i: (bi, i, 0)), vec, vec, const((1, d)),
                  const((d, _IN_COLS)), const((1, MLA_RANK)), const((MLA_RANK + LANES, MLA_WIDTH)),
                  const((MLA_RANK, GROUP_WIDTH)), tab, tab, tab, tab],
        out_specs=[pl.BlockSpec((1, tm, wd), lambda bi, i: (bi, i, 0)) for wd in _INPROJ_WIDTHS],
        compiler_params=_cparams(("arbitrary", "arbitrary"), est),
    )(x, shift, scale, g, w, kvg, wk, wv, *tabs)


def _attend_pair(qs, pieces, sinks):
    lo = _low_half((1, LANES))
    m0, m1 = _half_masks()
    outs = []
    for s in (0, 1):
        own, other = (m0, m1) if s == 0 else (m1, m0)
        scores = []
        for ks, _, biases in pieces:
            sc = _dot_nt(qs[s], ks[s])
            if biases[s] is not None:
                sc = sc + biases[s]
            scores.append(sc)
        m = functools.reduce(jnp.maximum, [jnp.max(sc, axis=-1, keepdims=True) for sc in scores])
        if sinks is not None:
            m = jnp.maximum(m, sinks[s])
        o = 0.0
        for sc, (_, v, _) in zip(scores, pieces):
            o = o + _dot(jnp.exp(sc - m).astype(BF16), v * own + other)
        own_lanes = lo if s == 0 else jnp.logical_not(lo)
        den = jnp.where(own_lanes, 1.0, o)
        if sinks is not None:
            den = den + jnp.exp(sinks[s] - m)
        outs.append(o * pltpu.roll(1.0 / den, HEAD_DIM, 1))
    return jnp.where(lo, outs[0], outs[1])


def _mla_kernel(q_ref, kc_ref, vc_ref, kl_ref, vl_ref, o_ref, m_ref, l_ref, acc_ref):
    j = pl.program_id(2)

    @pl.when(j == 0)
    def _():
        m_ref[...] = jnp.full(m_ref.shape, NEG_INF, F32)
        l_ref[...] = jnp.zeros(l_ref.shape, F32)
        acc_ref[...] = jnp.zeros(acc_ref.shape, F32)

    def update(k, v):
        lo = _low_half((1, LANES))
        m0, m1 = _half_masks()
        reps = k.shape[0] // LANES
        for pair in range(2):
            vp = v[:, pair * LANES:(pair + 1) * LANES]
            vals = (vp * m0 + m1, vp * m1 + m0)
            alphas, pvs = [], []
            for s in (0, 1):
                hd = 2 * pair + s
                q = q_ref[0, :, hd * LANES:(hd + 1) * LANES]
                sc = _dot_nt(q, k[:, hd * LANES:(hd + 1) * LANES])
                m_prev = m_ref[hd]
                m_new = jnp.maximum(m_prev, jnp.max(sc, axis=-1, keepdims=True))
                m_ref[hd] = m_new
                alphas.append(jnp.exp(m_prev - m_new))
                p = jnp.exp(sc - jnp.tile(m_new, (1, reps)))
                pvs.append(_dot(p.astype(BF16), vals[s]))
            acc_ref[pair] = (acc_ref[pair] * jnp.where(lo, alphas[0], alphas[1])
                             + jnp.where(lo, pvs[0], pvs[1]))
            l_ref[pair] = (l_ref[pair] * jnp.where(lo, alphas[1], alphas[0])
                           + jnp.where(lo, pvs[1], pvs[0]))

    @pl.when(j == 0)
    def _():
        update(kc_ref[0], vc_ref[0])

    @pl.when(j > 0)
    def _():
        update(kl_ref[0], vl_ref[0])

    @pl.when(j == pl.num_programs(2) - 1)
    def _():
        for pair in range(2):
            inv = pltpu.roll(1.0 / l_ref[pair], HEAD_DIM, 1)
            o_ref[0, :, pair * LANES:(pair + 1) * LANES] = (acc_ref[pair] * inv).astype(BF16)


def _mla_attention(q, kc, vc, kl, vl):
    b, n, _ = q.shape
    c = kc.shape[1]
    tq = min(2048, n)
    tk = min(1024, n)
    kv_width = MLA_WIDTH + GROUP_WIDTH
    est = (2 * 2 * (tq * MLA_WIDTH + (c + tk) * kv_width + tq * GROUP_WIDTH) + 6 * tq * LANES * 4
           + 10 * tq * tk * 4 + 8 * MIB)
    return pl.pallas_call(
        _mla_kernel,
        out_shape=jax.ShapeDtypeStruct((b, n, GROUP_WIDTH), BF16),
        grid=(b, n // tq, 1 + n // tk),
        in_specs=[pl.BlockSpec((1, tq, MLA_WIDTH), lambda bi, i, j: (bi, i, 0)),
                  pl.BlockSpec((1, c, MLA_WIDTH), lambda bi, i, j: (bi, 0, 0)),
                  pl.BlockSpec((1, c, GROUP_WIDTH), lambda bi, i, j: (bi, 0, 0)),
                  pl.BlockSpec((1, tk, MLA_WIDTH), lambda bi, i, j: (bi, jnp.maximum(j - 1, 0), 0)),
                  pl.BlockSpec((1, tk, GROUP_WIDTH), lambda bi, i, j: (bi, jnp.maximum(j - 1, 0), 0))],
        out_specs=pl.BlockSpec((1, tq, GROUP_WIDTH), lambda bi, i, j: (bi, i, 0)),
        scratch_shapes=[pltpu.VMEM((4, tq, LANES), F32), pltpu.VMEM((2, tq, LANES), F32), pltpu.VMEM((2, tq, LANES), F32)],
        compiler_params=_cparams(("arbitrary", "arbitrary", "arbitrary"), est),
    )(q, kc, vc, kl, vl)


def _swa_kernel(q_ref, kp_ref, kcur_ref, kn_ref, vp_ref, vcur_ref, vn_ref, kctx_ref, vctx_ref, sink_ref, o_ref):
    i = pl.program_id(1)
    last = pl.num_programs(1) - 1
    tq = q_ref.shape[1]

    def band_bias(width, offset, edge_penalty):
        r = lax.broadcasted_iota(jnp.int32, (tq, width), 0)
        cidx = lax.broadcasted_iota(jnp.int32, (tq, width), 1)
        rel = cidx + offset - r
        inside = jnp.where(rel >= -SWA_WINDOW, jnp.where(rel <= SWA_WINDOW, 1, 0), 0)
        return jnp.where(inside == 1, edge_penalty, NEG_INF)

    halo = kp_ref.shape[1]
    b_prev = band_bias(halo, -halo, jnp.where(i > 0, 0.0, NEG_INF))
    b_cur = band_bias(tq, 0, 0.0)
    b_next = band_bias(halo, tq, jnp.where(i < last, 0.0, NEG_INF))
    m0, m1 = _half_masks()
    kctx, vctx = kctx_ref[0], vctx_ref[0]
    pieces = [((kp_ref[0],) * 2, vp_ref[0], (b_prev,) * 2),
              ((kcur_ref[0],) * 2, vcur_ref[0], (b_cur,) * 2),
              ((kn_ref[0],) * 2, vn_ref[0], (b_next,) * 2),
              ((kctx,) * 2, vctx, (None, None))]
    for grp in range(2):
        qg = q_ref[0, :, grp * LANES:(grp + 1) * LANES]
        sinks = [sink_ref[2 * grp + s:2 * grp + s + 1, 0:1] for s in (0, 1)]
        o_ref[0, :, grp * LANES:(grp + 1) * LANES] = _attend_pair((qg * m0, qg * m1), pieces, sinks).astype(BF16)


def _na_kernel(q_ref, kp_ref, kcur_ref, kn_ref, vp_ref, vcur_ref, vn_ref, kctx_ref, vctx_ref, bias_ref, o_ref):
    tq = q_ref.shape[1]
    m0, m1 = _half_masks()
    for pair in range(2):
        sl = slice(pair * LANES, (pair + 1) * LANES)
        qp = q_ref[0, :, sl]
        pieces = []
        for idx, (kr, vr) in enumerate(((kp_ref, vp_ref), (kcur_ref, vcur_ref), (kn_ref, vn_ref))):
            biases = tuple(bias_ref[0, 2 * pair + s, :, idx * tq:(idx + 1) * tq] for s in (0, 1))
            pieces.append(((kr[0, :, sl],) * 2, vr[0, :, sl], biases))
        pieces.append(((kctx_ref[0, :, sl],) * 2, vctx_ref[0, :, sl], (None, None)))
        o_ref[0, :, sl] = _attend_pair((qp * m0, qp * m1), pieces, None).astype(BF16)


def _na_bias_tables(rpb, rows):
    nb = rows // NA_ROWS
    heads = rpb.shape[0]
    width = 2 * NA_KW - 1
    span = 2 * GRID_W - 1
    lead = GRID_W - 2 - (NA_KW - 1)
    rpb = rpb.astype(F32)
    per_row = jnp.stack([rpb[:, NA_ROWS - 1 - a:NA_ROWS - 1 - a + 3 * NA_ROWS, :] for a in range(NA_ROWS)], axis=1)
    padded = jnp.pad(per_row, ((0, 0), (0, 0), (0, 0), (lead, span - width - lead)))
    tiled = jnp.broadcast_to(padded[:, :, :, None, :], (heads, NA_ROWS, 3 * NA_ROWS, GRID_W, span))
    skew = tiled.reshape(heads, NA_ROWS, 3 * NA_ROWS, GRID_W * span)[..., :GRID_W * (span - 1)]
    toep = skew.reshape(heads, NA_ROWS, 3 * NA_ROWS, GRID_W, span - 1)[..., GRID_W - 2:]
    table = jnp.transpose(toep, (0, 1, 3, 2, 4)).reshape(heads, NA_ROWS * GRID_W, 3 * NA_ROWS * GRID_W)
    a = np.arange(NA_ROWS)[:, None, None, None]
    cq = np.arange(GRID_W)[None, :, None, None]
    kr = np.arange(3 * NA_ROWS)[None, None, :, None]
    ck = np.arange(GRID_W)[None, None, None, :]
    valids = []
    for j in (0, 1, nb - 1):
        r = NA_ROWS * j + a
        rs = np.clip(r - NA_KH // 2, 0, rows - NA_KH)
        rk = NA_ROWS * (j - 1) + kr
        cs = np.clip(cq - NA_KW // 2, 0, GRID_W - NA_KW)
        valid = (rk >= rs) & (rk < rs + NA_KH) & (ck >= cs) & (ck < cs + NA_KW)
        valids.append(valid.reshape(NA_ROWS * GRID_W, 3 * NA_ROWS * GRID_W))
    valid = jnp.asarray(np.stack(valids))
    return jnp.where(valid[:, None], table[None], NEG_INF)


def _local_kernel(*refs):
    _swa_kernel(*refs[:10], refs[20])
    _na_kernel(*refs[10:20], refs[21])


def _local_attention(sq, sk, sv, skc, svc, sink_rows, nq, nk, nv, nkc, nvc, bias):
    b, n, _ = sq.shape
    c = skc.shape[1]
    tq = NA_ROWS * GRID_W
    nb = n // tq
    halo = SWA_WINDOW
    r = tq // halo
    nh = n // halo
    cur = lambda bi, i: (bi, i, 0)
    prev = lambda bi, i: (bi, jnp.maximum(i - 1, 0), 0)
    nxt = lambda bi, i: (bi, jnp.minimum(i + 1, nb - 1), 0)
    hprev = lambda bi, i: (bi, jnp.maximum(i * r - 1, 0), 0)
    hnxt = lambda bi, i: (bi, jnp.minimum(i * r + r, nh - 1), 0)
    ctx = lambda bi, i: (bi, 0, 0)
    variant = lambda bi, i: (jnp.where(i == 0, 0, jnp.where(i == nb - 1, 2, 1)), 0, 0, 0)
    blk = pl.BlockSpec((1, tq, GROUP_WIDTH), cur)
    halo_specs = [pl.BlockSpec((1, halo, LANES), hprev), pl.BlockSpec((1, tq, LANES), cur),
                  pl.BlockSpec((1, halo, LANES), hnxt)]
    win_specs = [pl.BlockSpec((1, tq, GROUP_WIDTH), prev), blk, pl.BlockSpec((1, tq, GROUP_WIDTH), nxt)]
    in_specs = ([blk] + halo_specs + halo_specs + [pl.BlockSpec((1, c, LANES), ctx)] * 2
                + [pl.BlockSpec((4, LANES), lambda bi, i: (0, 0))]
                + [blk] + win_specs + win_specs + [pl.BlockSpec((1, c, GROUP_WIDTH), ctx)] * 2
                + [pl.BlockSpec((1, 4, tq, 3 * tq), variant)])
    est = 2 * 4 * tq * 3 * tq * 4 + 32 * tq * (3 * tq + c) * 4 + 8 * MIB
    return pl.pallas_call(
        _local_kernel,
        out_shape=[jax.ShapeDtypeStruct((b, n, GROUP_WIDTH), BF16)] * 2,
        grid=(b, nb),
        in_specs=in_specs,
        out_specs=[blk, blk],
        compiler_params=_cparams(("arbitrary", "arbitrary"), est),
    )(sq, sk, sk, sk, sv, sv, sv, skc, svc, sink_rows, nq, nk, nk, nk, nv, nv, nv, nkc, nvc, bias)


def _ctx_attn_kernel(mq_ref, mk_ref, mv_ref, sq_ref, sk_ref, sv_ref, sink_ref, nq_ref, nk_ref, nv_ref,
                     ya_ref, yb_ref, yc_ref):
    m0, m1 = _half_masks()
    none2 = (None, None)
    for pair in range(2):
        sl = slice(pair * LANES, (pair + 1) * LANES)
        h0 = slice(2 * pair * LANES, (2 * pair + 1) * LANES)
        h1 = slice((2 * pair + 1) * LANES, (2 * pair + 2) * LANES)
        ya_ref[0, :, sl] = _attend_pair((mq_ref[0, :, h0], mq_ref[0, :, h1]),
                                        [((mk_ref[0, :, h0], mk_ref[0, :, h1]), mv_ref[0, :, sl], none2)],
                                        None).astype(BF16)
        qg = sq_ref[0, :, sl]
        sinks = [sink_ref[2 * pair + s:2 * pair + s + 1, 0:1] for s in (0, 1)]
        yb_ref[0, :, sl] = _attend_pair((qg * m0, qg * m1), [((sk_ref[0],) * 2, sv_ref[0], none2)],
                                        sinks).astype(BF16)
        qn = nq_ref[0, :, sl]
        yc_ref[0, :, sl] = _attend_pair((qn * m0, qn * m1), [((nk_ref[0, :, sl],) * 2, nv_ref[0, :, sl], none2)],
                                        None).astype(BF16)


def _ctx_attention(mq, mk, mv, sq, sk, sv, sink_rows, nq, nk, nv):
    b, c, _ = mq.shape
    spec = lambda wd: pl.BlockSpec((1, c, wd), lambda bi: (bi, 0, 0))
    return pl.pallas_call(
        _ctx_attn_kernel,
        out_shape=[jax.ShapeDtypeStruct((b, c, GROUP_WIDTH), BF16)] * 3,
        grid=(b,),
        in_specs=[spec(MLA_WIDTH), spec(MLA_WIDTH), spec(GROUP_WIDTH), spec(GROUP_WIDTH), spec(LANES), spec(LANES),
                  pl.BlockSpec((4, LANES), lambda bi: (0, 0)), spec(GROUP_WIDTH), spec(GROUP_WIDTH), spec(GROUP_WIDTH)],
        out_specs=[spec(GROUP_WIDTH)] * 3,
        compiler_params=_cparams(("arbitrary",), 24 * MIB),
    )(mq, mk, mv, sq, sk, sv, sink_rows, nq, nk, nv)


def _ret_tables(dec_lane, reverse, c):
    lg = -(jnp.maximum(-dec_lane, 0.0) + jnp.log1p(jnp.exp(-jnp.abs(dec_lane))))
    ti = lax.broadcasted_iota(jnp.int32, (c, LANES), 0).astype(F32)
    if reverse:
        qpow, kpow = c - ti, ti
    else:
        qpow, kpow = ti + 1.0, c - 1.0 - ti
    ii = lax.broadcasted_iota(jnp.int32, (c, c), 0)
    jj = lax.broadcasted_iota(jnp.int32, (c, c), 1)
    dist = (jj - ii) if reverse else (ii - jj)
    distf = jnp.maximum(dist, 0).astype(F32)
    masks = [jnp.where(dist >= 0, jnp.exp(lg[:, s * HEAD_DIM:s * HEAD_DIM + 1] * distf), 0.0) for s in (0, 1)]
    return jnp.exp(lg * qpow), jnp.exp(lg * kpow), masks[0], masks[1], jnp.exp(lg * float(c))


def _ret_chunk(q, k, v, g, qdec, kdec, decays, cdec, gn_lane, s_ref):
    lo = _low_half((1, LANES))
    ii = lax.broadcasted_iota(jnp.int32, (LANES, LANES), 0)
    jj = lax.broadcasted_iota(jnp.int32, (LANES, LANES), 1)
    outs = []
    for msk, decay in zip(_half_masks(), decays):
        inner = _dot_nt(q * msk, k) * decay
        outs.append(_dot(inner.astype(BF16), v))
    state = s_ref[...]
    o = jnp.where(lo, outs[0], outs[1]) + _dot((q.astype(F32) * qdec).astype(BF16), state.astype(BF16))
    kd = (k.astype(F32) * kdec).astype(BF16)
    same_head = (ii < HEAD_DIM) == (jj < HEAD_DIM)
    s_ref[...] = state * cdec + jnp.where(same_head, _dot_tn(kd, v), 0.0)

    def head_mean(x):
        s_lo = jnp.sum(jnp.where(lo, x, 0.0), axis=-1, keepdims=True)
        s_hi = jnp.sum(jnp.where(lo, 0.0, x), axis=-1, keepdims=True)
        return jnp.where(lo, s_lo, s_hi) * (1.0 / HEAD_DIM)

    dev = o - head_mean(o)
    normed = dev * lax.rsqrt(head_mean(dev * dev) + EPS)
    gf = g.astype(F32)
    return (gf * jax.nn.sigmoid(gf)) * (normed * gn_lane)


def _ret_kernel(decf_ref, decb_ref, gnf_ref, gnb_ref, s0f_ref, s0b_ref,
                qf_ref, kf_ref, vf_ref, gf_ref, qb_ref, kb_ref, vb_ref, gb_ref,
                yf_ref, yb_ref, sf_ref, sb_ref, st_scr, tab_scr, mask_scr, cdec_scr):
    t = pl.program_id(1)
    directions = ((decf_ref, gnf_ref, qf_ref, kf_ref, vf_ref, gf_ref, yf_ref),
                  (decb_ref, gnb_ref, qb_ref, kb_ref, vb_ref, gb_ref, yb_ref))

    batch = qf_ref.shape[0]

    @pl.when(t == 0)
    def _():
        for e in range(batch):
            st_scr[e, 0] = s0f_ref[e]
            st_scr[e, 1] = s0b_ref[e]
        for pair in range(2):
            for d, refs in enumerate(directions):
                qdec, kdec, mask0, mask1, cdec = _ret_tables(refs[0][pair], d == 1, qf_ref.shape[1])
                tab_scr[d, pair, 0] = qdec
                tab_scr[d, pair, 1] = kdec
                mask_scr[d, pair, 0] = mask0
                mask_scr[d, pair, 1] = mask1
                cdec_scr[d, pair] = jnp.broadcast_to(cdec, (8, LANES))

    for e in range(batch):
        for d, (_, gn_ref, q_ref, k_ref, v_ref, g_ref, y_ref) in enumerate(directions):
            for pair in range(2):
                sl = slice(pair * LANES, (pair + 1) * LANES)
                y = _ret_chunk(q_ref[e, :, sl], k_ref[e, :, sl], v_ref[e, :, sl], g_ref[e, :, sl],
                               tab_scr[d, pair, 0], tab_scr[d, pair, 1], (mask_scr[d, pair, 0], mask_scr[d, pair, 1]),
                               cdec_scr[d, pair, 0:1, :], gn_ref[:, sl], st_scr.at[e, d, pair])
                y_ref[e, :, sl] = y.astype(BF16)

    @pl.when(t == pl.num_programs(1) - 1)
    def _():
        for e in range(batch):
            sf_ref[e] = st_scr[e, 0]
            sb_ref[e] = st_scr[e, 1]


def _retention(q, k, v, gf, gb, dec_f, dec_b, gn_f, gn_b, s0f, s0b):
    b, t, width = q.shape
    c = min(RET_CHUNK, t)
    nt = t // c
    bb = RET_BATCH if b % RET_BATCH == 0 else 1
    fwd = pl.BlockSpec((bb, c, width), lambda bi, i: (bi, i, 0))
    bwd = pl.BlockSpec((bb, c, width), lambda bi, i: (bi, nt - 1 - i, 0))
    dec = pl.BlockSpec((2, 1, LANES), lambda bi, i: (0, 0, 0))
    gn = pl.BlockSpec((1, width), lambda bi, i: (0, 0))
    st = pl.BlockSpec((bb, 2, LANES, LANES), lambda bi, i: (bi, 0, 0, 0))
    return pl.pallas_call(
        _ret_kernel,
        out_shape=[jax.ShapeDtypeStruct((b, t, width), BF16)] * 2 + [jax.ShapeDtypeStruct((b, 2, LANES, LANES), F32)] * 2,
        grid=(b // bb, nt),
        in_specs=[dec, dec, gn, gn, st, st, fwd, fwd, fwd, fwd, bwd, bwd, bwd, bwd],
        out_specs=[fwd, bwd, st, st],
        scratch_shapes=[pltpu.VMEM((bb, 2, 2, LANES, LANES), F32), pltpu.VMEM((2, 2, 2, c, LANES), F32),
                        pltpu.VMEM((2, 2, 2, c, c), F32),
                        pltpu.VMEM((2, 2, 8, LANES), F32)],
        compiler_params=_cparams(("arbitrary", "arbitrary"), 24 * MIB),
    )(dec_f, dec_b, gn_f, gn_b, s0f, s0b, q, k, v, gf, q, k, v, gb)


def _outproj_kernel(x_ref, ya_ref, yb_ref, yc_ref, ydf_ref, ydb_ref, beta_ref, w_ref, gate_ref, o_ref):
    ys = (ya_ref[0].astype(F32), yb_ref[0].astype(F32), yc_ref[0].astype(F32),
          ydf_ref[0].astype(F32) + ydb_ref[0].astype(F32))
    acc = 0.0
    for s, y in enumerate(ys):
        sl = slice(s * GROUP_WIDTH, (s + 1) * GROUP_WIDTH)
        acc = acc + _dot((y * beta_ref[:, sl]).astype(BF16), w_ref[sl, :])
    o_ref[0] = x_ref[0] + gate_ref[0] * acc


def _out_projection(x, ya, yb, yc, ydf, ydb, beta, w, gate):
    b, t, d = x.shape
    tm = min(512, t)
    row = lambda wd: pl.BlockSpec((1, tm, wd), lambda bi, i: (bi, i, 0))
    est = 2 * (2 * tm * d * 4 + 5 * tm * GROUP_WIDTH * 2 + d * d * 2) + 8 * MIB
    return pl.pallas_call(
        _outproj_kernel,
        out_shape=jax.ShapeDtypeStruct((b, t, d), F32),
        grid=(b, t // tm),
        in_specs=[row(d), row(GROUP_WIDTH), row(GROUP_WIDTH), row(GROUP_WIDTH), row(GROUP_WIDTH), row(GROUP_WIDTH),
                  pl.BlockSpec((1, d), lambda bi, i: (0, 0)), pl.BlockSpec((d, d), lambda bi, i: (0, 0)),
                  pl.BlockSpec((1, 1, d), lambda bi, i: (bi, 0, 0))],
        out_specs=row(d),
        compiler_params=_cparams(("arbitrary", "arbitrary"), est),
    )(x, ya, yb, yc, ydf, ydb, beta, w, gate)


def _sorting_network(n):
    size = 1
    while size < n:
        size *= 2
    pairs = []
    p = 1
    while p < size:
        k = p
        while k >= 1:
            for j in range(k % p, size - k, 2 * k):
                for i in range(min(k, size - j - k)):
                    if (i + j) // (2 * p) == (i + j + k) // (2 * p):
                        pairs.append((i + j, i + j + k))
            k //= 2
        p *= 2
    return [(i, j) for i, j in pairs if j < n]


def _top_values(scores, out_ref, count):
    groups = scores.shape[0] // 8
    lists = [scores[8 * g:8 * g + 8, :] for g in range(groups)]
    for i, j in _sorting_network(groups):
        lists[i], lists[j] = jnp.maximum(lists[i], lists[j]), jnp.minimum(lists[i], lists[j])
    for r in range(count):
        m = jnp.max(lists[0], axis=0, keepdims=True)
        out_ref[r:r + 1, :] = m
        hit = lists[0] == m
        need = count - r - 1
        for p in range(min(groups - 1, need)):
            lists[p] = jnp.where(hit, lists[p + 1], lists[p])
        if need >= groups:
            lists[groups - 1] = jnp.where(hit, -jnp.inf, lists[groups - 1])


def _staircase(w1, w2):
    row8 = lax.broadcasted_iota(jnp.int32, (8, w1.shape[1]), 0)
    cands = [w1[0:1] + w2]
    for a in range(1, 8):
        cands.append(jnp.where(row8 < PEER_TOPK // (a + 1), w1[a:a + 1] + w2[0:8], -jnp.inf))
    cands.append(w1[8:16] + w2[0:1])
    return jnp.concatenate(cands, axis=0)


def _bf16_twice(x):
    hi = pltpu.bitcast(x.astype(BF16).astype(F32), jnp.uint32)
    return hi | (hi >> 16)


def _bf16_rows(words, rows):
    return pltpu.bitcast(jnp.broadcast_to(words, (rows // 2, words.shape[1])), BF16)


def _peer_route(hd, q_scr, k1_ref, k2_ref, rk_scr, e2_scr, n_scr, e1_scr, v1_scr, v2_scr, top_scr):
    half = PEER_NKEYS
    base = pl.multiple_of(hd * 2 * half, 2 * half)
    s1 = _dot(k1_ref[...], q_scr[pl.ds(base, half), :])
    s2 = _dot(k2_ref[...], q_scr[pl.ds(base + half, half), :])
    _top_values(s1, v1_scr, PEER_TOPK)
    _top_values(s2, v2_scr, PEER_TOPK)
    v1, v2 = v1_scr[...], v2_scr[...]
    cand = _staircase(v1, v2)
    _top_values(cand, top_scr, PEER_TOPK)
    top = top_scr[...]
    z = jnp.sum(jnp.exp(top - top[0:1]), axis=0, keepdims=True)
    sel = jnp.where(cand >= top[PEER_TOPK - 1:PEER_TOPK], 1.0, 0.0)
    count = lambda picked: jnp.sum(picked, axis=0, keepdims=True)
    lens = [count(sel[0:16])] + [count(sel[8 + 8 * a:16 + 8 * a]) for a in range(1, 8)]
    lens += [sel[72 + a:73 + a] for a in range(8)]
    n_sel = jnp.zeros(s1.shape, F32)
    for a in reversed(range(PEER_TOPK)):
        n_sel = jnp.where(s1 >= v1[a:a + 1], lens[a], n_sel)
    rank = jnp.zeros(s2.shape, F32)
    for b in range(PEER_TOPK):
        rank = rank + jnp.where(s2 < v2[b:b + 1], 1.0, 0.0)
    rk_scr[hd] = rank.astype(BF16)
    n_scr[hd] = _bf16_twice(n_sel)
    e1_scr[hd] = _bf16_twice(jnp.exp(s1 - v1[0:1]) / z)
    e2_scr[hd] = (jnp.exp(s2 - v2[0:1]) * 0.5).astype(BF16)


def _peer_kernel(*refs, final_norm):
    if final_norm:
        (x_ref, sh_ref, sc_ref, gt_ref, g_ref, wqt_ref, k1_ref, k2_ref, u_ref, vt_ref, fg_ref, o_ref,
         ht_all, q_scr, rk_all, e2_all, n_all, e1_all, v1_scr, v2_scr, top_scr, a0_all, a1_all, acc_all) = refs
    else:
        (x_ref, sh_ref, sc_ref, gt_ref, g_ref, wqt_ref, k1_ref, k2_ref, u_ref, vt_ref, o_ref,
         ht_all, q_scr, rk_all, e2_all, n_all, e1_all, v1_scr, v2_scr, top_scr, a0_all, a1_all, acc_all) = refs
        fg_ref = None
    step = pl.program_id(2)
    sub = pl.program_id(3)
    ht_scr, rk_scr, e2_scr, n_scr, e1_scr, acc_scr = (r.at[sub] for r in (ht_all, rk_all, e2_all, n_all, e1_all, acc_all))
    tokens = ht_all.shape[2]
    tok_rows = pl.ds(pl.multiple_of(sub * tokens, tokens), tokens)

    @pl.when(step == 0)
    def _():
        h = _norm_mod(x_ref[0, tok_rows, :], g_ref[...], sh_ref[0], sc_ref[0])
        ht = jnp.transpose(h).astype(BF16)
        ht_scr[...] = ht
        q_scr[...] = _dot(wqt_ref[...], ht).astype(BF16)
        acc_scr[...] = jnp.zeros(acc_scr.shape, F32)

        def route(pair, carry):
            for k in range(2):
                _peer_route(2 * pair + k, q_scr, k1_ref, k2_ref, rk_scr, e2_scr, n_scr, e1_scr,
                            v1_scr.at[k], v2_scr.at[k], top_scr.at[k])
            return carry

        lax.fori_loop(0, PEER_HEADS // 2, route, 0)

    def project(a_write):
        a_write[...] = _dot(u_ref[...], ht_scr[...])

    def consume(a_read):
        y = None
        piece = PEER_PIECE_ROWS * PEER_NKEYS
        for kt in range(PEER_ROWS // PEER_PIECE_ROWS):
            ws = []
            for r in range(PEER_PIECE_ROWS * kt, PEER_PIECE_ROWS * (kt + 1)):
                rows = slice(r * PEER_NKEYS, (r + 1) * PEER_NKEYS)
                i = (step - 1) * PEER_ROWS + r
                gate = jnp.zeros((PEER_NKEYS, tokens), BF16)
                for hd in range(PEER_HEADS):
                    n_row = _bf16_rows(n_scr[hd, pl.ds(i, 1), :], PEER_NKEYS)
                    e1_row = _bf16_rows(e1_scr[hd, pl.ds(i, 1), :], PEER_NKEYS)
                    gate = gate + jnp.where(rk_scr[hd] < n_row, e2_scr[hd] * e1_row, jnp.zeros_like(gate))
                a = a_read[rows, :].astype(BF16)
                gelu2 = a * (1.0 + lax.erf(a * np.sqrt(0.5).astype(BF16)))
                ws.append(gelu2 * gate)
            part = _dot(vt_ref[:, kt * piece:(kt + 1) * piece], jnp.concatenate(ws, axis=0))
            y = part if y is None else y + part
        acc_scr[...] += y

    last = pl.num_programs(2) - 1
    bufs = (a0_all.at[sub], a1_all.at[sub])

    @pl.when(step == 0)
    def _():
        project(bufs[0])

    for parity in (0, 1):
        @pl.when(jnp.logical_and(jnp.logical_and(step > 0, step < last), step % 2 == parity))
        def _():
            project(bufs[parity])
            consume(bufs[1 - parity])

        @pl.when(jnp.logical_and(step == last, step % 2 == parity))
        def _():
            consume(bufs[1 - parity])

    @pl.when(step == last)
    def _():
        out = x_ref[0, tok_rows, :] + gt_ref[0] * jnp.transpose(acc_scr[...])
        if final_norm:
            out = _rmsnorm(out, fg_ref[...])
        o_ref[0, tok_rows, :] = out


def _peer_ffn(x, shift, scale, gate, g, wqt, k1, k2, u, vt, final_g=None):
    b, t, d = x.shape
    tt = PEER_TOKENS
    share = PEER_SHARE if t % (PEER_SHARE * tt) == 0 else 1
    ec = PEER_ROWS * PEER_NKEYS
    n_blocks = u.shape[0] // ec
    nq = wqt.shape[0]
    final_norm = final_g is not None
    tok = pl.BlockSpec((1, share * tt, d), lambda bi, i, s, k: (bi, i, 0))
    vec = pl.BlockSpec((1, 1, d), lambda bi, i, s, k: (bi, 0, 0))
    const = lambda shape: pl.BlockSpec(shape, lambda bi, i, s, k: (0,) * len(shape))
    in_specs = [tok, vec, vec, vec, const((1, d)), const((nq, d)), const(k1.shape), const(k2.shape),
                pl.BlockSpec((ec, d), lambda bi, i, s, k: (jnp.minimum(s, n_blocks - 1), 0)),
                pl.BlockSpec((d, ec), lambda bi, i, s, k: (0, jnp.clip(s - 1, 0, n_blocks - 1)))]
    args = [x, shift, scale, gate, g, wqt, k1, k2, u, vt]
    if final_norm:
        in_specs.append(const((1, d)))
        args.append(final_g)
    table = pltpu.VMEM((share, PEER_HEADS, PEER_NKEYS, tt), jnp.uint32)
    table16 = pltpu.VMEM((share, PEER_HEADS, PEER_NKEYS, tt), BF16)
    top = pltpu.VMEM((2, PEER_TOPK, tt), F32)
    a_buf = pltpu.VMEM((share, ec, tt), F32)
    scratch = [pltpu.VMEM((share, d, tt), BF16), pltpu.VMEM((nq, tt), BF16), table16, table16, table, table,
               top, top, top, a_buf, a_buf, pltpu.VMEM((share, d, tt), F32)]
    est = (4 * share * tt * d * 4 + 2 * nq * d * 2 + 4 * ec * d * 2 + nq * tt * 2
           + share * (tt * d * 2 + 12 * PEER_HEADS * PEER_NKEYS * tt + d * tt * 4 + 2 * ec * tt * 4) + 8 * MIB)
    return pl.pallas_call(
        functools.partial(_peer_kernel, final_norm=final_norm),
        out_shape=jax.ShapeDtypeStruct((b, t, d), F32),
        grid=(b, t // (share * tt), n_blocks + 1, share),
        in_specs=in_specs,
        out_specs=tok,
        scratch_shapes=scratch,
        compiler_params=_cparams(("arbitrary",) * 4, est),
    )(*args)


def _prep_in_weights(w_in):
    d = w_in.shape[0]
    zeros = lambda n: jnp.zeros((d, n), w_in.dtype)
    cols = []
    for hd in range(4):
        cols += [w_in[:, hd * MLA_QK:(hd + 1) * MLA_QK], zeros(LANES - MLA_QK)]
    cols.append(w_in[:, 384:512])
    cols += [zeros(HEAD_DIM), w_in[:, 512:544], zeros(LANES - HEAD_DIM - MLA_ROPE)]
    for hd in (0, 2, 1, 3):
        cols.append(w_in[:, 544 + hd * HEAD_DIM:544 + (hd + 1) * HEAD_DIM])
    cols.append(w_in[:, 800:])
    return jnp.concatenate(cols, axis=1).astype(BF16)


def _prep_kv_weights(w_ukv):
    zeros = jnp.zeros((MLA_RANK, HEAD_DIM), w_ukv.dtype)
    wkn = jnp.concatenate([blk for hd in range(4) for blk in (w_ukv[:, hd * LANES:hd * LANES + HEAD_DIM], zeros)], axis=1)
    wv = jnp.concatenate([w_ukv[:, hd * LANES + HEAD_DIM:(hd + 1) * LANES] for hd in range(4)], axis=1)
    sel = np.zeros((LANES, 4 * LANES), np.float32)
    for hd in range(4):
        for r in range(MLA_ROPE):
            sel[HEAD_DIM + r, hd * LANES + HEAD_DIM + r] = 1.0
    return jnp.concatenate([wkn.astype(BF16), jnp.asarray(sel, BF16)], axis=0), wv.astype(BF16)


def _rope_lane_tables(n):
    def tables(rot_dim):
        quarter = rot_dim // 4
        inv = ROPE_BASE ** (-jnp.arange(quarter, dtype=F32) / quarter)
        t = jnp.arange(n, dtype=jnp.int32)
        pos = jnp.stack([t // GRID_W, t % GRID_W], axis=-1).astype(F32)
        ang = pos[:, :, None] * inv
        lanes = lambda tb: jnp.concatenate([tb[:, 0], tb[:, 0], tb[:, 1], tb[:, 1]], axis=-1)
        return lanes(jnp.cos(ang)), lanes(jnp.sin(ang))

    c32, s32 = tables(MLA_ROPE)
    c64, s64 = tables(HEAD_DIM)
    ones = lambda w: jnp.ones((n, w), F32)
    zeros = lambda w: jnp.zeros((n, w), F32)
    return (jnp.concatenate([ones(HEAD_DIM), c32, ones(LANES - HEAD_DIM - MLA_ROPE)], axis=-1),
            jnp.concatenate([zeros(HEAD_DIM), s32, zeros(LANES - HEAD_DIM - MLA_ROPE)], axis=-1),
            jnp.concatenate([c64, c64], axis=-1), jnp.concatenate([s64, s64], axis=-1))


_SWA_SLOT_HEADS = (0, 2, 1, 3)


def kernel(x, c, ctx, c_ctx, ada_w, ada_b, norm_mix_g, w_in, mla_kv_norm_g, mla_w_ukv, swa_sink, na_rpb, ret_decay_f, ret_decay_b, ret_gn_f, ret_gn_b, mix_beta, w_out, norm_ffn_g, peer_wq, peer_k1, peer_k2, peer_u, peer_v, final_norm_g):
    b, n, d = x.shape
    n_ctx = ctx.shape[1]
    depth = ada_w.shape[0]
    rows = n // GRID_W

    pad_rows = -(b + 1) % 8
    cc = jnp.concatenate([c, c_ctx[None, :], jnp.zeros((pad_rows, d), F32)], axis=0)
    mod = _ada_modulation(cc, ada_w, ada_b)

    lat_tabs = _rope_lane_tables(n)
    ctx_tabs = (jnp.ones((n_ctx, LANES), F32), jnp.zeros((n_ctx, LANES), F32)) * 2
    slot_heads = np.asarray(_SWA_SLOT_HEADS)
    out_perm = np.arange(d)
    out_perm[GROUP_WIDTH:2 * GROUP_WIDTH] = GROUP_WIDTH + (slot_heads[:, None] * HEAD_DIM + np.arange(HEAD_DIM)[None, :]).reshape(-1)
    zero_state = jnp.zeros((b, 2, LANES, LANES), F32)
    pair_lanes = lambda p: jnp.repeat(p.astype(F32), HEAD_DIM).reshape(2, 1, LANES)

    xc = ctx
    for layer in range(depth):
        with_ctx = layer < depth - 1
        last = layer == depth - 1
        chunk = lambda k, lo, hi: mod[layer, lo:hi, k * d:(k + 1) * d][:, None, :]
        mod_l = [chunk(k, 0, b) for k in range(6)]
        mod_c = [jnp.broadcast_to(chunk(k, b, b + 1), (b, 1, d)) for k in range(6)]

        w_cols = _prep_in_weights(w_in[layer])
        wk, wv = _prep_kv_weights(mla_w_ukv[layer])
        g_mix = norm_mix_g[layer][None, :]
        kvg = mla_kv_norm_g[layer][None, :]
        p_l = _in_projection(x, mod_l[0], mod_l[1], g_mix, w_cols, kvg, wk, wv, lat_tabs)
        p_c = _in_projection(xc, mod_c[0], mod_c[1], g_mix, w_cols, kvg, wk, wv, ctx_tabs)
        mq, mk, mv, sq, sk, sv, nq, nk, nv, rq, rk, rv, gf, gb = p_l
        cmq, cmk, cmv, csq, csk, csv, cnq, cnk, cnv, crq, crk, crv, cgf, cgb = p_c

        sink_rows = jnp.broadcast_to(swa_sink[layer].astype(F32)[slot_heads][:, None], (4, LANES))
        bias = _na_bias_tables(na_rpb[layer], rows)
        dec_f, dec_b = pair_lanes(ret_decay_f[layer]), pair_lanes(ret_decay_b[layer])
        gn_f, gn_b = ret_gn_f[layer][None, :], ret_gn_b[layer][None, :]

        ya = _mla_attention(mq, cmk, cmv, mk, mv)
        yb, yc = _local_attention(sq, sk, sv, csk, csv, sink_rows, nq, nk, nv, cnk, cnv, bias)
        cyf, cyb, s_f, s_b = _retention(crq, crk, crv, cgf, cgb, dec_f, dec_b, gn_f, gn_b, zero_state, zero_state)
        ydf, ydb, _, _ = _retention(rq, rk, rv, gf, gb, dec_f, dec_b, gn_f, gn_b, s_f, s_b)

        beta = mix_beta[layer][out_perm][None, :]
        w_o = w_out[layer][out_perm, :].astype(BF16)
        x = _out_projection(x, ya, yb, yc, ydf, ydb, beta, w_o, mod_l[2])

        g_ffn = norm_ffn_g[layer][None, :]
        wqt = peer_wq[layer].T.astype(BF16)
        k1, k2 = peer_k1[layer].astype(BF16), peer_k2[layer].astype(BF16)
        u, vt = peer_u[layer].astype(BF16), peer_v[layer].T.astype(BF16)
        x = _peer_ffn(x, mod_l[3], mod_l[4], mod_l[5], g_ffn, wqt, k1, k2, u, vt,
                      final_norm_g[None, :] if last else None)
        if with_ctx:
            cya, cyb_, cyc = _ctx_attention(cmq, cmk, cmv, csq, csk, csv, sink_rows, cnq, cnk, cnv)
            xc = _out_projection(xc, cya, cyb_, cyc, cyf, cyb, beta, w_o, mod_c[2])
            xc = _peer_ffn(xc, mod_c[3], mod_c[4], mod_c[5], g_ffn, wqt, k1, k2, u, vt)
    return x
```

```python
import functools

import numpy as np
import jax
import jax.numpy as jnp
from jax import lax
from jax.experimental import pallas as pl
from jax.experimental.pallas import tpu as pltpu

F32 = jnp.float32
BF16 = jnp.bfloat16

EPS = 1e-6
NEG_INF = -1e30
ROPE_BASE = 10000.0
GRID_W = 64
HEAD_DIM = 64
GROUP_WIDTH = 256
MLA_WIDTH = 512
LANES = 128
V7X_VMEM_BYTES = 64 * 1024 * 1024
MIB = 1024 * 1024

MLA_ROPE = 32
MLA_QK = 96
MLA_RANK = 128
MLA_Q_SPLIT = 4
SWA_WINDOW = 128
NA_KH, NA_KW = 8, 16
NA_ROWS = 4
RET_CHUNK = 256
RET_BATCH = 2
PEER_HEADS = 8
PEER_NKEYS = 128
PEER_TOPK = 16
PEER_TOKENS = 256
PEER_ROWS = 16
PEER_PIECE_ROWS = 2
PEER_SHARE = 2

_MQ, _CKV, _KR, _SQ, _SK, _SV, _NQ, _NK, _NV, _RQ, _RK, _RV, _GF, _GB, _IN_COLS = (
    0, 512, 640, 768, 1024, 1152, 1280, 1536, 1792, 2048, 2304, 2560, 2816, 3072, 3328)


def _cparams(semantics, vmem_bytes):
    limit = int(min(vmem_bytes, V7X_VMEM_BYTES * 7 // 8))
    return pltpu.CompilerParams(dimension_semantics=semantics, vmem_limit_bytes=limit)


def _dot(a, b):
    return jnp.dot(a, b, preferred_element_type=F32)


def _dot_nt(a, b):
    return lax.dot_general(a, b, (((1,), (1,)), ((), ())), preferred_element_type=F32)


def _dot_tn(a, b):
    return lax.dot_general(a, b, (((0,), (0,)), ((), ())), preferred_element_type=F32)


def _low_half(shape):
    return lax.broadcasted_iota(jnp.int32, shape, len(shape) - 1) < HEAD_DIM


def _half_masks():
    lo = _low_half((1, LANES))
    return (jnp.where(lo, 1.0, 0.0).astype(BF16), jnp.where(lo, 0.0, 1.0).astype(BF16))


def _norm_mod(x, g, shift, scale):
    ms = jnp.mean(x * x, axis=-1, keepdims=True)
    return (x * lax.rsqrt(ms + EPS) * g) * (1.0 + scale) + shift


def _rmsnorm(x, g):
    ms = jnp.mean(x * x, axis=-1, keepdims=True)
    return x * lax.rsqrt(ms + EPS) * g


def _ada_kernel(c_ref, w_ref, b_ref, o_ref):
    c = c_ref[...]
    o_ref[0] = _dot(c * jax.nn.sigmoid(c), w_ref[0]) + b_ref[0]


def _ada_modulation(cc, ada_w, ada_b):
    depth, d, width = ada_w.shape
    rows = cc.shape[0]
    tn = 1024
    return pl.pallas_call(
        _ada_kernel,
        out_shape=jax.ShapeDtypeStruct((depth, rows, width), F32),
        grid=(depth, width // tn),
        in_specs=[pl.BlockSpec((rows, d), lambda l, j: (0, 0)),
                  pl.BlockSpec((1, d, tn), lambda l, j: (l, 0, j)),
                  pl.BlockSpec((1, 1, tn), lambda l, j: (l, 0, j))],
        out_specs=pl.BlockSpec((1, rows, tn), lambda l, j: (l, 0, j)),
        compiler_params=_cparams(("arbitrary", "arbitrary"), 2 * d * tn * 4 + 8 * MIB),
    )(cc, ada_w, ada_b.reshape(depth, 1, width))


def _rope(a, cos, sin, half):
    lane = lax.broadcasted_iota(jnp.int32, a.shape, 1)
    first = (lane % (2 * half)) < half
    rot = jnp.where(first, -pltpu.roll(a, LANES - half, 1), pltpu.roll(a, half, 1))
    return a * cos + rot * sin


def _inproj_kernel(x_ref, sh_ref, sc_ref, g_ref, w_ref, kvg_ref, wk_ref, wv_ref,
                   c32_ref, s32_ref, c64_ref, s64_ref,
                   mq_ref, mk_ref, mv_ref, sq_ref, sk_ref, sv_ref, nq_ref, nk_ref, nv_ref,
                   rq_ref, rk_ref, rv_ref, gf_ref, gb_ref):
    h = _norm_mod(x_ref[0], g_ref[...], sh_ref[0], sc_ref[0]).astype(BF16)

    full = _dot(h, w_ref[...])

    def seg(lo, width):
        return full[:, lo:lo + width]

    c32, s32, c64, s64 = c32_ref[...], s32_ref[...], c64_ref[...], s64_ref[...]
    mla_scale = MLA_QK ** -0.5
    head_scale = HEAD_DIM ** -0.5
    for hd in range(4):
        a = seg(_MQ + hd * LANES, LANES)
        mq_ref[0, :, hd * LANES:(hd + 1) * LANES] = (_rope(a, c32, s32, MLA_ROPE // 4) * mla_scale).astype(BF16)
    kvn = _rmsnorm(seg(_CKV, MLA_RANK), kvg_ref[...]).astype(BF16)
    kr = _rope(seg(_KR, LANES), c32, s32, MLA_ROPE // 4).astype(BF16)
    mk_ref[0] = _dot(jnp.concatenate([kvn, kr], axis=1), wk_ref[...]).astype(BF16)
    mv_ref[0] = _dot(kvn, wv_ref[...]).astype(BF16)
    for grp in range(2):
        a = seg(_SQ + grp * LANES, LANES)
        sq_ref[0, :, grp * LANES:(grp + 1) * LANES] = (_rope(a, c64, s64, HEAD_DIM // 4) * head_scale).astype(BF16)
    sk_ref[0] = _rope(seg(_SK, LANES), c64, s64, HEAD_DIM // 4).astype(BF16)
    sv_ref[0] = seg(_SV, LANES).astype(BF16)
    nq_ref[0] = (seg(_NQ, GROUP_WIDTH) * head_scale).astype(BF16)
    nk_ref[0] = seg(_NK, GROUP_WIDTH).astype(BF16)
    nv_ref[0] = seg(_NV, GROUP_WIDTH).astype(BF16)
    rq_ref[0] = seg(_RQ, GROUP_WIDTH).astype(BF16)
    rk_ref[0] = (seg(_RK, GROUP_WIDTH) * head_scale).astype(BF16)
    rv_ref[0] = seg(_RV, GROUP_WIDTH).astype(BF16)
    gf_ref[0] = seg(_GF, GROUP_WIDTH).astype(BF16)
    gb_ref[0] = seg(_GB, GROUP_WIDTH).astype(BF16)


_INPROJ_WIDTHS = (MLA_WIDTH, MLA_WIDTH, GROUP_WIDTH, GROUP_WIDTH, LANES, LANES) + (GROUP_WIDTH,) * 8


def _in_projection(x, shift, scale, g, w, kvg, wk, wv, tabs):
    b, t, d = x.shape
    tm = min(512, t)
    const = lambda shape: pl.BlockSpec(shape, lambda bi, i: (0,) * len(shape))
    tab = pl.BlockSpec((tm, LANES), lambda bi, i: (i, 0))
    vec = pl.BlockSpec((1, 1, d), lambda bi, i: (bi, 0, 0))
    est = 2 * (tm * d * 4 + d * _IN_COLS * 2 + tm * sum(_INPROJ_WIDTHS) * 2 + 4 * tm * LANES * 4) + 12 * MIB
    return pl.pallas_call(
        _inproj_kernel,
        out_shape=[jax.ShapeDtypeStruct((b, t, wd), BF16) for wd in _INPROJ_WIDTHS],
        grid=(b, t // tm),
        in_specs=[pl.BlockSpec((1, tm, d), lambda bi, i: (bi, i, 0)), vec, vec, const((1, d)),
                  const((d, _IN_COLS)), const((1, MLA_RANK)), const((MLA_RANK + LANES, MLA_WIDTH)),
                  const((MLA_RANK, GROUP_WIDTH)), tab, tab, tab, tab],
        out_specs=[pl.BlockSpec((1, tm, wd), lambda bi, i: (bi, i, 0)) for wd in _INPROJ_WIDTHS],
        compiler_params=_cparams(("arbitrary", "arbitrary"), est),
    )(x, shift, scale, g, w, kvg, wk, wv, *tabs)


def _attend_pair(qs, pieces, sinks):
    lo = _low_half((1, LANES))
    m0, m1 = _half_masks()
    outs = []
    for s in (0, 1):
        own, other = (m0, m1) if s == 0 else (m1, m0)
        scores = []
        for ks, _, biases in pieces:
            sc = _dot_nt(qs[s], ks[s])
            if biases[s] is not None:
                sc = sc + biases[s]
            scores.append(sc)
        m = functools.reduce(jnp.maximum, [jnp.max(sc, axis=-1, keepdims=True) for sc in scores])
        if sinks is not None:
            m = jnp.maximum(m, sinks[s])
        o = 0.0
        for sc, (_, v, _) in zip(scores, pieces):
            o = o + _dot(jnp.exp(sc - m).astype(BF16), v * own + other)
        own_lanes = lo if s == 0 else jnp.logical_not(lo)
        den = jnp.where(own_lanes, 1.0, o)
        if sinks is not None:
            den = den + jnp.exp(sinks[s] - m)
        outs.append(o * pltpu.roll(1.0 / den, HEAD_DIM, 1))
    return jnp.where(lo, outs[0], outs[1])


def _mla_kernel(q_ref, kc_ref, vc_ref, kl_ref, vl_ref, o_ref, m_ref, l_ref, acc_ref):
    j = pl.program_id(2)

    @pl.when(j == 0)
    def _():
        m_ref[...] = jnp.full(m_ref.shape, NEG_INF, F32)
        l_ref[...] = jnp.zeros(l_ref.shape, F32)
        acc_ref[...] = jnp.zeros(acc_ref.shape, F32)

    def update(k, v):
        lo = _low_half((1, LANES))
        m0, m1 = _half_masks()
        reps = k.shape[0] // LANES
        tq = q_ref.shape[1]
        sub = tq // MLA_Q_SPLIT
        for part in range(MLA_Q_SPLIT):
            rows = slice(part * sub, (part + 1) * sub)
            for pair in range(2):
                vp = v[:, pair * LANES:(pair + 1) * LANES]
                vals = (vp * m0 + m1, vp * m1 + m0)
                alphas, pvs = [], []
                for s in (0, 1):
                    hd = 2 * pair + s
                    q = q_ref[0, rows, hd * LANES:(hd + 1) * LANES]
                    sc = _dot_nt(q, k[:, hd * LANES:(hd + 1) * LANES])
                    m_prev = m_ref[hd, rows, :]
                    m_new = jnp.maximum(m_prev, jnp.max(sc, axis=-1, keepdims=True))
                    m_ref[hd, rows, :] = m_new
                    alphas.append(jnp.exp(m_prev - m_new))
                    p = jnp.exp(sc - jnp.tile(m_new, (1, reps)))
                    pvs.append(_dot(p.astype(BF16), vals[s]))
                acc_ref[pair, rows, :] = (acc_ref[pair, rows, :] * jnp.where(lo, alphas[0], alphas[1])
                                          + jnp.where(lo, pvs[0], pvs[1]))
                l_ref[pair, rows, :] = (l_ref[pair, rows, :] * jnp.where(lo, alphas[1], alphas[0])
                                        + jnp.where(lo, pvs[1], pvs[0]))

    @pl.when(j == 0)
    def _():
        update(kc_ref[0], vc_ref[0])

    @pl.when(j > 0)
    def _():
        update(kl_ref[0], vl_ref[0])

    @pl.when(j == pl.num_programs(2) - 1)
    def _():
        for pair in range(2):
            inv = pltpu.roll(1.0 / l_ref[pair], HEAD_DIM, 1)
            o_ref[0, :, pair * LANES:(pair + 1) * LANES] = (acc_ref[pair] * inv).astype(BF16)


def _mla_attention(q, kc, vc, kl, vl):
    b, n, _ = q.shape
    c = kc.shape[1]
    tq = min(2048, n)
    tk = min(1024, n)
    kv_width = MLA_WIDTH + GROUP_WIDTH
    est = (2 * 2 * (tq * MLA_WIDTH + (c + tk) * kv_width + tq * GROUP_WIDTH) + 6 * tq * LANES * 4
           + 10 * tq * tk * 4 + 8 * MIB)
    return pl.pallas_call(
        _mla_kernel,
        out_shape=jax.ShapeDtypeStruct((b, n, GROUP_WIDTH), BF16),
        grid=(b, n // tq, 1 + n // tk),
        in_specs=[pl.BlockSpec((1, tq, MLA_WIDTH), lambda bi, i, j: (bi, i, 0)),
                  pl.BlockSpec((1, c, MLA_WIDTH), lambda bi, i, j: (bi, 0, 0)),
                  pl.BlockSpec((1, c, GROUP_WIDTH), lambda bi, i, j: (bi, 0, 0)),
                  pl.BlockSpec((1, tk, MLA_WIDTH), lambda bi, i, j: (bi, jnp.maximum(j - 1, 0), 0)),
                  pl.BlockSpec((1, tk, GROUP_WIDTH), lambda bi, i, j: (bi, jnp.maximum(j - 1, 0), 0))],
        out_specs=pl.BlockSpec((1, tq, GROUP_WIDTH), lambda bi, i, j: (bi, i, 0)),
        scratch_shapes=[pltpu.VMEM((4, tq, LANES), F32), pltpu.VMEM((2, tq, LANES), F32), pltpu.VMEM((2, tq, LANES), F32)],
        compiler_params=_cparams(("arbitrary", "arbitrary", "arbitrary"), est),
    )(q, kc, vc, kl, vl)


def _swa_kernel(q_ref, kp_ref, kcur_ref, kn_ref, vp_ref, vcur_ref, vn_ref, kctx_ref, vctx_ref, sink_ref, o_ref):
    i = pl.program_id(1)
    last = pl.num_programs(1) - 1
    tq = q_ref.shape[1]

    def band_bias(width, offset, edge_penalty):
        r = lax.broadcasted_iota(jnp.int32, (tq, width), 0)
        cidx = lax.broadcasted_iota(jnp.int32, (tq, width), 1)
        rel = cidx + offset - r
        inside = jnp.where(rel >= -SWA_WINDOW, jnp.where(rel <= SWA_WINDOW, 1, 0), 0)
        return jnp.where(inside == 1, edge_penalty, NEG_INF)

    halo = kp_ref.shape[1]
    b_prev = band_bias(halo, -halo, jnp.where(i > 0, 0.0, NEG_INF))
    b_cur = band_bias(tq, 0, 0.0)
    b_next = band_bias(halo, tq, jnp.where(i < last, 0.0, NEG_INF))
    m0, m1 = _half_masks()
    kctx, vctx = kctx_ref[0], vctx_ref[0]
    pieces = [((kp_ref[0],) * 2, vp_ref[0], (b_prev,) * 2),
              ((kcur_ref[0],) * 2, vcur_ref[0], (b_cur,) * 2),
              ((kn_ref[0],) * 2, vn_ref[0], (b_next,) * 2),
              ((kctx,) * 2, vctx, (None, None))]
    for grp in range(2):
        qg = q_ref[0, :, grp * LANES:(grp + 1) * LANES]
        sinks = [sink_ref[2 * grp + s:2 * grp + s + 1, 0:1] for s in (0, 1)]
        o_ref[0, :, grp * LANES:(grp + 1) * LANES] = _attend_pair((qg * m0, qg * m1), pieces, sinks).astype(BF16)


def _na_kernel(q_ref, kp_ref, kcur_ref, kn_ref, vp_ref, vcur_ref, vn_ref, kctx_ref, vctx_ref, bias_ref, o_ref):
    tq = q_ref.shape[1]
    m0, m1 = _half_masks()
    for pair in range(2):
        sl = slice(pair * LANES, (pair + 1) * LANES)
        qp = q_ref[0, :, sl]
        pieces = []
        for idx, (kr, vr) in enumerate(((kp_ref, vp_ref), (kcur_ref, vcur_ref), (kn_ref, vn_ref))):
            biases = tuple(bias_ref[0, 2 * pair + s, :, idx * tq:(idx + 1) * tq] for s in (0, 1))
            pieces.append(((kr[0, :, sl],) * 2, vr[0, :, sl], biases))
        pieces.append(((kctx_ref[0, :, sl],) * 2, vctx_ref[0, :, sl], (None, None)))
        o_ref[0, :, sl] = _attend_pair((qp * m0, qp * m1), pieces, None).astype(BF16)


def _na_bias_tables(rpb, rows):
    nb = rows // NA_ROWS
    heads = rpb.shape[0]
    width = 2 * NA_KW - 1
    span = 2 * GRID_W - 1
    lead = GRID_W - 2 - (NA_KW - 1)
    rpb = rpb.astype(F32)
    per_row = jnp.stack([rpb[:, NA_ROWS - 1 - a:NA_ROWS - 1 - a + 3 * NA_ROWS, :] for a in range(NA_ROWS)], axis=1)
    padded = jnp.pad(per_row, ((0, 0), (0, 0), (0, 0), (lead, span - width - lead)))
    tiled = jnp.broadcast_to(padded[:, :, :, None, :], (heads, NA_ROWS, 3 * NA_ROWS, GRID_W, span))
    skew = tiled.reshape(heads, NA_ROWS, 3 * NA_ROWS, GRID_W * span)[..., :GRID_W * (span - 1)]
    toep = skew.reshape(heads, NA_ROWS, 3 * NA_ROWS, GRID_W, span - 1)[..., GRID_W - 2:]
    table = jnp.transpose(toep, (0, 1, 3, 2, 4)).reshape(heads, NA_ROWS * GRID_W, 3 * NA_ROWS * GRID_W)
    a = np.arange(NA_ROWS)[:, None, None, None]
    cq = np.arange(GRID_W)[None, :, None, None]
    kr = np.arange(3 * NA_ROWS)[None, None, :, None]
    ck = np.arange(GRID_W)[None, None, None, :]
    valids = []
    for j in (0, 1, nb - 1):
        r = NA_ROWS * j + a
        rs = np.clip(r - NA_KH // 2, 0, rows - NA_KH)
        rk = NA_ROWS * (j - 1) + kr
        cs = np.clip(cq - NA_KW // 2, 0, GRID_W - NA_KW)
        valid = (rk >= rs) & (rk < rs + NA_KH) & (ck >= cs) & (ck < cs + NA_KW)
        valids.append(valid.reshape(NA_ROWS * GRID_W, 3 * NA_ROWS * GRID_W))
    valid = jnp.asarray(np.stack(valids))
    return jnp.where(valid[:, None], table[None], NEG_INF)


def _local_kernel(*refs):
    _swa_kernel(*refs[:10], refs[20])
    _na_kernel(*refs[10:20], refs[21])


def _local_attention(sq, sk, sv, skc, svc, sink_rows, nq, nk, nv, nkc, nvc, bias):
    b, n, _ = sq.shape
    c = skc.shape[1]
    tq = NA_ROWS * GRID_W
    nb = n // tq
    halo = SWA_WINDOW
    r = tq // halo
    nh = n // halo
    cur = lambda bi, i: (bi, i, 0)
    prev = lambda bi, i: (bi, jnp.maximum(i - 1, 0), 0)
    nxt = lambda bi, i: (bi, jnp.minimum(i + 1, nb - 1), 0)
    hprev = lambda bi, i: (bi, jnp.maximum(i * r - 1, 0), 0)
    hnxt = lambda bi, i: (bi, jnp.minimum(i * r + r, nh - 1), 0)
    ctx = lambda bi, i: (bi, 0, 0)
    variant = lambda bi, i: (jnp.where(i == 0, 0, jnp.where(i == nb - 1, 2, 1)), 0, 0, 0)
    blk = pl.BlockSpec((1, tq, GROUP_WIDTH), cur)
    halo_specs = [pl.BlockSpec((1, halo, LANES), hprev), pl.BlockSpec((1, tq, LANES), cur),
                  pl.BlockSpec((1, halo, LANES), hnxt)]
    win_specs = [pl.BlockSpec((1, tq, GROUP_WIDTH), prev), blk, pl.BlockSpec((1, tq, GROUP_WIDTH), nxt)]
    in_specs = ([blk] + halo_specs + halo_specs + [pl.BlockSpec((1, c, LANES), ctx)] * 2
                + [pl.BlockSpec((4, LANES), lambda bi, i: (0, 0))]
                + [blk] + win_specs + win_specs + [pl.BlockSpec((1, c, GROUP_WIDTH), ctx)] * 2
                + [pl.BlockSpec((1, 4, tq, 3 * tq), variant)])
    est = 2 * 4 * tq * 3 * tq * 4 + 32 * tq * (3 * tq + c) * 4 + 8 * MIB
    return pl.pallas_call(
        _local_kernel,
        out_shape=[jax.ShapeDtypeStruct((b, n, GROUP_WIDTH), BF16)] * 2,
        grid=(b, nb),
        in_specs=in_specs,
        out_specs=[blk, blk],
        compiler_params=_cparams(("arbitrary", "arbitrary"), est),
    )(sq, sk, sk, sk, sv, sv, sv, skc, svc, sink_rows, nq, nk, nk, nk, nv, nv, nv, nkc, nvc, bias)


def _ctx_attn_kernel(mq_ref, mk_ref, mv_ref, sq_ref, sk_ref, sv_ref, sink_ref, nq_ref, nk_ref, nv_ref,
                     ya_ref, yb_ref, yc_ref):
    m0, m1 = _half_masks()
    none2 = (None, None)
    for pair in range(2):
        sl = slice(pair * LANES, (pair + 1) * LANES)
        h0 = slice(2 * pair * LANES, (2 * pair + 1) * LANES)
        h1 = slice((2 * pair + 1) * LANES, (2 * pair + 2) * LANES)
        ya_ref[0, :, sl] = _attend_pair((mq_ref[0, :, h0], mq_ref[0, :, h1]),
                                        [((mk_ref[0, :, h0], mk_ref[0, :, h1]), mv_ref[0, :, sl], none2)],
                                        None).astype(BF16)
        qg = sq_ref[0, :, sl]
        sinks = [sink_ref[2 * pair + s:2 * pair + s + 1, 0:1] for s in (0, 1)]
        yb_ref[0, :, sl] = _attend_pair((qg * m0, qg * m1), [((sk_ref[0],) * 2, sv_ref[0], none2)],
                                        sinks).astype(BF16)
        qn = nq_ref[0, :, sl]
        yc_ref[0, :, sl] = _attend_pair((qn * m0, qn * m1), [((nk_ref[0, :, sl],) * 2, nv_ref[0, :, sl], none2)],
                                        None).astype(BF16)


def _ctx_attention(mq, mk, mv, sq, sk, sv, sink_rows, nq, nk, nv):
    b, c, _ = mq.shape
    spec = lambda wd: pl.BlockSpec((1, c, wd), lambda bi: (bi, 0, 0))
    return pl.pallas_call(
        _ctx_attn_kernel,
        out_shape=[jax.ShapeDtypeStruct((b, c, GROUP_WIDTH), BF16)] * 3,
        grid=(b,),
        in_specs=[spec(MLA_WIDTH), spec(MLA_WIDTH), spec(GROUP_WIDTH), spec(GROUP_WIDTH), spec(LANES), spec(LANES),
                  pl.BlockSpec((4, LANES), lambda bi: (0, 0)), spec(GROUP_WIDTH), spec(GROUP_WIDTH), spec(GROUP_WIDTH)],
        out_specs=[spec(GROUP_WIDTH)] * 3,
        compiler_params=_cparams(("arbitrary",), 24 * MIB),
    )(mq, mk, mv, sq, sk, sv, sink_rows, nq, nk, nv)


def _ret_tables(dec_lane, reverse, c):
    lg = -(jnp.maximum(-dec_lane, 0.0) + jnp.log1p(jnp.exp(-jnp.abs(dec_lane))))
    ti = lax.broadcasted_iota(jnp.int32, (c, LANES), 0).astype(F32)
    if reverse:
        qpow, kpow = c - ti, ti
    else:
        qpow, kpow = ti + 1.0, c - 1.0 - ti
    ii = lax.broadcasted_iota(jnp.int32, (c, c), 0)
    jj = lax.broadcasted_iota(jnp.int32, (c, c), 1)
    dist = (jj - ii) if reverse else (ii - jj)
    distf = jnp.maximum(dist, 0).astype(F32)
    masks = [jnp.where(dist >= 0, jnp.exp(lg[:, s * HEAD_DIM:s * HEAD_DIM + 1] * distf), 0.0) for s in (0, 1)]
    return jnp.exp(lg * qpow), jnp.exp(lg * kpow), masks[0], masks[1], jnp.exp(lg * float(c))


def _ret_chunk(q, k, v, g, qdec, kdec, decays, cdec, gn_lane, s_ref):
    lo = _low_half((1, LANES))
    ii = lax.broadcasted_iota(jnp.int32, (LANES, LANES), 0)
    jj = lax.broadcasted_iota(jnp.int32, (LANES, LANES), 1)
    outs = []
    for msk, decay in zip(_half_masks(), decays):
        inner = _dot_nt(q * msk, k) * decay
        outs.append(_dot(inner.astype(BF16), v))
    state = s_ref[...]
    o = jnp.where(lo, outs[0], outs[1]) + _dot((q.astype(F32) * qdec).astype(BF16), state.astype(BF16))
    kd = (k.astype(F32) * kdec).astype(BF16)
    same_head = (ii < HEAD_DIM) == (jj < HEAD_DIM)
    s_ref[...] = state * cdec + jnp.where(same_head, _dot_tn(kd, v), 0.0)

    def head_mean(x):
        s_lo = jnp.sum(jnp.where(lo, x, 0.0), axis=-1, keepdims=True)
        s_hi = jnp.sum(jnp.where(lo, 0.0, x), axis=-1, keepdims=True)
        return jnp.where(lo, s_lo, s_hi) * (1.0 / HEAD_DIM)

    dev = o - head_mean(o)
    normed = dev * lax.rsqrt(head_mean(dev * dev) + EPS)
    gf = g.astype(F32)
    return (gf * jax.nn.sigmoid(gf)) * (normed * gn_lane)


def _ret_kernel(decf_ref, decb_ref, gnf_ref, gnb_ref, s0f_ref, s0b_ref,
                qf_ref, kf_ref, vf_ref, gf_ref, qb_ref, kb_ref, vb_ref, gb_ref,
                yf_ref, yb_ref, sf_ref, sb_ref, st_scr, tab_scr, mask_scr, cdec_scr):
    t = pl.program_id(1)
    directions = ((decf_ref, gnf_ref, qf_ref, kf_ref, vf_ref, gf_ref, yf_ref),
                  (decb_ref, gnb_ref, qb_ref, kb_ref, vb_ref, gb_ref, yb_ref))

    batch = qf_ref.shape[0]

    @pl.when(t == 0)
    def _():
        for e in range(batch):
            st_scr[e, 0] = s0f_ref[e]
            st_scr[e, 1] = s0b_ref[e]
        for pair in range(2):
            for d, refs in enumerate(directions):
                qdec, kdec, mask0, mask1, cdec = _ret_tables(refs[0][pair], d == 1, qf_ref.shape[1])
                tab_scr[d, pair, 0] = qdec
                tab_scr[d, pair, 1] = kdec
                mask_scr[d, pair, 0] = mask0
                mask_scr[d, pair, 1] = mask1
                cdec_scr[d, pair] = jnp.broadcast_to(cdec, (8, LANES))

    for e in range(batch):
        for d, (_, gn_ref, q_ref, k_ref, v_ref, g_ref, y_ref) in enumerate(directions):
            for pair in range(2):
                sl = slice(pair * LANES, (pair + 1) * LANES)
                y = _ret_chunk(q_ref[e, :, sl], k_ref[e, :, sl], v_ref[e, :, sl], g_ref[e, :, sl],
                               tab_scr[d, pair, 0], tab_scr[d, pair, 1], (mask_scr[d, pair, 0], mask_scr[d, pair, 1]),
                               cdec_scr[d, pair, 0:1, :], gn_ref[:, sl], st_scr.at[e, d, pair])
                y_ref[e, :, sl] = y.astype(BF16)

    @pl.when(t == pl.num_programs(1) - 1)
    def _():
        for e in range(batch):
            sf_ref[e] = st_scr[e, 0]
            sb_ref[e] = st_scr[e, 1]


def _retention(q, k, v, gf, gb, dec_f, dec_b, gn_f, gn_b, s0f, s0b):
    b, t, width = q.shape
    c = min(RET_CHUNK, t)
    nt = t // c
    bb = RET_BATCH if b % RET_BATCH == 0 else 1
    fwd = pl.BlockSpec((bb, c, width), lambda bi, i: (bi, i, 0))
    bwd = pl.BlockSpec((bb, c, width), lambda bi, i: (bi, nt - 1 - i, 0))
    dec = pl.BlockSpec((2, 1, LANES), lambda bi, i: (0, 0, 0))
    gn = pl.BlockSpec((1, width), lambda bi, i: (0, 0))
    st = pl.BlockSpec((bb, 2, LANES, LANES), lambda bi, i: (bi, 0, 0, 0))
    return pl.pallas_call(
        _ret_kernel,
        out_shape=[jax.ShapeDtypeStruct((b, t, width), BF16)] * 2 + [jax.ShapeDtypeStruct((b, 2, LANES, LANES), F32)] * 2,
        grid=(b // bb, nt),
        in_specs=[dec, dec, gn, gn, st, st, fwd, fwd, fwd, fwd, bwd, bwd, bwd, bwd],
        out_specs=[fwd, bwd, st, st],
        scratch_shapes=[pltpu.VMEM((bb, 2, 2, LANES, LANES), F32), pltpu.VMEM((2, 2, 2, c, LANES), F32),
                        pltpu.VMEM((2, 2, 2, c, c), F32),
                        pltpu.VMEM((2, 2, 8, LANES), F32)],
        compiler_params=_cparams(("arbitrary", "arbitrary"), 24 * MIB),
    )(dec_f, dec_b, gn_f, gn_b, s0f, s0b, q, k, v, gf, q, k, v, gb)


def _outproj_kernel(x_ref, ya_ref, yb_ref, yc_ref, ydf_ref, ydb_ref, beta_ref, w_ref, gate_ref, o_ref):
    ys = (ya_ref[0].astype(F32), yb_ref[0].astype(F32), yc_ref[0].astype(F32),
          ydf_ref[0].astype(F32) + ydb_ref[0].astype(F32))
    acc = 0.0
    for s, y in enumerate(ys):
        sl = slice(s * GROUP_WIDTH, (s + 1) * GROUP_WIDTH)
        acc = acc + _dot((y * beta_ref[:, sl]).astype(BF16), w_ref[sl, :])
    o_ref[0] = x_ref[0] + gate_ref[0] * acc


def _out_projection(x, ya, yb, yc, ydf, ydb, beta, w, gate):
    b, t, d = x.shape
    tm = min(512, t)
    row = lambda wd: pl.BlockSpec((1, tm, wd), lambda bi, i: (bi, i, 0))
    est = 2 * (2 * tm * d * 4 + 5 * tm * GROUP_WIDTH * 2 + d * d * 2) + 8 * MIB
    return pl.pallas_call(
        _outproj_kernel,
        out_shape=jax.ShapeDtypeStruct((b, t, d), F32),
        grid=(b, t // tm),
        in_specs=[row(d), row(GROUP_WIDTH), row(GROUP_WIDTH), row(GROUP_WIDTH), row(GROUP_WIDTH), row(GROUP_WIDTH),
                  pl.BlockSpec((1, d), lambda bi, i: (0, 0)), pl.BlockSpec((d, d), lambda bi, i: (0, 0)),
                  pl.BlockSpec((1, 1, d), lambda bi, i: (bi, 0, 0))],
        out_specs=row(d),
        compiler_params=_cparams(("arbitrary", "arbitrary"), est),
    )(x, ya, yb, yc, ydf, ydb, beta, w, gate)


def _sorting_network(n):
    size = 1
    while size < n:
        size *= 2
    pairs = []
    p = 1
    while p < size:
        k = p
        while k >= 1:
            for j in range(k % p, size - k, 2 * k):
                for i in range(min(k, size - j - k)):
                    if (i + j) // (2 * p) == (i + j + k) // (2 * p):
                        pairs.append((i + j, i + j + k))
            k //= 2
        p *= 2
    return [(i, j) for i, j in pairs if j < n]


def _top_values(scores, out_ref, count):
    groups = scores.shape[0] // 8
    lists = [scores[8 * g:8 * g + 8, :] for g in range(groups)]
    for i, j in _sorting_network(groups):
        lists[i], lists[j] = jnp.maximum(lists[i], lists[j]), jnp.minimum(lists[i], lists[j])
    for r in range(count):
        m = jnp.max(lists[0], axis=0, keepdims=True)
        out_ref[r:r + 1, :] = m
        hit = lists[0] == m
        need = count - r - 1
        for p in range(min(groups - 1, need)):
            lists[p] = jnp.where(hit, lists[p + 1], lists[p])
        if need >= groups:
            lists[groups - 1] = jnp.where(hit, -jnp.inf, lists[groups - 1])


def _staircase(w1, w2):
    row8 = lax.broadcasted_iota(jnp.int32, (8, w1.shape[1]), 0)
    cands = [w1[0:1] + w2]
    for a in range(1, 8):
        cands.append(jnp.where(row8 < PEER_TOPK // (a + 1), w1[a:a + 1] + w2[0:8], -jnp.inf))
    cands.append(w1[8:16] + w2[0:1])
    return jnp.concatenate(cands, axis=0)


def _bf16_twice(x):
    hi = pltpu.bitcast(x.astype(BF16).astype(F32), jnp.uint32)
    return hi | (hi >> 16)


def _bf16_rows(words, rows):
    return pltpu.bitcast(jnp.broadcast_to(words, (rows // 2, words.shape[1])), BF16)


def _peer_route(hd, q_scr, k1_ref, k2_ref, rk_scr, e2_scr, n_scr, e1_scr, v1_scr, v2_scr, top_scr):
    half = PEER_NKEYS
    base = pl.multiple_of(hd * 2 * half, 2 * half)
    s1 = _dot(k1_ref[...], q_scr[pl.ds(base, half), :])
    s2 = _dot(k2_ref[...], q_scr[pl.ds(base + half, half), :])
    _top_values(s1, v1_scr, PEER_TOPK)
    _top_values(s2, v2_scr, PEER_TOPK)
    v1, v2 = v1_scr[...], v2_scr[...]
    cand = _staircase(v1, v2)
    _top_values(cand, top_scr, PEER_TOPK)
    top = top_scr[...]
    z = jnp.sum(jnp.exp(top - top[0:1]), axis=0, keepdims=True)
    sel = jnp.where(cand >= top[PEER_TOPK - 1:PEER_TOPK], 1.0, 0.0)
    count = lambda picked: jnp.sum(picked, axis=0, keepdims=True)
    lens = [count(sel[0:16])] + [count(sel[8 + 8 * a:16 + 8 * a]) for a in range(1, 8)]
    lens += [sel[72 + a:73 + a] for a in range(8)]
    n_sel = jnp.zeros(s1.shape, F32)
    for a in reversed(range(PEER_TOPK)):
        n_sel = jnp.where(s1 >= v1[a:a + 1], lens[a], n_sel)
    rank = jnp.zeros(s2.shape, F32)
    for b in range(PEER_TOPK):
        rank = rank + jnp.where(s2 < v2[b:b + 1], 1.0, 0.0)
    rk_scr[hd] = rank.astype(BF16)
    n_scr[hd] = _bf16_twice(n_sel)
    e1_scr[hd] = _bf16_twice(jnp.exp(s1 - v1[0:1]) / z)
    e2_scr[hd] = (jnp.exp(s2 - v2[0:1]) * 0.5).astype(BF16)


def _peer_kernel(*refs, final_norm):
    if final_norm:
        (x_ref, sh_ref, sc_ref, gt_ref, g_ref, wqt_ref, k1_ref, k2_ref, u_ref, vt_ref, fg_ref, o_ref,
         ht_all, q_scr, rk_all, e2_all, n_all, e1_all, v1_scr, v2_scr, top_scr, a0_all, a1_all, acc_all) = refs
    else:
        (x_ref, sh_ref, sc_ref, gt_ref, g_ref, wqt_ref, k1_ref, k2_ref, u_ref, vt_ref, o_ref,
         ht_all, q_scr, rk_all, e2_all, n_all, e1_all, v1_scr, v2_scr, top_scr, a0_all, a1_all, acc_all) = refs
        fg_ref = None
    step = pl.program_id(2)
    sub = pl.program_id(3)
    ht_scr, rk_scr, e2_scr, n_scr, e1_scr, acc_scr = (r.at[sub] for r in (ht_all, rk_all, e2_all, n_all, e1_all, acc_all))
    tokens = ht_all.shape[2]
    tok_rows = pl.ds(pl.multiple_of(sub * tokens, tokens), tokens)

    @pl.when(step == 0)
    def _():
        h = _norm_mod(x_ref[0, tok_rows, :], g_ref[...], sh_ref[0], sc_ref[0])
        ht = jnp.transpose(h).astype(BF16)
        ht_scr[...] = ht
        q_scr[...] = _dot(wqt_ref[...], ht).astype(BF16)
        acc_scr[...] = jnp.zeros(acc_scr.shape, F32)

        def route(pair, carry):
            for k in range(2):
                _peer_route(2 * pair + k, q_scr, k1_ref, k2_ref, rk_scr, e2_scr, n_scr, e1_scr,
                            v1_scr.at[k], v2_scr.at[k], top_scr.at[k])
            return carry

        lax.fori_loop(0, PEER_HEADS // 2, route, 0)

    def project(a_write):
        a_write[...] = _dot(u_ref[...], ht_scr[...])

    def consume(a_read):
        y = None
        piece = PEER_PIECE_ROWS * PEER_NKEYS
        for kt in range(PEER_ROWS // PEER_PIECE_ROWS):
            ws = []
            for r in range(PEER_PIECE_ROWS * kt, PEER_PIECE_ROWS * (kt + 1)):
                rows = slice(r * PEER_NKEYS, (r + 1) * PEER_NKEYS)
                i = (step - 1) * PEER_ROWS + r
                gate = jnp.zeros((PEER_NKEYS, tokens), BF16)
                for hd in range(PEER_HEADS):
                    n_row = _bf16_rows(n_scr[hd, pl.ds(i, 1), :], PEER_NKEYS)
                    e1_row = _bf16_rows(e1_scr[hd, pl.ds(i, 1), :], PEER_NKEYS)
                    gate = gate + jnp.where(rk_scr[hd] < n_row, e2_scr[hd] * e1_row, jnp.zeros_like(gate))
                a = a_read[rows, :].astype(BF16)
                gelu2 = a * (1.0 + lax.erf(a * np.sqrt(0.5).astype(BF16)))
                ws.append(gelu2 * gate)
            part = _dot(vt_ref[:, kt * piece:(kt + 1) * piece], jnp.concatenate(ws, axis=0))
            y = part if y is None else y + part
        acc_scr[...] += y

    last = pl.num_programs(2) - 1
    bufs = (a0_all.at[sub], a1_all.at[sub])

    @pl.when(step == 0)
    def _():
        project(bufs[0])

    for parity in (0, 1):
        @pl.when(jnp.logical_and(jnp.logical_and(step > 0, step < last), step % 2 == parity))
        def _():
            project(bufs[parity])
            consume(bufs[1 - parity])

        @pl.when(jnp.logical_and(step == last, step % 2 == parity))
        def _():
            consume(bufs[1 - parity])

    @pl.when(step == last)
    def _():
        out = x_ref[0, tok_rows, :] + gt_ref[0] * jnp.transpose(acc_scr[...])
        if final_norm:
            out = _rmsnorm(out, fg_ref[...])
        o_ref[0, tok_rows, :] = out


def _peer_ffn(x, shift, scale, gate, g, wqt, k1, k2, u, vt, final_g=None):
    b, t, d = x.shape
    tt = PEER_TOKENS
    share = PEER_SHARE if t % (PEER_SHARE * tt) == 0 else 1
    ec = PEER_ROWS * PEER_NKEYS
    n_blocks = u.shape[0] // ec
    nq = wqt.shape[0]
    final_norm = final_g is not None
    tok = pl.BlockSpec((1, share * tt, d), lambda bi, i, s, k: (bi, i, 0))
    vec = pl.BlockSpec((1, 1, d), lambda bi, i, s, k: (bi, 0, 0))
    const = lambda shape: pl.BlockSpec(shape, lambda bi, i, s, k: (0,) * len(shape))
    in_specs = [tok, vec, vec, vec, const((1, d)), const((nq, d)), const(k1.shape), const(k2.shape),
                pl.BlockSpec((ec, d), lambda bi, i, s, k: (jnp.minimum(s, n_blocks - 1), 0)),
                pl.BlockSpec((d, ec), lambda bi, i, s, k: (0, jnp.clip(s - 1, 0, n_blocks - 1)))]
    args = [x, shift, scale, gate, g, wqt, k1, k2, u, vt]
    if final_norm:
        in_specs.append(const((1, d)))
        args.append(final_g)
    table = pltpu.VMEM((share, PEER_HEADS, PEER_NKEYS, tt), jnp.uint32)
    table16 = pltpu.VMEM((share, PEER_HEADS, PEER_NKEYS, tt), BF16)
    top = pltpu.VMEM((2, PEER_TOPK, tt), F32)
    a_buf = pltpu.VMEM((share, ec, tt), F32)
    scratch = [pltpu.VMEM((share, d, tt), BF16), pltpu.VMEM((nq, tt), BF16), table16, table16, table, table,
               top, top, top, a_buf, a_buf, pltpu.VMEM((share, d, tt), F32)]
    est = (4 * share * tt * d * 4 + 2 * nq * d * 2 + 4 * ec * d * 2 + nq * tt * 2
           + share * (tt * d * 2 + 12 * PEER_HEADS * PEER_NKEYS * tt + d * tt * 4 + 2 * ec * tt * 4) + 8 * MIB)
    return pl.pallas_call(
        functools.partial(_peer_kernel, final_norm=final_norm),
        out_shape=jax.ShapeDtypeStruct((b, t, d), F32),
        grid=(b, t // (share * tt), n_blocks + 1, share),
        in_specs=in_specs,
        out_specs=tok,
        scratch_shapes=scratch,
        compiler_params=_cparams(("arbitrary",) * 4, est),
    )(*args)


def _prep_in_weights(w_in):
    d = w_in.shape[0]
    zeros = lambda n: jnp.zeros((d, n), w_in.dtype)
    cols = []
    for hd in range(4):
        cols += [w_in[:, hd * MLA_QK:(hd + 1) * MLA_QK], zeros(LANES - MLA_QK)]
    cols.append(w_in[:, 384:512])
    cols += [zeros(HEAD_DIM), w_in[:, 512:544], zeros(LANES - HEAD_DIM - MLA_ROPE)]
    for hd in (0, 2, 1, 3):
        cols.append(w_in[:, 544 + hd * HEAD_DIM:544 + (hd + 1) * HEAD_DIM])
    cols.append(w_in[:, 800:])
    return jnp.concatenate(cols, axis=1).astype(BF16)


def _prep_kv_weights(w_ukv):
    zeros = jnp.zeros((MLA_RANK, HEAD_DIM), w_ukv.dtype)
    wkn = jnp.concatenate([blk for hd in range(4) for blk in (w_ukv[:, hd * LANES:hd * LANES + HEAD_DIM], zeros)], axis=1)
    wv = jnp.concatenate([w_ukv[:, hd * LANES + HEAD_DIM:(hd + 1) * LANES] for hd in range(4)], axis=1)
    sel = np.zeros((LANES, 4 * LANES), np.float32)
    for hd in range(4):
        for r in range(MLA_ROPE):
            sel[HEAD_DIM + r, hd * LANES + HEAD_DIM + r] = 1.0
    return jnp.concatenate([wkn.astype(BF16), jnp.asarray(sel, BF16)], axis=0), wv.astype(BF16)


def _rope_lane_tables(n):
    def tables(rot_dim):
        quarter = rot_dim // 4
        inv = ROPE_BASE ** (-jnp.arange(quarter, dtype=F32) / quarter)
        t = jnp.arange(n, dtype=jnp.int32)
        pos = jnp.stack([t // GRID_W, t % GRID_W], axis=-1).astype(F32)
        ang = pos[:, :, None] * inv
        lanes = lambda tb: jnp.concatenate([tb[:, 0], tb[:, 0], tb[:, 1], tb[:, 1]], axis=-1)
        return lanes(jnp.cos(ang)), lanes(jnp.sin(ang))

    c32, s32 = tables(MLA_ROPE)
    c64, s64 = tables(HEAD_DIM)
    ones = lambda w: jnp.ones((n, w), F32)
    zeros = lambda w: jnp.zeros((n, w), F32)
    return (jnp.concatenate([ones(HEAD_DIM), c32, ones(LANES - HEAD_DIM - MLA_ROPE)], axis=-1),
            jnp.concatenate([zeros(HEAD_DIM), s32, zeros(LANES - HEAD_DIM - MLA_ROPE)], axis=-1),
            jnp.concatenate([c64, c64], axis=-1), jnp.concatenate([s64, s64], axis=-1))


_SWA_SLOT_HEADS = (0, 2, 1, 3)


def kernel(x, c, ctx, c_ctx, ada_w, ada_b, norm_mix_g, w_in, mla_kv_norm_g, mla_w_ukv, swa_sink, na_rpb, ret_decay_f, ret_decay_b, ret_gn_f, ret_gn_b, mix_beta, w_out, norm_ffn_g, peer_wq, peer_k1, peer_k2, peer_u, peer_v, final_norm_g):
    b, n, d = x.shape
    n_ctx = ctx.shape[1]
    depth = ada_w.shape[0]
    rows = n // GRID_W

    pad_rows = -(b + 1) % 8
    cc = jnp.concatenate([c, c_ctx[None, :], jnp.zeros((pad_rows, d), F32)], axis=0)
    mod = _ada_modulation(cc, ada_w, ada_b)

    lat_tabs = _rope_lane_tables(n)
    ctx_tabs = (jnp.ones((n_ctx, LANES), F32), jnp.zeros((n_ctx, LANES), F32)) * 2
    slot_heads = np.asarray(_SWA_SLOT_HEADS)
    out_perm = np.arange(d)
    out_perm[GROUP_WIDTH:2 * GROUP_WIDTH] = GROUP_WIDTH + (slot_heads[:, None] * HEAD_DIM + np.arange(HEAD_DIM)[None, :]).reshape(-1)
    zero_state = jnp.zeros((b, 2, LANES, LANES), F32)
    pair_lanes = lambda p: jnp.repeat(p.astype(F32), HEAD_DIM).reshape(2, 1, LANES)

    xc = ctx
    for layer in range(depth):
        with_ctx = layer < depth - 1
        last = layer == depth - 1
        chunk = lambda k, lo, hi: mod[layer, lo:hi, k * d:(k + 1) * d][:, None, :]
        mod_l = [chunk(k, 0, b) for k in range(6)]
        mod_c = [jnp.broadcast_to(chunk(k, b, b + 1), (b, 1, d)) for k in range(6)]

        w_cols = _prep_in_weights(w_in[layer])
        wk, wv = _prep_kv_weights(mla_w_ukv[layer])
        g_mix = norm_mix_g[layer][None, :]
        kvg = mla_kv_norm_g[layer][None, :]
        p_l = _in_projection(x, mod_l[0], mod_l[1], g_mix, w_cols, kvg, wk, wv, lat_tabs)
        p_c = _in_projection(xc, mod_c[0], mod_c[1], g_mix, w_cols, kvg, wk, wv, ctx_tabs)
        mq, mk, mv, sq, sk, sv, nq, nk, nv, rq, rk, rv, gf, gb = p_l
        cmq, cmk, cmv, csq, csk, csv, cnq, cnk, cnv, crq, crk, crv, cgf, cgb = p_c

        sink_rows = jnp.broadcast_to(swa_sink[layer].astype(F32)[slot_heads][:, None], (4, LANES))
        bias = _na_bias_tables(na_rpb[layer], rows)
        dec_f, dec_b = pair_lanes(ret_decay_f[layer]), pair_lanes(ret_decay_b[layer])
        gn_f, gn_b = ret_gn_f[layer][None, :], ret_gn_b[layer][None, :]

        ya = _mla_attention(mq, cmk, cmv, mk, mv)
        yb, yc = _local_attention(sq, sk, sv, csk, csv, sink_rows, nq, nk, nv, cnk, cnv, bias)
        cyf, cyb, s_f, s_b = _retention(crq, crk, crv, cgf, cgb, dec_f, dec_b, gn_f, gn_b, zero_state, zero_state)
        ydf, ydb, _, _ = _retention(rq, rk, rv, gf, gb, dec_f, dec_b, gn_f, gn_b, s_f, s_b)

        beta = mix_beta[layer][out_perm][None, :]
        w_o = w_out[layer][out_perm, :].astype(BF16)
        x = _out_projection(x, ya, yb, yc, ydf, ydb, beta, w_o, mod_l[2])

        g_ffn = norm_ffn_g[layer][None, :]
        wqt = peer_wq[layer].T.astype(BF16)
        k1, k2 = peer_k1[layer].astype(BF16), peer_k2[layer].astype(BF16)
        u, vt = peer_u[layer].astype(BF16), peer_v[layer].T.astype(BF16)
        x = _peer_ffn(x, mod_l[3], mod_l[4], mod_l[5], g_ffn, wqt, k1, k2, u, vt,
                      final_norm_g[None, :] if last else None)
        if with_ctx:
            cya, cyb_, cyc = _ctx_attention(cmq, cmk, cmv, csq, csk, csv, sink_rows, cnq, cnk, cnv)
            xc = _out_projection(xc, cya, cyb_, cyc, cyf, cyb, beta, w_o, mod_c[2])
            xc = _peer_ffn(xc, mod_c[3], mod_c[4], mod_c[5], g_ffn, wqt, k1, k2, u, vt)
    return x
```
